```python
import jax
import jax.numpy as jnp
from jax import lax
import numpy as np

D_MODEL = 1024
BATCH = 8
SEQ = 8192
DEPTH = 1
DEC_BATCH = 128
DEC_SEQ = 4
PAST_LEN = 8192
PAGE_SIZE = 128

HEAD_DIM = 64
NSA_HEADS = 8
NSA_KV_HEADS = 2
NSA_GROUP = NSA_HEADS // NSA_KV_HEADS
SB_HEADS = 8
NSA_WIDTH = NSA_HEADS * HEAD_DIM
SB_WIDTH = SB_HEADS * HEAD_DIM
MIX_WIDTH = NSA_WIDTH + SB_WIDTH
KV_COLS = 2 * NSA_KV_HEADS * HEAD_DIM
IN_COLS = NSA_WIDTH + 3 * KV_COLS + 3 * NSA_HEADS + 3 * SB_WIDTH
CMP_BLOCK = 64
CMP_HIDDEN = 128
N_SELECT = 16
WINDOW = 512
Q_BLOCK = 128
D_FF = ((8 * D_MODEL + 3 * 256 - 1) // (3 * 256)) * 256
EPS = 1e-6
NEG = -1e30

kernel_name = "hymba_nsa_stickbreaking_decode_step"


def rmsnorm(x, g):
    xf = x.astype(jnp.float32)
    y = xf * lax.rsqrt(jnp.mean(xf * xf, axis=-1, keepdims=True) + EPS)
    return (y * g.astype(jnp.float32)).astype(x.dtype)


def modulate(x, g, shift, scale):
    return rmsnorm(x, g) * (1.0 + scale) + shift


def alibi_slopes():
    h = jnp.arange(1, NSA_HEADS + 1, dtype=jnp.float32)
    return jnp.exp2(-8.0 * h / NSA_HEADS).reshape(NSA_KV_HEADS, NSA_GROUP)


def masked_softmax(s, valid):
    p = jax.nn.softmax(jnp.where(valid, s, NEG), axis=-1)
    return jnp.where(valid, p, 0.0)


def project(h, w_in):
    B, T = h.shape[:2]
    cuts = [NSA_WIDTH, NSA_WIDTH + KV_COLS, NSA_WIDTH + 2 * KV_COLS, NSA_WIDTH + 3 * KV_COLS,
            NSA_WIDTH + 3 * KV_COLS + 3 * NSA_HEADS]
    q, kc, ks, kw, g, sb = jnp.split(h @ w_in, cuts, axis=-1)
    kv_shape = (B, T, 2, NSA_KV_HEADS, HEAD_DIM)
    sb = sb.reshape(B, T, 3, SB_HEADS, HEAD_DIM)
    return (q.reshape(B, T, NSA_HEADS, HEAD_DIM), kc.reshape(kv_shape), ks.reshape(kv_shape),
            kw.reshape(kv_shape), jax.nn.sigmoid(g.reshape(B, T, 3, NSA_HEADS)),
            sb[:, :, 0], sb[:, :, 1:])


def compress_blocks(k, pe, w1, w2):
    B, L = k.shape[:2]
    nb = L // CMP_BLOCK
    blk = k.reshape(B, nb, CMP_BLOCK, NSA_KV_HEADS, HEAD_DIM) + pe[:, None, :]
    flat = jnp.moveaxis(blk, 3, 2).reshape(B, nb, NSA_KV_HEADS, CMP_BLOCK * HEAD_DIM)
    return jax.nn.gelu(flat @ w1) @ w2


def nsa_compress(kv, lw):
    kc = compress_blocks(kv[:, :, 0], lw[5], lw[6], lw[7])
    vc = compress_blocks(kv[:, :, 1], lw[8], lw[9], lw[10])
    return kc, vc


def nsa_cmp_sel(q, pos_q, kc, vc, ks, vs, slopes):
    B, Tq = q.shape[:2]
    nb = kc.shape[1]
    scale = HEAD_DIM ** -0.5
    qg = q.reshape(B, Tq, NSA_KV_HEADS, NSA_GROUP, HEAD_DIM)
    blk = jnp.arange(nb, dtype=jnp.int32)
    blk_end = (blk + 1) * CMP_BLOCK - 1
    dist_c = (pos_q[:, None] - blk_end[None, :]).astype(jnp.float32)
    s_c = jnp.einsum('btgrd,bngd->bgrtn', qg, kc).astype(jnp.float32) * scale
    s_c = s_c - slopes[:, :, None, None] * dist_c
    p_c = masked_softmax(s_c, dist_c >= 0)
    o_cmp = jnp.einsum('bgrtn,bngd->btgrd', p_c.astype(vc.dtype), vc)
    cur = pos_q // CMP_BLOCK
    imp = p_c.sum(axis=2)
    forced = (blk[None, :] == 0) | (blk[None, :] == cur[:, None]) | (blk[None, :] == cur[:, None] - 1)
    imp = jnp.where(forced, jnp.inf, imp)
    imp = jnp.where(blk[None, :] <= cur[:, None], imp, -jnp.inf)
    n_sel = min(N_SELECT, nb)
    _, idx = lax.top_k(imp, n_sel)
    ks_b = jnp.moveaxis(ks.reshape(B, nb, CMP_BLOCK, NSA_KV_HEADS, HEAD_DIM), 3, 1)
    vs_b = jnp.moveaxis(vs.reshape(B, nb, CMP_BLOCK, NSA_KV_HEADS, HEAD_DIM), 3, 1)
    bi = jnp.arange(B)[:, None, None, None]
    gi = jnp.arange(NSA_KV_HEADS)[None, :, None, None]
    k_sel = ks_b[bi, gi, idx]
    v_sel = vs_b[bi, gi, idx].reshape(B, NSA_KV_HEADS, Tq, n_sel * CMP_BLOCK, HEAD_DIM)
    tok = idx[..., None] * CMP_BLOCK + jnp.arange(CMP_BLOCK, dtype=jnp.int32)
    dist_s = (pos_q[None, None, :, None, None] - tok).astype(jnp.float32)
    s_s = jnp.einsum('btgrd,bgtnkd->bgrtnk', qg, k_sel).astype(jnp.float32) * scale
    s_s = (s_s - slopes[:, :, None, None, None] * dist_s[:, :, None]).reshape(
        B, NSA_KV_HEADS, NSA_GROUP, Tq, n_sel * CMP_BLOCK)
    valid_s = (dist_s >= 0)[:, :, None].reshape(B, NSA_KV_HEADS, 1, Tq, n_sel * CMP_BLOCK)
    p_s = masked_softmax(s_s, valid_s)
    o_sel = jnp.einsum('bgrtk,bgtkd->btgrd', p_s.astype(v_sel.dtype), v_sel)
    return o_cmp.reshape(B, Tq, NSA_HEADS, HEAD_DIM), o_sel.reshape(B, Tq, NSA_HEADS, HEAD_DIM)


def window_attend(q, pos_q, k, v, pos_k, slopes):
    B, Tq = q.shape[:2]
    qg = q.reshape(B, Tq, NSA_KV_HEADS, NSA_GROUP, HEAD_DIM)
    dist = pos_q[:, None] - pos_k[None, :]
    valid = (dist >= 0) & (dist <= WINDOW) & (pos_k[None, :] >= 0)
    s = jnp.einsum('btgrd,bkgd->bgrtk', qg, k).astype(jnp.float32) * (HEAD_DIM ** -0.5)
    s = s - slopes[:, :, None, None] * dist.astype(jnp.float32)
    p = masked_softmax(s, valid)
    o = jnp.einsum('bgrtk,bkgd->btgrd', p.astype(v.dtype), v)
    return o.reshape(B, Tq, NSA_HEADS, HEAD_DIM)


def stick_breaking(q, pos_q, k, v, pos_k):
    z = jnp.einsum('bthd,bkhd->bhtk', q, k).astype(jnp.float32) * (HEAD_DIM ** -0.5)
    valid = pos_k[None, :] < pos_q[:, None]
    c = jnp.where(valid, jax.nn.log_sigmoid(-z), 0.0)
    suffix = lax.cumsum(c, axis=3, reverse=True) - c
    a = jnp.where(valid, jnp.exp(jax.nn.log_sigmoid(z) + suffix), 0.0)
    return jnp.einsum('bhtk,bkhd->bthd', a.astype(v.dtype), v)


def attn_input(x, c, lw):
    mods = jnp.split((jax.nn.silu(c) @ lw[0] + lw[1])[:, None, :], 6, axis=-1)
    h = modulate(x, lw[2], mods[0], mods[1])
    return project(h, lw[4]), mods


def layer_output(x, o_cmp, o_sel, o_win, gates, o_sb, mods, lw):
    norm_ffn, norm_out_nsa, norm_out_sb, w_out = lw[3], lw[11], lw[12], lw[13]
    w_gate, w_up, w_down = lw[14], lw[15], lw[16]
    B, T = x.shape[:2]
    g = gates[..., None]
    o_nsa = g[:, :, 0] * o_cmp + g[:, :, 1] * o_sel + g[:, :, 2] * o_win
    cat = jnp.concatenate([rmsnorm(o_nsa.reshape(B, T, NSA_WIDTH), norm_out_nsa),
                           rmsnorm(o_sb.reshape(B, T, SB_WIDTH), norm_out_sb)], axis=-1)
    x = x + mods[2] * (cat @ w_out)
    h = modulate(x, norm_ffn, mods[3], mods[4])
    return x + mods[5] * ((jax.nn.silu(h @ w_gate) * (h @ w_up)) @ w_down)


def prompt_layer(x, c, lw, slopes):
    B, T = x.shape[:2]
    (q_nsa, kv_cmp, kv_sel, kv_win, gates, q_sb, kv_sb), mods = attn_input(x, c, lw)
    kc, vc = nsa_compress(kv_cmp, lw)
    ks, vs = kv_sel[:, :, 0], kv_sel[:, :, 1]
    k_sb, v_sb = kv_sb[:, :, 0], kv_sb[:, :, 1]
    pos_all = jnp.arange(T, dtype=jnp.int32)
    kw_pad = jnp.pad(kv_win, ((0, 0), (WINDOW, 0), (0, 0), (0, 0), (0, 0)))

    def query_block(b):
        start = b * Q_BLOCK
        pos_q = start + jnp.arange(Q_BLOCK, dtype=jnp.int32)
        q_b = lax.dynamic_slice_in_dim(q_nsa, start, Q_BLOCK, axis=1)
        o_c, o_s = nsa_cmp_sel(q_b, pos_q, kc, vc, ks, vs, slopes)
        kw = lax.dynamic_slice_in_dim(kw_pad, start, WINDOW + Q_BLOCK, axis=1)
        pos_k = start - WINDOW + jnp.arange(WINDOW + Q_BLOCK, dtype=jnp.int32)
        o_w = window_attend(q_b, pos_q, kw[:, :, 0], kw[:, :, 1], pos_k, slopes)
        qs_b = lax.dynamic_slice_in_dim(q_sb, start, Q_BLOCK, axis=1)
        o_b = stick_breaking(qs_b, pos_q, k_sb, v_sb, pos_all)
        return o_c, o_s, o_w, o_b

    blocks = lax.map(query_block, jnp.arange(T // Q_BLOCK, dtype=jnp.int32))
    o_cmp, o_sel, o_win, o_sb = [jnp.moveaxis(o, 0, 1).reshape(B, T, -1, HEAD_DIM) for o in blocks]
    y = layer_output(x, o_cmp, o_sel, o_win, gates, o_sb, mods, lw)
    win = kv_win[:, T - min(WINDOW, T):]
    return y, kv_cmp, kv_sel, kv_sb, win


def sample_layer(x, c, cache_cmp, cache_sel, cache_sb, win_buf, page_table, lw, slopes):
    B, Tn = x.shape[:2]
    P = page_table.shape[1] * PAGE_SIZE
    L = P + Tn
    L_pad = -(-L // CMP_BLOCK) * CMP_BLOCK
    (q_nsa, kv_cmp, kv_sel, kv_win, gates, q_sb, kv_sb), mods = attn_input(x, c, lw)
    pos_q = P + jnp.arange(Tn, dtype=jnp.int32)

    def gather_past(cache, new, total):
        past = cache[page_table].reshape((B, P) + cache.shape[2:])
        pad = jnp.zeros((B, total - L) + cache.shape[2:], new.dtype)
        return jnp.concatenate([past, new, pad], axis=1)

    all_cmp = gather_past(cache_cmp, kv_cmp, L_pad)
    all_sel = gather_past(cache_sel, kv_sel, L_pad)
    all_sb = gather_past(cache_sb, kv_sb, L)
    kc, vc = nsa_compress(all_cmp, lw)
    o_cmp, o_sel = nsa_cmp_sel(q_nsa, pos_q, kc, vc, all_sel[:, :, 0], all_sel[:, :, 1], slopes)
    Lb = win_buf.shape[1]
    all_win = jnp.concatenate([win_buf, kv_win], axis=1)
    pos_w = P - Lb + jnp.arange(Lb + Tn, dtype=jnp.int32)
    o_win = window_attend(q_nsa, pos_q, all_win[:, :, 0], all_win[:, :, 1], pos_w, slopes)
    o_sb = stick_breaking(q_sb, pos_q, all_sb[:, :, 0], all_sb[:, :, 1], jnp.arange(L, dtype=jnp.int32))
    y = layer_output(x, o_cmp, o_sel, o_win, gates, o_sb, mods, lw)
    new_win = all_win[:, Lb + Tn - min(WINDOW, Lb + Tn):]
    return y, kv_cmp, kv_sel, kv_sb, new_win


def setup_inputs(seed: int = 0) -> dict:
    key = jax.random.key(seed)
    ks = jax.random.split(key, 32)
    nrm = jax.random.normal
    f32 = jnp.float32
    n_pages = PAST_LEN // PAGE_SIZE
    n_used = DEC_BATCH * n_pages
    n_phys = n_used + max(1, n_used // 4)
    page_table = jax.random.permutation(ks[0], n_phys)[:n_used].reshape(DEC_BATCH, n_pages).astype(jnp.int32)
    kv_nsa = (DEPTH, n_phys, PAGE_SIZE, 2, NSA_KV_HEADS, HEAD_DIM)
    fan_cmp = CMP_BLOCK * HEAD_DIM
    return {
        "x_prompt": nrm(ks[1], (BATCH, SEQ, D_MODEL), f32),
        "x_sample": nrm(ks[2], (DEC_BATCH, DEC_SEQ, D_MODEL), f32),
        "cache_nsa_cmp": nrm(ks[3], kv_nsa, f32),
        "cache_nsa_sel": nrm(ks[4], kv_nsa, f32),
        "cache_sb": nrm(ks[5], (DEPTH, n_phys, PAGE_SIZE, 2, SB_HEADS, HEAD_DIM), f32),
        "state_nsa_win": nrm(ks[6], (DEPTH, DEC_BATCH, min(WINDOW, PAST_LEN), 2, NSA_KV_HEADS, HEAD_DIM), f32),
        "page_table": page_table,
        "c_prompt": nrm(ks[7], (BATCH, D_MODEL), f32),
        "c_sample": nrm(ks[8], (DEC_BATCH, D_MODEL), f32),
        "w_ada": nrm(ks[9], (DEPTH, D_MODEL, 6 * D_MODEL), f32) * (0.5 * D_MODEL ** -0.5),
        "b_ada": 0.02 * nrm(ks[10], (DEPTH, 6 * D_MODEL), f32),
        "norm_attn": 1.0 + 0.02 * nrm(ks[11], (DEPTH, D_MODEL), f32),
        "norm_ffn": 1.0 + 0.02 * nrm(ks[12], (DEPTH, D_MODEL), f32),
        "w_in": nrm(ks[13], (DEPTH, D_MODEL, IN_COLS), f32) * D_MODEL ** -0.5,
        "cmp_pe_k": 0.02 * nrm(ks[14], (DEPTH, CMP_BLOCK, HEAD_DIM), f32),
        "cmp_k_w1": nrm(ks[15], (DEPTH, fan_cmp, CMP_HIDDEN), f32) * fan_cmp ** -0.5,
        "cmp_k_w2": nrm(ks[16], (DEPTH, CMP_HIDDEN, HEAD_DIM), f32) * CMP_HIDDEN ** -0.5,
        "cmp_pe_v": 0.02 * nrm(ks[17], (DEPTH, CMP_BLOCK, HEAD_DIM), f32),
        "cmp_v_w1": nrm(ks[18], (DEPTH, fan_cmp, CMP_HIDDEN), f32) * fan_cmp ** -0.5,
        "cmp_v_w2": nrm(ks[19], (DEPTH, CMP_HIDDEN, HEAD_DIM), f32) * CMP_HIDDEN ** -0.5,
        "norm_out_nsa": 1.0 + 0.02 * nrm(ks[20], (DEPTH, NSA_WIDTH), f32),
        "norm_out_sb": 1.0 + 0.02 * nrm(ks[21], (DEPTH, SB_WIDTH), f32),
        "w_out": nrm(ks[22], (DEPTH, MIX_WIDTH, D_MODEL), f32) * MIX_WIDTH ** -0.5,
        "w_gate": nrm(ks[23], (DEPTH, D_MODEL, D_FF), f32) * D_MODEL ** -0.5,
        "w_up": nrm(ks[24], (DEPTH, D_MODEL, D_FF), f32) * D_MODEL ** -0.5,
        "w_down": nrm(ks[25], (DEPTH, D_FF, D_MODEL), f32) * D_FF ** -0.5,
        "norm_final": 1.0 + 0.02 * nrm(ks[26], (D_MODEL,), f32),
    }


def reference(x_prompt, x_sample, cache_nsa_cmp, cache_nsa_sel, cache_sb, state_nsa_win, page_table,
              c_prompt, c_sample, w_ada, b_ada, norm_attn, norm_ffn, w_in, cmp_pe_k, cmp_k_w1, cmp_k_w2,
              cmp_pe_v, cmp_v_w1, cmp_v_w2, norm_out_nsa, norm_out_sb, w_out, w_gate, w_up, w_down,
              norm_final):
    slopes = alibi_slopes()
    xp, xs = x_prompt, x_sample
    p_cmp, p_sel, p_sb, p_win = [], [], [], []
    s_cmp, s_sel, s_sb, s_win = [], [], [], []
    for l in range(DEPTH):
        lw = (w_ada[l], b_ada[l], norm_attn[l], norm_ffn[l], w_in[l], cmp_pe_k[l], cmp_k_w1[l], cmp_k_w2[l],
              cmp_pe_v[l], cmp_v_w1[l], cmp_v_w2[l], norm_out_nsa[l], norm_out_sb[l], w_out[l],
              w_gate[l], w_up[l], w_down[l])
        xp, a_cmp, a_sel, a_sb, a_win = prompt_layer(xp, c_prompt, lw, slopes)
        xs, b_cmp, b_sel, b_sb, b_win = sample_layer(xs, c_sample, cache_nsa_cmp[l], cache_nsa_sel[l],
                                                     cache_sb[l], state_nsa_win[l], page_table, lw, slopes)
        p_cmp.append(a_cmp)
        p_sel.append(a_sel)
        p_sb.append(a_sb)
        p_win.append(a_win)
        s_cmp.append(b_cmp)
        s_sel.append(b_sel)
        s_sb.append(b_sb)
        s_win.append(b_win)
    y_prompt = rmsnorm(xp, norm_final)
    y_sample = rmsnorm(xs, norm_final)
    new_cmp_prompt = jnp.stack(p_cmp)
    new_sel_prompt = jnp.stack(p_sel)
    new_sb_prompt = jnp.stack(p_sb)
    new_win_prompt = jnp.stack(p_win)
    new_cmp_sample = jnp.stack(s_cmp)
    new_sel_sample = jnp.stack(s_sel)
    new_sb_sample = jnp.stack(s_sb)
    new_win_sample = jnp.stack(s_win)
    return (y_prompt, y_sample, new_cmp_prompt, new_sel_prompt, new_sb_prompt, new_win_prompt,
            new_cmp_sample, new_sel_sample, new_sb_sample, new_win_sample)
```

```python
import functools

import jax
import jax.numpy as jnp
from jax import lax
from jax.experimental import pallas as pl
from jax.experimental.pallas import tpu as pltpu

HEAD_DIM = 64
NSA_HEADS = 8
NSA_KV_HEADS = 2
NSA_GROUP = NSA_HEADS // NSA_KV_HEADS
SB_HEADS = 8
NSA_WIDTH = NSA_HEADS * HEAD_DIM
SB_WIDTH = SB_HEADS * HEAD_DIM
KV_COLS = 2 * NSA_KV_HEADS * HEAD_DIM
CMP_BLOCK = 64
CMP_HIDDEN = 128
N_SELECT = 16
WINDOW = 512
EPS = 1e-6
NEG = -1e30
MASK_BIAS = -1e9
LANES = 128
Q_TILE = 128
K_TILE = 128
QK_SCALE = HEAD_DIM ** -0.5
VMEM_LIMIT = 56 * 1024 * 1024

f32 = jnp.float32
bf16 = jnp.bfloat16


def _dot(a, b):
    return jnp.dot(a, b, preferred_element_type=f32)


def _dot_nt(a, b):
    return lax.dot_general(a, b, (((1,), (1,)), ((), ())), preferred_element_type=f32)


def _dot_tn(a, b):
    return lax.dot_general(a, b, (((0,), (0,)), ((), ())), preferred_element_type=f32)


def _split(x):
    hi = x.astype(bf16)
    lo = (x - hi.astype(f32)).astype(bf16)
    return hi, lo


def _rms(x, g):
    return x * lax.rsqrt(jnp.mean(x * x, axis=-1, keepdims=True) + EPS) * g


def _params(n_axes, vmem=VMEM_LIMIT):
    return pltpu.CompilerParams(dimension_semantics=("arbitrary",) * n_axes, vmem_limit_bytes=vmem)


def _mods_kernel(c_ref, w_ref, b_ref, o_ref):
    c = c_ref[...]
    a = c * (1.0 / (1.0 + jnp.exp(-c)))
    a_hi, a_lo = _split(a)
    w_hi, w_lo = _split(w_ref[...])
    o_ref[...] = _dot(a_hi, w_hi) + _dot(a_lo, w_hi) + _dot(a_hi, w_lo) + b_ref[...]


def _mods_call(c_all, w_ada, b_ada):
    m, d = c_all.shape
    n = w_ada.shape[1]
    tn = 1536 if n % 1536 == 0 else n
    return pl.pallas_call(
        _mods_kernel,
        grid=(n // tn,),
        in_specs=[pl.BlockSpec((m, d), lambda j: (0, 0)),
                  pl.BlockSpec((d, tn), lambda j: (0, j)),
                  pl.BlockSpec((1, tn), lambda j: (0, j))],
        out_specs=pl.BlockSpec((m, tn), lambda j: (0, j)),
        out_shape=jax.ShapeDtypeStruct((m, n), f32),
        compiler_params=_params(1),
        name="mods",
    )(c_all, w_ada, b_ada.reshape(1, n))


_C_Q, _C_KC, _C_KS, _C_KW, _C_QSB, _C_KVSB, _C_G, _C_END = 0, 512, 768, 1024, 1280, 1792, 2816, 2944


def _proj_kernel(x_ref, sh_ref, sc_ref, g_ref, w_ref,
                 kvc_ref, kvs_ref, kvw_ref, kvsb_ref, gate_ref,
                 qn_ref, qsb_ref, ksel_ref, kwin_ref, ksb_ref):
    x = x_ref[0]
    h = _rms(x, g_ref[...]) * (1.0 + sc_ref[0]) + sh_ref[0]
    r = _dot(h.astype(bf16), w_ref[...])
    kvc_ref[0] = r[:, _C_KC:_C_KS]
    kvs_ref[0] = r[:, _C_KS:_C_KW]
    kvw_ref[0] = r[:, _C_KW:_C_QSB]
    kvsb_ref[0] = r[:, _C_KVSB:_C_G]
    gate_ref[0] = 1.0 / (1.0 + jnp.exp(-r[:, _C_G:_C_END]))
    for h_i in range(NSA_HEADS):
        lo = _C_Q + h_i * HEAD_DIM
        qn_ref[0, h_i] = (r[:, lo:lo + HEAD_DIM] * QK_SCALE).astype(bf16)
    for h_i in range(SB_HEADS):
        lo = _C_QSB + h_i * HEAD_DIM
        qsb_ref[0, h_i] = (r[:, lo:lo + HEAD_DIM] * QK_SCALE).astype(bf16)
    for j in range(2 * NSA_KV_HEADS):
        lo = _C_KS + j * HEAD_DIM
        ksel_ref[0, j] = r[:, lo:lo + HEAD_DIM].astype(bf16)
        lo = _C_KW + j * HEAD_DIM
        kwin_ref[0, j] = r[:, lo:lo + HEAD_DIM].astype(bf16)
    for j in range(2 * SB_HEADS):
        lo = _C_KVSB + j * HEAD_DIM
        ksb_ref[0, j] = r[:, lo:lo + HEAD_DIM].astype(bf16)


def _proj_call(x, shift, scale, g, w_cat, tm):
    b, t, d = x.shape
    rows = shift.shape[1]
    mod_spec = (pl.BlockSpec((1, 1, d), lambda i, j: (i, 0, 0)) if rows == 1
                else pl.BlockSpec((1, tm, d), lambda i, j: (i, j, 0)))

    def flat(c):
        return pl.BlockSpec((1, tm, c), lambda i, j: (i, j, 0))

    def heads(n):
        return pl.BlockSpec((1, n, tm, HEAD_DIM), lambda i, j: (i, 0, j, 0))

    out_shape = (
        jax.ShapeDtypeStruct((b, t, KV_COLS), f32),
        jax.ShapeDtypeStruct((b, t, KV_COLS), f32),
        jax.ShapeDtypeStruct((b, t, KV_COLS), f32),
        jax.ShapeDtypeStruct((b, t, 2 * SB_WIDTH), f32),
        jax.ShapeDtypeStruct((b, t, LANES), f32),
        jax.ShapeDtypeStruct((b, NSA_HEADS, t, HEAD_DIM), bf16),
        jax.ShapeDtypeStruct((b, SB_HEADS, t, HEAD_DIM), bf16),
        jax.ShapeDtypeStruct((b, 2 * NSA_KV_HEADS, t, HEAD_DIM), bf16),
        jax.ShapeDtypeStruct((b, 2 * NSA_KV_HEADS, t, HEAD_DIM), bf16),
        jax.ShapeDtypeStruct((b, 2 * SB_HEADS, t, HEAD_DIM), bf16),
    )
    out_specs = (flat(KV_COLS), flat(KV_COLS), flat(KV_COLS), flat(2 * SB_WIDTH), flat(LANES),
                 heads(NSA_HEADS), heads(SB_HEADS), heads(2 * NSA_KV_HEADS), heads(2 * NSA_KV_HEADS),
                 heads(2 * SB_HEADS))
    return pl.pallas_call(
        _proj_kernel,
        grid=(b, t // tm),
        in_specs=[flat(d), mod_spec, mod_spec,
                  pl.BlockSpec((1, d), lambda i, j: (0, 0)),
                  pl.BlockSpec(w_cat.shape, lambda i, j: (0, 0))],
        out_specs=out_specs,
        out_shape=out_shape,
        compiler_params=_params(2),
        name="proj",
    )(x, shift, scale, g.reshape(1, d), w_cat)


def _gelu_tanh(x):
    return 0.5 * x * (1.0 + jnp.tanh(0.7978845608028654 * (x + 0.044715 * (x * x * x))))


def _compress_kernel(x_ref, pe_ref, w1_ref, w2_ref, o_ref, acc_ref):
    k = pl.program_id(1)

    @pl.when(k == 0)
    def _():
        acc_ref[...] = jnp.zeros_like(acc_ref)

    acc_ref[...] += _dot((x_ref[...] + pe_ref[...]).astype(bf16), w1_ref[...])

    @pl.when(k == pl.num_programs(1) - 1)
    def _():
        o_ref[...] = _dot(_gelu_tanh(acc_ref[...]).astype(bf16), w2_ref[...])


def _compress_call(x_rows, pe_row, w1_big, w2_big):
    m, kdim = x_rows.shape
    tm = 256 if m % 256 == 0 else m
    tk = 4096
    n1 = w1_big.shape[1]
    n2 = w2_big.shape[1]
    return pl.pallas_call(
        _compress_kernel,
        grid=(m // tm, kdim // tk),
        in_specs=[pl.BlockSpec((tm, tk), lambda i, k: (i, k)),
                  pl.BlockSpec((1, tk), lambda i, k: (0, k)),
                  pl.BlockSpec((tk, n1), lambda i, k: (k, 0)),
                  pl.BlockSpec((n1, n2), lambda i, k: (0, 0))],
        out_specs=pl.BlockSpec((tm, n2), lambda i, k: (i, 0)),
        out_shape=jax.ShapeDtypeStruct((m, n2), f32),
        scratch_shapes=[pltpu.VMEM((tm, n1), f32)],
        compiler_params=_params(2),
        name="compress",
    )(x_rows, pe_row, w1_big, w2_big)


def _compress_weights(pe_k, w1_k, w2_k, pe_v, w1_v, w2_v):
    g = NSA_KV_HEADS
    w1 = jnp.zeros((CMP_BLOCK, 2, g, HEAD_DIM, 2, g, CMP_HIDDEN), f32)
    w2 = jnp.zeros((2, g, CMP_HIDDEN, 2, g, HEAD_DIM), f32)
    for kv, (a1, a2) in enumerate(((w1_k, w2_k), (w1_v, w2_v))):
        a1 = a1.reshape(CMP_BLOCK, HEAD_DIM, CMP_HIDDEN)
        for gi in range(g):
            w1 = w1.at[:, kv, gi, :, kv, gi, :].set(a1)
            w2 = w2.at[kv, gi, :, kv, gi, :].set(a2)
    w1 = w1.reshape(CMP_BLOCK * 2 * g * HEAD_DIM, 2 * g * CMP_HIDDEN).astype(bf16)
    w2 = w2.reshape(2 * g * CMP_HIDDEN, 2 * g * HEAD_DIM).astype(bf16)
    pe = jnp.stack([jnp.broadcast_to(pe_k[:, None, :], (CMP_BLOCK, g, HEAD_DIM)),
                    jnp.broadcast_to(pe_v[:, None, :], (CMP_BLOCK, g, HEAD_DIM))], axis=1)
    return pe.reshape(1, -1), w1, w2


def _gather_kernel(pt_ref, c1_ref, c2_ref, c3_ref, o1_ref, o2_ref, o3_ref, sem):
    b = pl.program_id(0)
    n_pages = pt_ref.shape[1]
    page = c1_ref.shape[1]
    pairs = ((c1_ref, o1_ref), (c2_ref, o2_ref), (c3_ref, o3_ref))

    def copy(i, pg, p):
        src, dst = pairs[i]
        return pltpu.make_async_copy(src.at[pg], dst.at[b, pl.ds(p * page, page)], sem.at[i])

    def issue(p, carry):
        pg = pt_ref[b, p]
        for i in range(3):
            copy(i, pg, p).start()
        return carry

    lax.fori_loop(0, n_pages, issue, 0)

    def drain(p, carry):
        for i in range(3):
            copy(i, 0, p).wait()
        return carry

    lax.fori_loop(0, n_pages, drain, 0)


def _gather_call(page_table, c1, c2, c3):
    b, n_pages = page_table.shape
    page = c1.shape[1]
    any_spec = pl.BlockSpec(memory_space=pl.ANY)
    return pl.pallas_call(
        _gather_kernel,
        grid_spec=pltpu.PrefetchScalarGridSpec(
            num_scalar_prefetch=1, grid=(b,),
            in_specs=[any_spec, any_spec, any_spec],
            out_specs=(any_spec, any_spec, any_spec),
            scratch_shapes=[pltpu.SemaphoreType.DMA((3,))]),
        out_shape=tuple(jax.ShapeDtypeStruct((b, n_pages * page, c.shape[2]), c.dtype) for c in (c1, c2, c3)),
        compiler_params=_params(1),
        name="paged_gather",
    )(page_table, c1, c2, c3)


def _select_blocks(imp, n_idx, cur, n_rounds):
    forced = (n_idx == 0) | (n_idx == cur) | (n_idx == cur - 1)
    in_range = n_idx <= cur
    v = jnp.where(forced, 8.0, imp)
    v = jnp.where(in_range, v, -1.0)
    sel = jnp.zeros(imp.shape, f32)
    big = jnp.int32(1 << 20)
    for _ in range(n_rounds):
        mx = jnp.max(v, axis=0, keepdims=True)
        idx = jnp.min(jnp.where(v == mx, n_idx, big), axis=0, keepdims=True)
        hit = n_idx == idx
        sel = jnp.where(hit, 1.0, sel)
        v = jnp.where(hit, -2.0, v)
    return jnp.where(in_range, sel, 0.0)


def _slope(g, r):
    return jnp.where(g == 0, 2.0 ** -(r + 1), 2.0 ** -(NSA_GROUP + r + 1)).astype(f32)


def _cmp_kernel(q_ref, kv_ref, ocmp_ref, selb_ref, any_ref):
    g = pl.program_id(1)
    qi = pl.program_id(2)
    tq = q_ref.shape[2]
    nb = kv_ref.shape[1]
    kvb = kv_ref[0]
    kc = jnp.where(g == 0, kvb[:, 0:HEAD_DIM], kvb[:, HEAD_DIM:2 * HEAD_DIM]).astype(bf16)
    vc = jnp.where(g == 0, kvb[:, 2 * HEAD_DIM:3 * HEAD_DIM], kvb[:, 3 * HEAD_DIM:4 * HEAD_DIM]).astype(bf16)
    n_idx = lax.broadcasted_iota(jnp.int32, (nb, tq), 0)
    pos = qi * tq + lax.broadcasted_iota(jnp.int32, (nb, tq), 1)
    dist_i = pos - (n_idx * CMP_BLOCK + (CMP_BLOCK - 1))
    valid = dist_i >= 0
    dist = dist_i.astype(f32)
    imp = jnp.zeros((nb, tq), f32)
    outs = []
    for r in range(NSA_GROUP):
        s = _dot_nt(kc, q_ref[0, r]) - _slope(g, r) * dist
        s = jnp.where(valid, s, NEG)
        e = jnp.exp(s - jnp.max(s, axis=0, keepdims=True))
        p = jnp.where(valid, e / jnp.sum(e, axis=0, keepdims=True), 0.0)
        imp = imp + p
        outs.append(_dot_tn(p.astype(bf16), vc))
    ocmp_ref[0] = jnp.concatenate(outs, axis=1)
    cur = pos // CMP_BLOCK
    sel = _select_blocks(imp, n_idx, cur, min(N_SELECT, nb))
    sel_t = sel.T
    selb_ref[0, 0] = jnp.where(sel_t > 0.5, 0.0, MASK_BIAS).astype(bf16)
    any_ref[0, 0, 0] = jnp.max(sel_t, axis=0, keepdims=True).astype(jnp.int32)


def _cmp_call(q_hm, kcvc):
    b, _, t, _ = q_hm.shape
    nb = kcvc.shape[1]
    tq = Q_TILE
    nq = t // tq
    return pl.pallas_call(
        _cmp_kernel,
        grid=(b, NSA_KV_HEADS, nq),
        in_specs=[pl.BlockSpec((1, NSA_GROUP, tq, HEAD_DIM), lambda i, g, j: (i, g, j, 0)),
                  pl.BlockSpec((1, nb, KV_COLS), lambda i, g, j: (i, 0, 0))],
        out_specs=(pl.BlockSpec((1, tq, NSA_GROUP * HEAD_DIM), lambda i, g, j: (i, j, g)),
                   pl.BlockSpec((1, 1, tq, nb), lambda i, g, j: (i, g, j, 0)),
                   pl.BlockSpec((1, 1, 1, 1, nb), lambda i, g, j: (i, g, j, 0, 0))),
        out_shape=(jax.ShapeDtypeStruct((b, t, NSA_WIDTH), f32),
                   jax.ShapeDtypeStruct((b, NSA_KV_HEADS, t, nb), bf16),
                   jax.ShapeDtypeStruct((b, NSA_KV_HEADS, nq, 1, nb), jnp.int32)),
        compiler_params=_params(3),
        name="cmp_select",
    )(q_hm, kcvc)


def _softmax_tile(qa, k_tile, v_tile, carry, mask):
    m, l, acc = carry
    s = _dot_nt(qa, k_tile)
    if mask is not None:
        s = jnp.where(mask, s, MASK_BIAS)
    m_new = jnp.maximum(m, jnp.max(s, axis=1, keepdims=True))
    alpha = jnp.exp(m - m_new)
    p = jnp.exp(s - m_new)
    l = alpha * l + jnp.sum(p, axis=1, keepdims=True)
    acc = alpha * acc + _dot(p.astype(bf16), v_tile)
    return m_new, l, acc


def _selwin_kernel(flag_ref, q_ref, qal_ref, selb_ref, ks_ref, vs_ref, kw_ref, vw_ref, kconst_ref,
                   osel_ref, owin_ref, kaug_s, kaug_w, qaug):
    b = pl.program_id(0)
    g = pl.program_id(1)
    qi = pl.program_id(2)
    tq = q_ref.shape[2]
    tk = K_TILE
    rows = NSA_GROUP * tq

    @pl.when(qi == 0)
    def _():
        kaug_s[...] = kconst_ref[...]
        kaug_s[:, 0:HEAD_DIM] = ks_ref[0, 0]
        kaug_w[...] = kconst_ref[:, 0:LANES]
        kaug_w[:, 0:HEAD_DIM] = kw_ref[0, 0]

    qaug[:, 0:HEAD_DIM] = q_ref[0].reshape(rows, HEAD_DIM)
    qaug[:, HEAD_DIM:LANES] = qal_ref[0].reshape(rows, HEAD_DIM)
    sb = selb_ref[0, 0]
    for r in range(NSA_GROUP):
        qaug[r * tq:(r + 1) * tq, LANES:] = sb
    qa = qaug[...]
    qa_w = qaug[:, 0:LANES]

    t_idx = lax.broadcasted_iota(jnp.int32, (rows, tk), 0) & (tq - 1)
    k_idx = lax.broadcasted_iota(jnp.int32, (rows, tk), 1)
    causal = k_idx <= t_idx
    init = (jnp.full((rows, 1), NEG, f32), jnp.zeros((rows, 1), f32), jnp.zeros((rows, HEAD_DIM), f32))

    def sel_body(j, carry):
        n_tiles = ks_ref.shape[2] // tk
        n_words = max(1, n_tiles // 32)
        word = flag_ref[((b * NSA_KV_HEADS + g) * (ks_ref.shape[2] // tq) + qi) * n_words + (j >> 5)]
        hit = ((word >> (j & 31)) & 1) == 1
        off = pl.multiple_of(j * tk, tk)

        def run(c):
            return _softmax_tile(qa, kaug_s[pl.ds(off, tk), :], vs_ref[0, 0, pl.ds(off, tk), :], c, None)

        return lax.cond(hit, run, lambda c: c, carry)

    carry = lax.fori_loop(0, qi, sel_body, init)
    off_d = pl.multiple_of(qi * tk, tk)
    m, l, acc = _softmax_tile(qa, kaug_s[pl.ds(off_d, tk), :], vs_ref[0, 0, pl.ds(off_d, tk), :], carry, causal)
    o = acc / l
    osel_ref[0] = jnp.concatenate([o[r * tq:(r + 1) * tq] for r in range(NSA_GROUP)], axis=1)

    n_back = WINDOW // tk
    carry = init
    for i in range(n_back):
        jt = qi - n_back + i
        mask = (k_idx >= t_idx) if i == 0 else None

        def run_w(c, jt=jt, mask=mask):
            off = pl.multiple_of(jt * tk, tk)
            return _softmax_tile(qa_w, kaug_w[pl.ds(off, tk), :], vw_ref[0, 0, pl.ds(off, tk), :], c, mask)

        carry = lax.cond(jt >= 0, run_w, lambda c: c, carry)
    m, l, acc = _softmax_tile(qa_w, kaug_w[pl.ds(off_d, tk), :], vw_ref[0, 0, pl.ds(off_d, tk), :], carry, causal)
    o = acc / l
    owin_ref[0] = jnp.concatenate([o[r * tq:(r + 1) * tq] for r in range(NSA_GROUP)], axis=1)


def _selwin_call(flags, q_hm, qal, selb, ksel_hm, kwin_hm, kconst):
    b, _, t, _ = q_hm.shape
    tq = Q_TILE
    nq = t // tq
    ka = kconst.shape[1]
    g_n = NSA_KV_HEADS

    def kv_spec(off):
        return pl.BlockSpec((1, 1, t, HEAD_DIM), lambda i, g, j, fl: (i, off + g, 0, 0))

    out_spec = pl.BlockSpec((1, tq, NSA_GROUP * HEAD_DIM), lambda i, g, j, fl: (i, j, g))
    return pl.pallas_call(
        _selwin_kernel,
        grid_spec=pltpu.PrefetchScalarGridSpec(
            num_scalar_prefetch=1, grid=(b, g_n, nq),
            in_specs=[pl.BlockSpec((1, NSA_GROUP, tq, HEAD_DIM), lambda i, g, j, fl: (i, g, j, 0)),
                      pl.BlockSpec((1, NSA_GROUP, tq, HEAD_DIM), lambda i, g, j, fl: (g, 0, j, 0)),
                      pl.BlockSpec((1, 1, tq, ka - LANES), lambda i, g, j, fl: (i, g, j, 0)),
                      kv_spec(0), kv_spec(g_n), kv_spec(0), kv_spec(g_n),
                      pl.BlockSpec((t, ka), lambda i, g, j, fl: (0, 0))],
            out_specs=(out_spec, out_spec),
            scratch_shapes=[pltpu.VMEM((t, ka), bf16), pltpu.VMEM((t, LANES), bf16),
                            pltpu.VMEM((NSA_GROUP * tq, ka), bf16)]),
        out_shape=(jax.ShapeDtypeStruct((b, t, NSA_WIDTH), f32), jax.ShapeDtypeStruct((b, t, NSA_WIDTH), f32)),
        compiler_params=_params(3),
        name="sel_win",
    )(flags, q_hm, qal, selb, ksel_hm, ksel_hm, kwin_hm, kwin_hm, kconst)


def _alibi_key_cols(pos, width):
    cols = jnp.stack([pos // CMP_BLOCK, pos % CMP_BLOCK, jnp.ones_like(pos), jnp.ones_like(pos)], axis=1)
    return jnp.pad(cols.astype(f32), ((0, 0), (0, width - 4))).astype(bf16)


def _alibi_query_cols(slopes, pos, width):
    s = slopes[..., None]
    a = (pos // CMP_BLOCK).astype(f32)
    r = (pos % CMP_BLOCK).astype(f32)
    cols = jnp.stack([jnp.broadcast_to(s * CMP_BLOCK, s.shape[:-1] + pos.shape),
                      jnp.broadcast_to(s, s.shape[:-1] + pos.shape),
                      -s * CMP_BLOCK * a, -s * r], axis=-1)
    pad = [(0, 0)] * (cols.ndim - 1) + [(0, width - 4)]
    return jnp.pad(cols, pad).astype(bf16)


def _slopes():
    h = jnp.arange(1, NSA_HEADS + 1, dtype=f32)
    return jnp.exp2(-8.0 * h / NSA_HEADS).reshape(NSA_KV_HEADS, NSA_GROUP)


def _neg_softplus(z):
    return -(jnp.maximum(z, 0.0) + jnp.log(1.0 + jnp.exp(-jnp.abs(z))))


def _sb_kernel(q_ref, k_ref, v_ref, tri_ref, o_ref):
    qi = pl.program_id(2)
    tq = q_ref.shape[2]
    tk = K_TILE
    q = q_ref[0, 0]
    tri = tri_ref[...]
    t_idx = lax.broadcasted_iota(jnp.int32, (tq, tk), 0)
    k_idx = lax.broadcasted_iota(jnp.int32, (tq, tk), 1)
    strict = k_idx < t_idx

    def tile(j, carry, acc, mask):
        off = pl.multiple_of(j * tk, tk)
        z = _dot_nt(q, k_ref[0, 0, pl.ds(off, tk), :])
        c = _neg_softplus(z)
        if mask is not None:
            c = jnp.where(mask, c, 0.0)
        c_hi, c_lo = _split(c)
        res = _dot(c_hi, tri) + _dot(c_lo, tri)
        a = jnp.exp(z + res[:, :tk] + carry)
        if mask is not None:
            a = jnp.where(mask, a, 0.0)
        acc = acc + _dot(a.astype(bf16), v_ref[0, 0, pl.ds(off, tk), :])
        return carry + res[:, tk:], acc

    carry, acc = tile(qi, jnp.zeros((tq, tk), f32), jnp.zeros((tq, HEAD_DIM), f32), strict)

    def body(i, c):
        return tile(qi - 1 - i, c[0], c[1], None)

    carry, acc = lax.fori_loop(0, qi, body, (carry, acc))
    o_ref[0, 0] = acc


def _sb_call(q_hm, kv_hm, tri):
    b, h, t, _ = q_hm.shape
    tq = Q_TILE
    return pl.pallas_call(
        _sb_kernel,
        grid=(b, h, t // tq),
        in_specs=[pl.BlockSpec((1, 1, tq, HEAD_DIM), lambda i, hh, j: (i, hh, j, 0)),
                  pl.BlockSpec((1, 1, t, HEAD_DIM), lambda i, hh, j: (i, hh, 0, 0)),
                  pl.BlockSpec((1, 1, t, HEAD_DIM), lambda i, hh, j: (i, h + hh, 0, 0)),
                  pl.BlockSpec(tri.shape, lambda i, hh, j: (0, 0))],
        out_specs=pl.BlockSpec((1, 1, tq, HEAD_DIM), lambda i, hh, j: (i, hh, j, 0)),
        out_shape=jax.ShapeDtypeStruct((b, h, t, HEAD_DIM), f32),
        compiler_params=_params(3),
        name="stick_breaking",
    )(q_hm, kv_hm, kv_hm, tri)


def _tri_weights(n):
    j = jnp.arange(n)[:, None]
    s = jnp.arange(n)[None, :]
    return jnp.concatenate([(j >= s).astype(bf16), jnp.ones((n, n), bf16)], axis=1)


def _out_ffn_kernel(x_ref, ocmp_ref, osel_ref, owin_ref, gate_ref, osb_ref,
                    gmsa_ref, shmlp_ref, scmlp_ref, gmlp_ref,
                    gexp_ref, nnsa_ref, nsb_ref, wout_ref, nffn_ref, wg_ref, wu_ref, wd_ref, nfin_ref,
                    y_ref, x1_ref, h2_ref, acc_ref):
    f = pl.program_id(2)

    @pl.when(f == 0)
    def _():
        g_hi, g_lo = _split(gate_ref[0])
        gx = _dot(g_hi, gexp_ref[...]) + _dot(g_lo, gexp_ref[...])
        w = NSA_WIDTH
        o_nsa = gx[:, 0:w] * ocmp_ref[0] + gx[:, w:2 * w] * osel_ref[0] + gx[:, 2 * w:3 * w] * owin_ref[0]
        o_sb = jnp.concatenate([osb_ref[0, h_i] for h_i in range(SB_HEADS)], axis=1)
        cat = jnp.concatenate([_rms(o_nsa, nnsa_ref[...]), _rms(o_sb, nsb_ref[...])], axis=1)
        x1 = x_ref[0] + gmsa_ref[0] * _dot(cat.astype(bf16), wout_ref[...])
        x1_ref[...] = x1
        h2_ref[...] = (_rms(x1, nffn_ref[...]) * (1.0 + scmlp_ref[0]) + shmlp_ref[0]).astype(bf16)
        acc_ref[...] = jnp.zeros_like(acc_ref)

    h2 = h2_ref[...]
    gt = _dot(h2, wg_ref[...])
    up = _dot(h2, wu_ref[...])
    act = gt * (1.0 / (1.0 + jnp.exp(-gt))) * up
    acc_ref[...] += _dot(act.astype(bf16), wd_ref[...])

    @pl.when(f == pl.num_programs(2) - 1)
    def _():
        y = x1_ref[...] + gmlp_ref[0] * acc_ref[...]
        y_ref[0] = _rms(y, nfin_ref[...])


def _out_ffn_call(x, ocmp, osel, owin, gates, osb_hm, mods4, gexp, nnsa, nsb, wout, nffn, wg, wu, wd, nfin, tm):
    b, t, d = x.shape
    tm = min(tm, t)
    dff = wg.shape[1]
    tf = dff // 2 if (dff // 2) % LANES == 0 else dff
    rows = mods4[0].shape[1]
    mod_spec = (pl.BlockSpec((1, 1, d), lambda i, j, f: (i, 0, 0)) if rows == 1
                else pl.BlockSpec((1, tm, d), lambda i, j, f: (i, j, 0)))

    def flat(c):
        return pl.BlockSpec((1, tm, c), lambda i, j, f: (i, j, 0))

    def const(shape):
        return pl.BlockSpec(shape, lambda i, j, f: (0,) * len(shape))

    return pl.pallas_call(
        _out_ffn_kernel,
        grid=(b, t // tm, dff // tf),
        in_specs=[flat(d), flat(NSA_WIDTH), flat(NSA_WIDTH), flat(NSA_WIDTH), flat(LANES),
                  pl.BlockSpec((1, SB_HEADS, tm, HEAD_DIM), lambda i, j, f: (i, 0, j, 0)),
                  mod_spec, mod_spec, mod_spec, mod_spec,
                  const(gexp.shape), const((1, NSA_WIDTH)), const((1, SB_WIDTH)), const(wout.shape),
                  const((1, d)),
                  pl.BlockSpec((d, tf), lambda i, j, f: (0, f)),
                  pl.BlockSpec((d, tf), lambda i, j, f: (0, f)),
                  pl.BlockSpec((tf, d), lambda i, j, f: (f, 0)),
                  const((1, d))],
        out_specs=flat(d),
        out_shape=jax.ShapeDtypeStruct((b, t, d), f32),
        scratch_shapes=[pltpu.VMEM((tm, d), f32), pltpu.VMEM((tm, d), bf16), pltpu.VMEM((tm, d), f32)],
        compiler_params=_params(3),
        name="out_ffn",
    )(x, ocmp, osel, owin, gates, osb_hm, *mods4, gexp, nnsa.reshape(1, -1), nsb.reshape(1, -1), wout,
      nffn.reshape(1, -1), wg, wu, wd, nfin.reshape(1, -1))


def _gate_expand():
    rows = jnp.arange(LANES)[:, None]
    cols = jnp.arange(3 * NSA_WIDTH)[None, :]
    return (rows == (cols // NSA_WIDTH) * NSA_HEADS + (cols % NSA_WIDTH) // HEAD_DIM).astype(bf16)


def _s_cmp_kernel(kv_ref, qbd_ref, qal_ref, pq_ref, ocmp_ref, sel_ref, *, tn):
    nbp = kv_ref.shape[1]
    kvb = kv_ref[0]
    kc = kvb[:, 0:LANES].astype(bf16)
    vc = kvb[:, LANES:2 * LANES].astype(bf16)
    n_idx = lax.broadcasted_iota(jnp.int32, (nbp, LANES), 0)
    pos = jnp.broadcast_to(pq_ref[...], (nbp, LANES))
    dist_i = pos - (n_idx * CMP_BLOCK + (CMP_BLOCK - 1))
    valid = dist_i >= 0
    slope = qal_ref[...]
    s = _dot(kc, qbd_ref[0]) - slope * dist_i.astype(f32)
    s = jnp.where(valid, s, NEG)
    e = jnp.exp(s - jnp.max(s, axis=0, keepdims=True))
    p = jnp.where(valid, e / jnp.sum(e, axis=0, keepdims=True), 0.0)
    ocmp_ref[0] = _dot_tn(p.astype(bf16), vc)
    imp = p
    for r in range(1, NSA_GROUP):
        imp = imp + pltpu.roll(p, LANES - r * tn, axis=1)
    cur = pos // CMP_BLOCK
    sel_ref[0] = _select_blocks(imp, n_idx, cur, min(N_SELECT, nbp))


def _col_softmax_pv(z_list, v_list):
    m = z_list[0].max(axis=0, keepdims=True)
    for z in z_list[1:]:
        m = jnp.maximum(m, z.max(axis=0, keepdims=True))
    e_list = [jnp.exp(z - m) for z in z_list]
    l = e_list[0].sum(axis=0, keepdims=True)
    for e in e_list[1:]:
        l = l + e.sum(axis=0, keepdims=True)
    inv = 1.0 / l
    o = None
    for e, v in zip(e_list, v_list):
        pv = _dot_tn((e * inv).astype(bf16), v)
        o = pv if o is None else o + pv
    return o


def _s_selwin_kernel(qbd_ref, saug_ref, eaug_ref, kvs_ref, news_ref, nbias_ref,
                     waug_ref, ewin_ref, wmask_ref, kvw_ref, neww_ref,
                     osel_ref, owin_ref):
    qbd = qbd_ref[0]
    kvs = kvs_ref[0]
    z = _dot(kvs[:, 0:LANES].astype(bf16), qbd) + _dot(eaug_ref[...], saug_ref[0])
    new = news_ref[0]
    z_new = _dot(new[:, 0:LANES].astype(bf16), qbd) + nbias_ref[...]
    osel_ref[0] = _col_softmax_pv([z, z_new], [kvs[:, LANES:].astype(bf16), new[:, LANES:].astype(bf16)])
    kvw = kvw_ref[0]
    z = _dot(kvw[:, 0:LANES].astype(bf16), qbd) + _dot(ewin_ref[...], waug_ref[...]) + wmask_ref[...]
    new = neww_ref[0]
    z_new = _dot(new[:, 0:LANES].astype(bf16), qbd) + nbias_ref[...]
    owin_ref[0] = _col_softmax_pv([z, z_new], [kvw[:, LANES:].astype(bf16), new[:, LANES:].astype(bf16)])


def _s_sb_kernel(qbd_ref, kv_ref, new_ref, nmask_ref, tri_ref, o_ref, carry_ref, acc_ref):
    i = pl.program_id(1)
    qbd = qbd_ref[0]
    tri = tri_ref[...]
    sub = tri.shape[0]
    w = SB_WIDTH

    def piece(rows, mask):
        z = _dot(rows[:, 0:w].astype(bf16), qbd)
        c = _neg_softplus(z)
        if mask is not None:
            c = c * mask
        c_hi, c_lo = _split(c)
        suf = _dot(tri, c_hi) + _dot(tri, c_lo)
        a = jnp.exp(z + suf + carry_ref[...])
        if mask is not None:
            a = a * mask
        acc_ref[...] += _dot_tn(a.astype(bf16), rows[:, w:].astype(bf16))
        carry_ref[...] += suf[0:1, :]

    @pl.when(i == 0)
    def _():
        carry_ref[...] = jnp.zeros_like(carry_ref)
        acc_ref[...] = jnp.zeros_like(acc_ref)
        piece(new_ref[0], nmask_ref[...])

    n_sub = kv_ref.shape[1] // sub
    for s_i in reversed(range(n_sub)):
        piece(kv_ref[0, s_i * sub:(s_i + 1) * sub, :], None)

    @pl.when(i == pl.num_programs(1) - 1)
    def _():
        o_ref[0] = acc_ref[...]


def _block_diag_q(q, n_groups):
    bs, h, tn, hd = q.shape
    per = h // n_groups
    eye = jnp.eye(n_groups, dtype=q.dtype)
    qg = q.reshape(bs, n_groups, per, tn, hd)
    out = jnp.einsum('bgrtd,gk->bgdkrt', qg, eye).reshape(bs, n_groups * hd, h * tn)
    return jnp.pad(out, ((0, 0), (0, 0), (0, LANES - h * tn)))


def kernel(x_prompt, x_sample, cache_nsa_cmp, cache_nsa_sel, cache_sb, state_nsa_win, page_table,
           c_prompt, c_sample, w_ada, b_ada, norm_attn, norm_ffn, w_in, cmp_pe_k, cmp_k_w1, cmp_k_w2,
           cmp_pe_v, cmp_v_w1, cmp_v_w2, norm_out_nsa, norm_out_sb, w_out, w_gate, w_up, w_down, norm_final):
    assert w_ada.shape[0] == 1, "single-layer trunk"
    bp, t, d = x_prompt.shape
    bs, tn, _ = x_sample.shape
    n_pages = page_table.shape[1]
    page = cache_nsa_cmp.shape[2]
    past = n_pages * page
    nb = t // CMP_BLOCK
    assert nb == LANES and past % CMP_BLOCK == 0 and NSA_HEADS * tn <= LANES
    slopes = _slopes()

    wi = w_in[0]
    c0 = NSA_WIDTH + 3 * KV_COLS
    w_cat = jnp.concatenate([wi[:, :c0], wi[:, c0 + 3 * NSA_HEADS:], wi[:, c0:c0 + 3 * NSA_HEADS],
                             jnp.zeros((d, _C_END - _C_G - 3 * NSA_HEADS), f32)], axis=1).astype(bf16)
    pe_row, w1_big, w2_big = _compress_weights(cmp_pe_k[0], cmp_k_w1[0], cmp_k_w2[0],
                                               cmp_pe_v[0], cmp_v_w1[0], cmp_v_w2[0])
    gexp = _gate_expand()
    wout_b, wg_b, wu_b, wd_b = (w_out[0].astype(bf16), w_gate[0].astype(bf16), w_up[0].astype(bf16),
                                w_down[0].astype(bf16))

    mods = _mods_call(jnp.concatenate([c_prompt, c_sample], axis=0), w_ada[0], b_ada[0])
    mods_p = mods[:bp].reshape(bp, 6, 1, d)
    mods_s = jnp.repeat(mods[bp:].reshape(bs, 6, d), tn, axis=0).reshape(bs * tn, 6, d)
    mp = [mods_p[:, i] for i in range(6)]
    ms = [mods_s[:, i].reshape(1, bs * tn, d) for i in range(6)]

    (kv_cmp, kv_sel, kv_win, kv_sb, gates, q_hm, qsb_hm, ksel_hm, kwin_hm, ksb_hm) = _proj_call(
        x_prompt, mp[0], mp[1], norm_attn[0], w_cat, 512)
    kcvc = _compress_call(kv_cmp.reshape(bp * nb, CMP_BLOCK * KV_COLS), pe_row, w1_big, w2_big)
    kcvc = kcvc.reshape(bp, nb, KV_COLS)
    o_cmp, selb, any_sel = _cmp_call(q_hm, kcvc)
    nq = t // Q_TILE
    any_tile = any_sel.reshape(bp, NSA_KV_HEADS, nq, nb * CMP_BLOCK // K_TILE, K_TILE // CMP_BLOCK).max(axis=-1)
    n_words = max(1, any_tile.shape[-1] // 32)
    bits = any_tile.reshape(bp, NSA_KV_HEADS, nq, n_words, -1).astype(jnp.uint32)
    flags = (bits << jnp.arange(bits.shape[-1], dtype=jnp.uint32)).sum(axis=-1, dtype=jnp.uint32)
    flags = lax.bitcast_convert_type(flags, jnp.int32).reshape(-1)
    pos_t = jnp.arange(t, dtype=jnp.int32)
    kconst = jnp.concatenate([jnp.zeros((t, HEAD_DIM), bf16), _alibi_key_cols(pos_t, HEAD_DIM),
                              (pos_t[:, None] // CMP_BLOCK == jnp.arange(nb)[None, :]).astype(bf16)], axis=1)
    qal = _alibi_query_cols(slopes, pos_t, HEAD_DIM)
    o_sel, o_win = _selwin_call(flags, q_hm, qal, selb, ksel_hm, kwin_hm, kconst)
    o_sb = _sb_call(qsb_hm, ksb_hm, _tri_weights(K_TILE))
    y_prompt = _out_ffn_call(x_prompt, o_cmp, o_sel, o_win, gates, o_sb, mp[2:], gexp, norm_out_nsa[0],
                             norm_out_sb[0], wout_b, norm_ffn[0], wg_b, wu_b, wd_b, norm_final, 256)
    win_len = min(WINDOW, t)
    new_win_prompt = kv_win[:, t - win_len:]

    n_tok = bs * tn
    (s_kv_cmp, s_kv_sel, s_kv_win, s_kv_sb, s_gates, s_q_hm, s_qsb_hm, _, _, _) = _proj_call(
        x_sample.reshape(1, n_tok, d), ms[0], ms[1], norm_attn[0], w_cat, n_tok)
    past_cmp, past_sel, past_sb = _gather_call(
        page_table, cache_nsa_cmp[0].reshape(-1, page, KV_COLS), cache_nsa_sel[0].reshape(-1, page, KV_COLS),
        cache_sb[0].reshape(-1, page, 2 * SB_WIDTH))
    nb_past = past // CMP_BLOCK
    nbs = nb_past + 1
    nbp = 2 * LANES
    assert nbs <= nbp - 8
    kc_past = _compress_call(past_cmp.reshape(bs * nb_past, CMP_BLOCK * KV_COLS), pe_row, w1_big, w2_big)
    tail = jnp.pad(s_kv_cmp.reshape(bs, tn, KV_COLS), ((0, 0), (0, CMP_BLOCK - tn), (0, 0)))
    kc_tail = _compress_call(tail.reshape(bs, CMP_BLOCK * KV_COLS), pe_row, w1_big, w2_big)
    s_kcvc = jnp.concatenate([kc_past.reshape(bs, nb_past, KV_COLS), kc_tail[:, None, :],
                              jnp.zeros((bs, nbp - nbs, KV_COLS), f32)], axis=1)

    pos_q = past + jnp.arange(tn, dtype=jnp.int32)
    n_cols = NSA_HEADS * tn
    col_t = jnp.arange(LANES) % tn
    col_h = jnp.minimum(jnp.arange(LANES) // tn, NSA_HEADS - 1)
    col_slope = slopes.reshape(-1)[col_h]
    col_pos = past + col_t
    s_q = s_q_hm.reshape(NSA_HEADS, bs, tn, HEAD_DIM).transpose(1, 0, 2, 3)
    qbd = _block_diag_q(s_q, NSA_KV_HEADS)

    def per_b(shape):
        return pl.BlockSpec((1,) + shape, lambda i: (i,) + (0,) * len(shape))

    def const1(shape):
        return pl.BlockSpec(shape, lambda i: (0,) * len(shape))

    s_ocmp, s_sel = pl.pallas_call(
        functools.partial(_s_cmp_kernel, tn=tn),
        grid=(bs,),
        in_specs=[per_b((nbp, KV_COLS)), per_b((LANES, LANES)), const1((1, LANES)), const1((1, LANES))],
        out_specs=(per_b((LANES, LANES)), per_b((nbp, LANES))),
        out_shape=(jax.ShapeDtypeStruct((bs, LANES, LANES), f32), jax.ShapeDtypeStruct((bs, nbp, LANES), f32)),
        compiler_params=_params(1),
        name="s_cmp_select",
    )(s_kcvc, qbd, col_slope.reshape(1, LANES).astype(f32), col_pos.reshape(1, LANES).astype(jnp.int32))

    col_g0 = (jnp.arange(LANES) // (NSA_GROUP * tn)) * (NSA_GROUP * tn) + col_t
    sel_cols = s_sel[:, :, col_g0]
    n_hot = nbp // 2 + 8
    k_aug = 2 * LANES
    selb_rows = jnp.where(sel_cols[:, :n_hot] > 0.5, 0.0, MASK_BIAS)
    al_rows = jnp.stack([col_slope * CMP_BLOCK, col_slope,
                         -col_slope * CMP_BLOCK * (col_pos // CMP_BLOCK).astype(f32),
                         -col_slope * (col_pos % CMP_BLOCK).astype(f32)], axis=0)
    saug = jnp.concatenate([selb_rows, jnp.broadcast_to(al_rows, (bs, 4, LANES)),
                            jnp.zeros((bs, k_aug - n_hot - 4, LANES), f32)], axis=1).astype(bf16)
    pos_p = jnp.arange(past, dtype=jnp.int32)
    eaug = jnp.concatenate([(pos_p[:, None] // CMP_BLOCK == jnp.arange(n_hot)[None, :]).astype(bf16),
                            _alibi_key_cols(pos_p, k_aug - n_hot)], axis=1)
    win_rows = state_nsa_win.shape[2]
    pos_w = past - win_rows + jnp.arange(win_rows, dtype=jnp.int32)
    ewin = _alibi_key_cols(pos_w, LANES)
    waug = jnp.concatenate([al_rows, jnp.zeros((LANES - 4, LANES), f32)], axis=0).astype(bf16)
    dist_w = col_pos[None, :] - pos_w[:, None]
    wmask = jnp.where((dist_w >= 0) & (dist_w <= WINDOW) & (pos_w[:, None] >= 0), 0.0, MASK_BIAS).astype(f32)
    j_new = jnp.arange(K_TILE)[:, None]
    ok_new = (j_new <= col_t[None, :]) & (j_new < tn)
    nbias = jnp.where(ok_new, -col_slope[None, :] * (col_t[None, :] - j_new).astype(f32), MASK_BIAS).astype(f32)
    pad_new = ((0, 0), (0, K_TILE - tn), (0, 0))
    new_sel = jnp.pad(s_kv_sel.reshape(bs, tn, KV_COLS), pad_new)
    new_win = jnp.pad(s_kv_win.reshape(bs, tn, KV_COLS), pad_new)
    win_buf = state_nsa_win[0].reshape(bs, win_rows, KV_COLS)

    out_o = jax.ShapeDtypeStruct((bs, LANES, LANES), f32)
    s_osel, s_owin = pl.pallas_call(
        _s_selwin_kernel,
        grid=(bs,),
        in_specs=[per_b((LANES, LANES)), per_b((k_aug, LANES)), const1((past, k_aug)),
                  per_b((past, KV_COLS)), per_b((K_TILE, KV_COLS)), const1((K_TILE, LANES)),
                  const1((LANES, LANES)), const1((win_rows, LANES)), const1((win_rows, LANES)),
                  per_b((win_rows, KV_COLS)), per_b((K_TILE, KV_COLS))],
        out_specs=(per_b((LANES, LANES)), per_b((LANES, LANES))),
        out_shape=(out_o, out_o),
        compiler_params=_params(1),
        name="s_sel_win",
    )(qbd, saug, eaug, past_sel, new_sel, nbias, waug, ewin, wmask, win_buf, new_win)

    def nsa_heads(o_full):
        o = o_full[:, :n_cols, :].reshape(bs, NSA_KV_HEADS, NSA_GROUP, tn, NSA_KV_HEADS, HEAD_DIM)
        o = jnp.stack([o[:, gi, :, :, gi, :] for gi in range(NSA_KV_HEADS)], axis=1)
        return o.transpose(0, 3, 1, 2, 4).reshape(1, n_tok, NSA_WIDTH)

    so_cmp = nsa_heads(s_ocmp)
    so_sel = nsa_heads(s_osel)
    so_win = nsa_heads(s_owin)

    s_qsb = s_qsb_hm.reshape(SB_HEADS, bs, tn, HEAD_DIM).transpose(1, 0, 2, 3)
    qbd_sb = _block_diag_q(s_qsb, SB_HEADS)
    new_sb = jnp.pad(s_kv_sb.reshape(bs, tn, 2 * SB_WIDTH), pad_new)
    nmask = ((j_new < col_t[None, :]) & (j_new < tn)).astype(f32)
    chunk = 2048 if past % 2048 == 0 else past
    n_chunks = past // chunk
    tri_rows = (jnp.arange(K_TILE)[None, :] >= jnp.arange(K_TILE)[:, None]).astype(bf16)
    s_osb = pl.pallas_call(
        _s_sb_kernel,
        grid=(bs, n_chunks),
        in_specs=[pl.BlockSpec((1, SB_WIDTH, LANES), lambda i, j: (i, 0, 0)),
                  pl.BlockSpec((1, chunk, 2 * SB_WIDTH), lambda i, j: (i, n_chunks - 1 - j, 0)),
                  pl.BlockSpec((1, K_TILE, 2 * SB_WIDTH), lambda i, j: (i, 0, 0)),
                  pl.BlockSpec((K_TILE, LANES), lambda i, j: (0, 0)),
                  pl.BlockSpec((K_TILE, K_TILE), lambda i, j: (0, 0))],
        out_specs=pl.BlockSpec((1, LANES, SB_WIDTH), lambda i, j: (i, 0, 0)),
        out_shape=jax.ShapeDtypeStruct((bs, LANES, SB_WIDTH), f32),
        scratch_shapes=[pltpu.VMEM((1, LANES), f32), pltpu.VMEM((LANES, SB_WIDTH), f32)],
        compiler_params=_params(2),
        name="s_stick_breaking",
    )(qbd_sb, past_sb, new_sb, nmask, tri_rows)
    o = s_osb[:, :SB_HEADS * tn, :].reshape(bs, SB_HEADS, tn, SB_HEADS, HEAD_DIM)
    so_sb = jnp.stack([o[:, hi, :, hi, :] for hi in range(SB_HEADS)], axis=0)
    so_sb = so_sb.reshape(1, SB_HEADS, n_tok, HEAD_DIM)

    y_sample = _out_ffn_call(x_sample.reshape(1, n_tok, d), so_cmp, so_sel, so_win, s_gates, so_sb, ms[2:], gexp,
                             norm_out_nsa[0], norm_out_sb[0], wout_b, norm_ffn[0], wg_b, wu_b, wd_b, norm_final,
                             256).reshape(bs, tn, d)
    all_win = jnp.concatenate([win_buf, s_kv_win.reshape(bs, tn, KV_COLS)], axis=1)
    keep = min(WINDOW, win_rows + tn)
    new_win_sample = all_win[:, win_rows + tn - keep:]

    g_n = NSA_KV_HEADS
    return (y_prompt, y_sample,
            kv_cmp.reshape(1, bp, t, 2, g_n, HEAD_DIM), kv_sel.reshape(1, bp, t, 2, g_n, HEAD_DIM),
            kv_sb.reshape(1, bp, t, 2, SB_HEADS, HEAD_DIM), new_win_prompt.reshape(1, bp, win_len, 2, g_n, HEAD_DIM),
            s_kv_cmp.reshape(1, bs, tn, 2, g_n, HEAD_DIM), s_kv_sel.reshape(1, bs, tn, 2, g_n, HEAD_DIM),
            s_kv_sb.reshape(1, bs, tn, 2, SB_HEADS, HEAD_DIM), new_win_sample.reshape(1, bs, keep, 2, g_n, HEAD_DIM))
```

```python
import functools

import jax
import jax.numpy as jnp
from jax import lax
from jax.experimental import pallas as pl
from jax.experimental.pallas import tpu as pltpu

HEAD_DIM = 64
NSA_HEADS = 8
NSA_KV_HEADS = 2
NSA_GROUP = NSA_HEADS // NSA_KV_HEADS
SB_HEADS = 8
NSA_WIDTH = NSA_HEADS * HEAD_DIM
SB_WIDTH = SB_HEADS * HEAD_DIM
KV_COLS = 2 * NSA_KV_HEADS * HEAD_DIM
CMP_BLOCK = 64
CMP_HIDDEN = 128
N_SELECT = 16
WINDOW = 512
EPS = 1e-6
NEG = -1e30
MASK_BIAS = -1e9
LANES = 128
Q_TILE = 128
K_TILE = 128
QK_SCALE = HEAD_DIM ** -0.5
VMEM_LIMIT = 56 * 1024 * 1024

f32 = jnp.float32
bf16 = jnp.bfloat16


def _dot(a, b):
    return jnp.dot(a, b, preferred_element_type=f32)


def _dot_nt(a, b):
    return lax.dot_general(a, b, (((1,), (1,)), ((), ())), preferred_element_type=f32)


def _dot_tn(a, b):
    return lax.dot_general(a, b, (((0,), (0,)), ((), ())), preferred_element_type=f32)


def _split(x):
    hi = x.astype(bf16)
    lo = (x - hi.astype(f32)).astype(bf16)
    return hi, lo


def _rms(x, g):
    return x * lax.rsqrt(jnp.mean(x * x, axis=-1, keepdims=True) + EPS) * g


def _params(n_axes, vmem=VMEM_LIMIT):
    return pltpu.CompilerParams(dimension_semantics=("arbitrary",) * n_axes, vmem_limit_bytes=vmem)


def _mods_kernel(c_ref, w_ref, b_ref, o_ref):
    c = c_ref[...]
    a = c * (1.0 / (1.0 + jnp.exp(-c)))
    a_hi, a_lo = _split(a)
    w_hi, w_lo = _split(w_ref[...])
    o_ref[...] = _dot(a_hi, w_hi) + _dot(a_lo, w_hi) + _dot(a_hi, w_lo) + b_ref[...]


def _mods_call(c_all, w_ada, b_ada):
    m, d = c_all.shape
    n = w_ada.shape[1]
    tn = 1536 if n % 1536 == 0 else n
    return pl.pallas_call(
        _mods_kernel,
        grid=(n // tn,),
        in_specs=[pl.BlockSpec((m, d), lambda j: (0, 0)),
                  pl.BlockSpec((d, tn), lambda j: (0, j)),
                  pl.BlockSpec((1, tn), lambda j: (0, j))],
        out_specs=pl.BlockSpec((m, tn), lambda j: (0, j)),
        out_shape=jax.ShapeDtypeStruct((m, n), f32),
        compiler_params=_params(1),
        name="mods",
    )(c_all, w_ada, b_ada.reshape(1, n))


_C_Q, _C_KC, _C_KS, _C_KW, _C_QSB, _C_KVSB, _C_G, _C_END = 0, 512, 768, 1024, 1280, 1792, 2816, 2944


def _proj_kernel(x_ref, sh_ref, sc_ref, g_ref, w_ref,
                 kvc_ref, kvs_ref, kvw_ref, kvsb_ref, gate_ref,
                 qn_ref, qsb_ref, ksel_ref, kwin_ref, ksb_ref):
    x = x_ref[0]
    h = _rms(x, g_ref[...]) * (1.0 + sc_ref[0]) + sh_ref[0]
    r = _dot(h.astype(bf16), w_ref[...])
    kvc_ref[0] = r[:, _C_KC:_C_KS]
    kvs_ref[0] = r[:, _C_KS:_C_KW]
    kvw_ref[0] = r[:, _C_KW:_C_QSB]
    kvsb_ref[0] = r[:, _C_KVSB:_C_G]
    gate_ref[0] = 1.0 / (1.0 + jnp.exp(-r[:, _C_G:_C_END]))
    for h_i in range(NSA_HEADS):
        lo = _C_Q + h_i * HEAD_DIM
        qn_ref[0, h_i] = (r[:, lo:lo + HEAD_DIM] * QK_SCALE).astype(bf16)
    for h_i in range(SB_HEADS):
        lo = _C_QSB + h_i * HEAD_DIM
        qsb_ref[0, h_i] = (r[:, lo:lo + HEAD_DIM] * QK_SCALE).astype(bf16)
    for j in range(2 * NSA_KV_HEADS):
        lo = _C_KS + j * HEAD_DIM
        ksel_ref[0, j] = r[:, lo:lo + HEAD_DIM].astype(bf16)
        lo = _C_KW + j * HEAD_DIM
        kwin_ref[0, j] = r[:, lo:lo + HEAD_DIM].astype(bf16)
    for j in range(2 * SB_HEADS):
        lo = _C_KVSB + j * HEAD_DIM
        ksb_ref[0, j] = r[:, lo:lo + HEAD_DIM].astype(bf16)


def _proj_call(x, shift, scale, g, w_cat, tm):
    b, t, d = x.shape
    rows = shift.shape[1]
    mod_spec = (pl.BlockSpec((1, 1, d), lambda i, j: (i, 0, 0)) if rows == 1
                else pl.BlockSpec((1, tm, d), lambda i, j: (i, j, 0)))

    def flat(c):
        return pl.BlockSpec((1, tm, c), lambda i, j: (i, j, 0))

    def heads(n):
        return pl.BlockSpec((1, n, tm, HEAD_DIM), lambda i, j: (i, 0, j, 0))

    out_shape = (
        jax.ShapeDtypeStruct((b, t, KV_COLS), f32),
        jax.ShapeDtypeStruct((b, t, KV_COLS), f32),
        jax.ShapeDtypeStruct((b, t, KV_COLS), f32),
        jax.ShapeDtypeStruct((b, t, 2 * SB_WIDTH), f32),
        jax.ShapeDtypeStruct((b, t, LANES), f32),
        jax.ShapeDtypeStruct((b, NSA_HEADS, t, HEAD_DIM), bf16),
        jax.ShapeDtypeStruct((b, SB_HEADS, t, HEAD_DIM), bf16),
        jax.ShapeDtypeStruct((b, 2 * NSA_KV_HEADS, t, HEAD_DIM), bf16),
        jax.ShapeDtypeStruct((b, 2 * NSA_KV_HEADS, t, HEAD_DIM), bf16),
        jax.ShapeDtypeStruct((b, 2 * SB_HEADS, t, HEAD_DIM), bf16),
    )
    out_specs = (flat(KV_COLS), flat(KV_COLS), flat(KV_COLS), flat(2 * SB_WIDTH), flat(LANES),
                 heads(NSA_HEADS), heads(SB_HEADS), heads(2 * NSA_KV_HEADS), heads(2 * NSA_KV_HEADS),
                 heads(2 * SB_HEADS))
    return pl.pallas_call(
        _proj_kernel,
        grid=(b, t // tm),
        in_specs=[flat(d), mod_spec, mod_spec,
                  pl.BlockSpec((1, d), lambda i, j: (0, 0)),
                  pl.BlockSpec(w_cat.shape, lambda i, j: (0, 0))],
        out_specs=out_specs,
        out_shape=out_shape,
        compiler_params=_params(2),
        name="proj",
    )(x, shift, scale, g.reshape(1, d), w_cat)


def _gelu_tanh(x):
    return 0.5 * x * (1.0 + jnp.tanh(0.7978845608028654 * (x + 0.044715 * (x * x * x))))


def _compress_kernel(x_ref, pe_ref, w1_ref, w2_ref, o_ref, acc_ref):
    k = pl.program_id(1)

    @pl.when(k == 0)
    def _():
        acc_ref[...] = jnp.zeros_like(acc_ref)

    acc_ref[...] += _dot((x_ref[...] + pe_ref[...]).astype(bf16), w1_ref[...])

    @pl.when(k == pl.num_programs(1) - 1)
    def _():
        o_ref[...] = _dot(_gelu_tanh(acc_ref[...]).astype(bf16), w2_ref[...])


def _compress_call(x_rows, pe_row, w1_big, w2_big):
    m, kdim = x_rows.shape
    tm = 256 if m % 256 == 0 else m
    tk = 4096
    n1 = w1_big.shape[1]
    n2 = w2_big.shape[1]
    return pl.pallas_call(
        _compress_kernel,
        grid=(m // tm, kdim // tk),
        in_specs=[pl.BlockSpec((tm, tk), lambda i, k: (i, k)),
                  pl.BlockSpec((1, tk), lambda i, k: (0, k)),
                  pl.BlockSpec((tk, n1), lambda i, k: (k, 0)),
                  pl.BlockSpec((n1, n2), lambda i, k: (0, 0))],
        out_specs=pl.BlockSpec((tm, n2), lambda i, k: (i, 0)),
        out_shape=jax.ShapeDtypeStruct((m, n2), f32),
        scratch_shapes=[pltpu.VMEM((tm, n1), f32)],
        compiler_params=_params(2),
        name="compress",
    )(x_rows, pe_row, w1_big, w2_big)


def _compress_weights(pe_k, w1_k, w2_k, pe_v, w1_v, w2_v):
    g = NSA_KV_HEADS
    w1 = jnp.zeros((CMP_BLOCK, 2, g, HEAD_DIM, 2, g, CMP_HIDDEN), f32)
    w2 = jnp.zeros((2, g, CMP_HIDDEN, 2, g, HEAD_DIM), f32)
    for kv, (a1, a2) in enumerate(((w1_k, w2_k), (w1_v, w2_v))):
        a1 = a1.reshape(CMP_BLOCK, HEAD_DIM, CMP_HIDDEN)
        for gi in range(g):
            w1 = w1.at[:, kv, gi, :, kv, gi, :].set(a1)
            w2 = w2.at[kv, gi, :, kv, gi, :].set(a2)
    w1 = w1.reshape(CMP_BLOCK * 2 * g * HEAD_DIM, 2 * g * CMP_HIDDEN).astype(bf16)
    w2 = w2.reshape(2 * g * CMP_HIDDEN, 2 * g * HEAD_DIM).astype(bf16)
    pe = jnp.stack([jnp.broadcast_to(pe_k[:, None, :], (CMP_BLOCK, g, HEAD_DIM)),
                    jnp.broadcast_to(pe_v[:, None, :], (CMP_BLOCK, g, HEAD_DIM))], axis=1)
    return pe.reshape(1, -1), w1, w2


PAGES_PER_STEP = 8


def _gather_kernel(pt_ref, *refs, n_caches):
    per = (len(refs) - n_caches) // n_caches
    ins, outs = refs[:n_caches * per], refs[n_caches * per:]
    for c in range(n_caches):
        page = ins[c * per].shape[1]
        for k in range(per):
            outs[c][0, k * page:(k + 1) * page, :] = ins[c * per + k][0]


def _gather_call(page_table, caches):
    b, n_pages = page_table.shape
    per = PAGES_PER_STEP if n_pages % PAGES_PER_STEP == 0 else 1
    in_specs, args = [], []
    for c in caches:
        for k in range(per):
            in_specs.append(pl.BlockSpec((1,) + c.shape[1:], lambda i, j, pt, k=k: (pt[i, j * per + k], 0, 0)))
            args.append(c)
    out_specs = tuple(pl.BlockSpec((1, per * c.shape[1], c.shape[2]), lambda i, j, pt: (i, j, 0)) for c in caches)
    return pl.pallas_call(
        functools.partial(_gather_kernel, n_caches=len(caches)),
        grid_spec=pltpu.PrefetchScalarGridSpec(
            num_scalar_prefetch=1, grid=(b, n_pages // per), in_specs=in_specs, out_specs=out_specs),
        out_shape=tuple(jax.ShapeDtypeStruct((b, n_pages * c.shape[1], c.shape[2]), c.dtype) for c in caches),
        compiler_params=_params(2),
        name="paged_gather",
    )(page_table, *args)


def _select_blocks(imp, n_idx, cur, n_rounds):
    forced = (n_idx == 0) | (n_idx == cur) | (n_idx == cur - 1)
    in_range = n_idx <= cur
    v = jnp.where(forced, 8.0, imp)
    v = jnp.where(in_range, v, -1.0)
    sel = jnp.zeros(imp.shape, f32)
    big = jnp.int32(1 << 20)
    for _ in range(n_rounds):
        mx = jnp.max(v, axis=0, keepdims=True)
        idx = jnp.min(jnp.where(v == mx, n_idx, big), axis=0, keepdims=True)
        hit = n_idx == idx
        sel = jnp.where(hit, 1.0, sel)
        v = jnp.where(hit, -2.0, v)
    return jnp.where(in_range, sel, 0.0)


def _slope(g, r):
    return jnp.where(g == 0, 2.0 ** -(r + 1), 2.0 ** -(NSA_GROUP + r + 1)).astype(f32)


def _cmp_kernel(q_ref, kv_ref, ocmp_ref, selb_ref, any_ref):
    g = pl.program_id(1)
    qi = pl.program_id(2)
    tq = q_ref.shape[2]
    nb = kv_ref.shape[1]
    kvb = kv_ref[0]
    kc = jnp.where(g == 0, kvb[:, 0:HEAD_DIM], kvb[:, HEAD_DIM:2 * HEAD_DIM]).astype(bf16)
    vc = jnp.where(g == 0, kvb[:, 2 * HEAD_DIM:3 * HEAD_DIM], kvb[:, 3 * HEAD_DIM:4 * HEAD_DIM]).astype(bf16)
    n_idx = lax.broadcasted_iota(jnp.int32, (nb, tq), 0)
    pos = qi * tq + lax.broadcasted_iota(jnp.int32, (nb, tq), 1)
    dist_i = pos - (n_idx * CMP_BLOCK + (CMP_BLOCK - 1))
    valid = dist_i >= 0
    dist = dist_i.astype(f32)
    imp = jnp.zeros((nb, tq), f32)
    outs = []
    for r in range(NSA_GROUP):
        s = _dot_nt(kc, q_ref[0, r]) - _slope(g, r) * dist
        s = jnp.where(valid, s, NEG)
        e = jnp.exp(s - jnp.max(s, axis=0, keepdims=True))
        p = jnp.where(valid, e / jnp.sum(e, axis=0, keepdims=True), 0.0)
        imp = imp + p
        outs.append(_dot_tn(p.astype(bf16), vc))
    ocmp_ref[0] = jnp.concatenate(outs, axis=1)
    cur = pos // CMP_BLOCK
    sel = _select_blocks(imp, n_idx, cur, min(N_SELECT, nb))
    sel_t = sel.T
    selb_ref[0, 0] = jnp.where(sel_t > 0.5, 0.0, MASK_BIAS).astype(bf16)
    any_ref[0, 0, 0] = jnp.max(sel_t, axis=0, keepdims=True).astype(jnp.int32)


def _cmp_call(q_hm, kcvc):
    b, _, t, _ = q_hm.shape
    nb = kcvc.shape[1]
    tq = Q_TILE
    nq = t // tq
    return pl.pallas_call(
        _cmp_kernel,
        grid=(b, NSA_KV_HEADS, nq),
        in_specs=[pl.BlockSpec((1, NSA_GROUP, tq, HEAD_DIM), lambda i, g, j: (i, g, j, 0)),
                  pl.BlockSpec((1, nb, KV_COLS), lambda i, g, j: (i, 0, 0))],
        out_specs=(pl.BlockSpec((1, tq, NSA_GROUP * HEAD_DIM), lambda i, g, j: (i, j, g)),
                   pl.BlockSpec((1, 1, tq, nb), lambda i, g, j: (i, g, j, 0)),
                   pl.BlockSpec((1, 1, 1, 1, nb), lambda i, g, j: (i, g, j, 0, 0))),
        out_shape=(jax.ShapeDtypeStruct((b, t, NSA_WIDTH), f32),
                   jax.ShapeDtypeStruct((b, NSA_KV_HEADS, t, nb), bf16),
                   jax.ShapeDtypeStruct((b, NSA_KV_HEADS, nq, 1, nb), jnp.int32)),
        compiler_params=_params(3),
        name="cmp_select",
    )(q_hm, kcvc)


SEL_TK = 256


def _softmax_tile(k_tile, qa, v_tile, m_ref, l_ref, acc_ref, mask):
    s = _dot_nt(k_tile, qa)
    if mask is not None:
        s = jnp.where(mask, s, MASK_BIAS)
    m_old = m_ref[...]
    m_new = jnp.maximum(m_old, jnp.max(s, axis=0, keepdims=True))
    alpha = jnp.exp(m_old - m_new)
    p = jnp.exp(s - m_new)
    l_ref[...] = alpha * l_ref[...] + jnp.sum(p, axis=0, keepdims=True)
    acc_ref[...] = alpha * acc_ref[...] + _dot_tn(v_tile, p.astype(bf16))
    m_ref[...] = m_new


def _selwin_kernel(flag_ref, q_ref, qal_ref, selb_ref, ks_ref, vs_ref, kw_ref, vw_ref, kconst_ref,
                   osel_ref, owin_ref, kaug_s, kaug_w, qaug, m_ref, l_ref, acc_ref):
    b = pl.program_id(0)
    g = pl.program_id(1)
    qi = pl.program_id(2)
    tq = q_ref.shape[2]
    tk = SEL_TK
    rows = NSA_GROUP * tq
    t_all = ks_ref.shape[2]

    @pl.when(qi == 0)
    def _():
        kaug_s[...] = kconst_ref[...]
        kaug_s[:, 0:HEAD_DIM] = ks_ref[0, 0]
        kaug_w[...] = kconst_ref[:, 0:LANES]
        kaug_w[:, 0:HEAD_DIM] = kw_ref[0, 0]

    qaug[:, 0:HEAD_DIM] = q_ref[0].reshape(rows, HEAD_DIM)
    qaug[:, HEAD_DIM:LANES] = qal_ref[0].reshape(rows, HEAD_DIM)
    sb = selb_ref[0, 0]
    for r in range(NSA_GROUP):
        qaug[r * tq:(r + 1) * tq, LANES:] = sb
    qa = qaug[...]
    qa_w = qaug[:, 0:LANES]

    q0 = qi * tq
    jd = q0 // tk
    q_pos = q0 + (lax.broadcasted_iota(jnp.int32, (tk, rows), 1) & (tq - 1))
    k_rel = lax.broadcasted_iota(jnp.int32, (tk, rows), 0)

    def reset():
        m_ref[...] = jnp.full(m_ref.shape, NEG, f32)
        l_ref[...] = jnp.zeros(l_ref.shape, f32)
        acc_ref[...] = jnp.zeros(acc_ref.shape, f32)

    def finish(o_ref):
        o_t = acc_ref[...] / l_ref[...]
        o_ref[0] = jnp.concatenate([o_t[:, r * tq:(r + 1) * tq].T for r in range(NSA_GROUP)], axis=1)

    def tile(kaug, v_ref, qmat, j, mask):
        off = pl.multiple_of(j * tk, tk)
        _softmax_tile(kaug[pl.ds(off, tk), :], qmat, v_ref[0, 0, pl.ds(off, tk), :], m_ref, l_ref, acc_ref, mask)

    reset()
    n_words = max(1, (t_all // tk) // 32)
    base = ((b * NSA_KV_HEADS + g) * (t_all // tq) + qi) * n_words

    def sel_body(j, carry):
        word = flag_ref[base + (j >> 5)]

        @pl.when(((word >> (j & 31)) & 1) == 1)
        def _():
            tile(kaug_s, vs_ref, qa, j, None)

        return carry

    lax.fori_loop(0, jd, sel_body, 0)
    tile(kaug_s, vs_ref, qa, jd, jd * tk + k_rel <= q_pos)
    finish(osel_ref)

    reset()
    n_back = -(-WINDOW // tk)
    for i in range(n_back, 0, -1):
        jt = jd - i
        mask_fn = (lambda jt=jt: jt * tk + k_rel >= q_pos - WINDOW) if i == n_back else None

        @pl.when(jt >= 0)
        def _(jt=jt, mask_fn=mask_fn):
            tile(kaug_w, vw_ref, qa_w, jt, None if mask_fn is None else mask_fn())

    tile(kaug_w, vw_ref, qa_w, jd, jd * tk + k_rel <= q_pos)
    finish(owin_ref)


def _selwin_call(flags, q_hm, qal, selb, ksel_hm, kwin_hm, kconst):
    b, _, t, _ = q_hm.shape
    tq = Q_TILE
    nq = t // tq
    ka = kconst.shape[1]
    g_n = NSA_KV_HEADS

    def kv_spec(off):
        return pl.BlockSpec((1, 1, t, HEAD_DIM), lambda i, g, j, fl: (i, off + g, 0, 0))

    out_spec = pl.BlockSpec((1, tq, NSA_GROUP * HEAD_DIM), lambda i, g, j, fl: (i, j, g))
    return pl.pallas_call(
        _selwin_kernel,
        grid_spec=pltpu.PrefetchScalarGridSpec(
            num_scalar_prefetch=1, grid=(b, g_n, nq),
            in_specs=[pl.BlockSpec((1, NSA_GROUP, tq, HEAD_DIM), lambda i, g, j, fl: (i, g, j, 0)),
                      pl.BlockSpec((1, NSA_GROUP, tq, HEAD_DIM), lambda i, g, j, fl: (g, 0, j, 0)),
                      pl.BlockSpec((1, 1, tq, ka - LANES), lambda i, g, j, fl: (i, g, j, 0)),
                      kv_spec(0), kv_spec(g_n), kv_spec(0), kv_spec(g_n),
                      pl.BlockSpec((t, ka), lambda i, g, j, fl: (0, 0))],
            out_specs=(out_spec, out_spec),
            scratch_shapes=[pltpu.VMEM((t, ka), bf16), pltpu.VMEM((t, LANES), bf16),
                            pltpu.VMEM((NSA_GROUP * tq, ka), bf16),
                            pltpu.VMEM((1, NSA_GROUP * tq), f32), pltpu.VMEM((1, NSA_GROUP * tq), f32),
                            pltpu.VMEM((HEAD_DIM, NSA_GROUP * tq), f32)]),
        out_shape=(jax.ShapeDtypeStruct((b, t, NSA_WIDTH), f32), jax.ShapeDtypeStruct((b, t, NSA_WIDTH), f32)),
        compiler_params=_params(3),
        name="sel_win",
    )(flags, q_hm, qal, selb, ksel_hm, ksel_hm, kwin_hm, kwin_hm, kconst)


def _alibi_key_cols(pos, width):
    cols = jnp.stack([pos // CMP_BLOCK, pos % CMP_BLOCK, jnp.ones_like(pos), jnp.ones_like(pos)], axis=1)
    return jnp.pad(cols.astype(f32), ((0, 0), (0, width - 4))).astype(bf16)


def _alibi_query_cols(slopes, pos, width):
    s = slopes[..., None]
    a = (pos // CMP_BLOCK).astype(f32)
    r = (pos % CMP_BLOCK).astype(f32)
    cols = jnp.stack([jnp.broadcast_to(s * CMP_BLOCK, s.shape[:-1] + pos.shape),
                      jnp.broadcast_to(s, s.shape[:-1] + pos.shape),
                      -s * CMP_BLOCK * a, -s * r], axis=-1)
    pad = [(0, 0)] * (cols.ndim - 1) + [(0, width - 4)]
    return jnp.pad(cols, pad).astype(bf16)


def _slopes():
    h = jnp.arange(1, NSA_HEADS + 1, dtype=f32)
    return jnp.exp2(-8.0 * h / NSA_HEADS).reshape(NSA_KV_HEADS, NSA_GROUP)


def _neg_softplus(z):
    return -(jnp.maximum(z, 0.0) + jnp.log(1.0 + jnp.exp(-jnp.abs(z))))


SB_TILE = 512


def _sb_kernel(q_ref, k_ref, v_ref, tri_ref, o_ref, carry_ref, acc_ref):
    qi = pl.program_id(2)
    tq = q_ref.shape[2]
    tk = tq
    sub = tri_ref.shape[1] // 2
    q = q_ref[0, 0]
    tri = tri_ref[...]

    def tile(j, mask):
        off = pl.multiple_of(j * tk, tk)
        z = _dot_nt(q, k_ref[0, 0, pl.ds(off, tk), :])
        c = _neg_softplus(z)
        if mask is not None:
            c = jnp.where(mask, c, 0.0)
        c_hi, c_lo = _split(c)
        run = carry_ref[...]
        parts = [None] * (tk // sub)
        for i in reversed(range(tk // sub)):
            sl = slice(i * sub, (i + 1) * sub)
            res = _dot(jnp.concatenate([c_hi[:, sl], c_lo[:, sl]], axis=1), tri)
            parts[i] = z[:, sl] + res[:, :sub] + run
            run = run + res[:, sub:]
        a = jnp.exp(jnp.concatenate(parts, axis=1))
        if mask is not None:
            a = jnp.where(mask, a, 0.0)
        acc_ref[...] += _dot(a.astype(bf16), v_ref[0, 0, pl.ds(off, tk), :])
        carry_ref[...] = run

    carry_ref[...] = jnp.zeros(carry_ref.shape, f32)
    acc_ref[...] = jnp.zeros(acc_ref.shape, f32)
    t_idx = lax.broadcasted_iota(jnp.int32, (tq, tk), 0)
    k_idx = lax.broadcasted_iota(jnp.int32, (tq, tk), 1)
    tile(qi, k_idx < t_idx)

    def body(i, c):
        tile(qi - 1 - i, None)
        return c

    lax.fori_loop(0, qi, body, 0)
    o_ref[0, 0] = acc_ref[...]


def _sb_call(q_hm, kv_hm, tri):
    b, h, t, _ = q_hm.shape
    tq = SB_TILE
    return pl.pallas_call(
        _sb_kernel,
        grid=(b, h, t // tq),
        in_specs=[pl.BlockSpec((1, 1, tq, HEAD_DIM), lambda i, hh, j: (i, hh, j, 0)),
                  pl.BlockSpec((1, 1, t, HEAD_DIM), lambda i, hh, j: (i, hh, 0, 0)),
                  pl.BlockSpec((1, 1, t, HEAD_DIM), lambda i, hh, j: (i, h + hh, 0, 0)),
                  pl.BlockSpec(tri.shape, lambda i, hh, j: (0, 0))],
        out_specs=pl.BlockSpec((1, 1, tq, HEAD_DIM), lambda i, hh, j: (i, hh, j, 0)),
        out_shape=jax.ShapeDtypeStruct((b, h, t, HEAD_DIM), f32),
        scratch_shapes=[pltpu.VMEM((tq, tri.shape[1] // 2), f32), pltpu.VMEM((tq, HEAD_DIM), f32)],
        compiler_params=_params(3),
        name="stick_breaking",
    )(q_hm, kv_hm, kv_hm, tri)


def _tri_weights(n):
    j = jnp.arange(n)[:, None]
    s = jnp.arange(n)[None, :]
    w = jnp.concatenate([(j >= s).astype(bf16), jnp.ones((n, n), bf16)], axis=1)
    return jnp.concatenate([w, w], axis=0)


def _out_ffn_kernel(x_ref, ocmp_ref, osel_ref, owin_ref, gate_ref, osb_ref,
                    gmsa_ref, shmlp_ref, scmlp_ref, gmlp_ref,
                    gexp_ref, nnsa_ref, nsb_ref, wout_ref, nffn_ref, wg_ref, wu_ref, wd_ref, nfin_ref,
                    y_ref, x1_ref, h2_ref, acc_ref):
    f = pl.program_id(2)

    @pl.when(f == 0)
    def _():
        g_hi, g_lo = _split(gate_ref[0])
        gx = _dot(g_hi, gexp_ref[...]) + _dot(g_lo, gexp_ref[...])
        w = NSA_WIDTH
        o_nsa = gx[:, 0:w] * ocmp_ref[0] + gx[:, w:2 * w] * osel_ref[0] + gx[:, 2 * w:3 * w] * owin_ref[0]
        o_sb = jnp.concatenate([osb_ref[0, h_i] for h_i in range(SB_HEADS)], axis=1)
        cat = jnp.concatenate([_rms(o_nsa, nnsa_ref[...]), _rms(o_sb, nsb_ref[...])], axis=1)
        x1 = x_ref[0] + gmsa_ref[0] * _dot(cat.astype(bf16), wout_ref[...])
        x1_ref[...] = x1
        h2_ref[...] = (_rms(x1, nffn_ref[...]) * (1.0 + scmlp_ref[0]) + shmlp_ref[0]).astype(bf16)
        acc_ref[...] = jnp.zeros_like(acc_ref)

    h2 = h2_ref[...]
    gt = _dot(h2, wg_ref[...])
    up = _dot(h2, wu_ref[...])
    act = gt * (1.0 / (1.0 + jnp.exp(-gt))) * up
    acc_ref[...] += _dot(act.astype(bf16), wd_ref[...])

    @pl.when(f == pl.num_programs(2) - 1)
    def _():
        y = x1_ref[...] + gmlp_ref[0] * acc_ref[...]
        y_ref[0] = _rms(y, nfin_ref[...])


def _out_ffn_call(x, ocmp, osel, owin, gates, osb_hm, mods4, gexp, nnsa, nsb, wout, nffn, wg, wu, wd, nfin, tm):
    b, t, d = x.shape
    tm = min(tm, t)
    dff = wg.shape[1]
    tf = dff // 2 if (dff // 2) % LANES == 0 else dff
    rows = mods4[0].shape[1]
    mod_spec = (pl.BlockSpec((1, 1, d), lambda i, j, f: (i, 0, 0)) if rows == 1
                else pl.BlockSpec((1, tm, d), lambda i, j, f: (i, j, 0)))

    def flat(c):
        return pl.BlockSpec((1, tm, c), lambda i, j, f: (i, j, 0))

    def const(shape):
        return pl.BlockSpec(shape, lambda i, j, f: (0,) * len(shape))

    return pl.pallas_call(
        _out_ffn_kernel,
        grid=(b, t // tm, dff // tf),
        in_specs=[flat(d), flat(NSA_WIDTH), flat(NSA_WIDTH), flat(NSA_WIDTH), flat(LANES),
                  pl.BlockSpec((1, SB_HEADS, tm, HEAD_DIM), lambda i, j, f: (i, 0, j, 0)),
                  mod_spec, mod_spec, mod_spec, mod_spec,
                  const(gexp.shape), const((1, NSA_WIDTH)), const((1, SB_WIDTH)), const(wout.shape),
                  const((1, d)),
                  pl.BlockSpec((d, tf), lambda i, j, f: (0, f)),
                  pl.BlockSpec((d, tf), lambda i, j, f: (0, f)),
                  pl.BlockSpec((tf, d), lambda i, j, f: (f, 0)),
                  const((1, d))],
        out_specs=flat(d),
        out_shape=jax.ShapeDtypeStruct((b, t, d), f32),
        scratch_shapes=[pltpu.VMEM((tm, d), f32), pltpu.VMEM((tm, d), bf16), pltpu.VMEM((tm, d), f32)],
        compiler_params=_params(3),
        name="out_ffn",
    )(x, ocmp, osel, owin, gates, osb_hm, *mods4, gexp, nnsa.reshape(1, -1), nsb.reshape(1, -1), wout,
      nffn.reshape(1, -1), wg, wu, wd, nfin.reshape(1, -1))


def _gate_expand():
    rows = jnp.arange(LANES)[:, None]
    cols = jnp.arange(3 * NSA_WIDTH)[None, :]
    return (rows == (cols // NSA_WIDTH) * NSA_HEADS + (cols % NSA_WIDTH) // HEAD_DIM).astype(bf16)


def _s_cmp_kernel(kv_ref, qbd_ref, qal_ref, pq_ref, ocmp_ref, sel_ref, *, tn):
    nbp = kv_ref.shape[1]
    kvb = kv_ref[0]
    kc = kvb[:, 0:LANES].astype(bf16)
    vc = kvb[:, LANES:2 * LANES].astype(bf16)
    n_idx = lax.broadcasted_iota(jnp.int32, (nbp, LANES), 0)
    pos = jnp.broadcast_to(pq_ref[...], (nbp, LANES))
    dist_i = pos - (n_idx * CMP_BLOCK + (CMP_BLOCK - 1))
    valid = dist_i >= 0
    slope = qal_ref[...]
    s = _dot(kc, qbd_ref[0]) - slope * dist_i.astype(f32)
    s = jnp.where(valid, s, NEG)
    e = jnp.exp(s - jnp.max(s, axis=0, keepdims=True))
    p = jnp.where(valid, e / jnp.sum(e, axis=0, keepdims=True), 0.0)
    ocmp_ref[0] = _dot_tn(p.astype(bf16), vc)
    imp = p
    for r in range(1, NSA_GROUP):
        imp = imp + pltpu.roll(p, LANES - r * tn, axis=1)
    cur = pos // CMP_BLOCK
    sel_ref[0] = _select_blocks(imp, n_idx, cur, min(N_SELECT, nbp))


def _col_softmax_pv(z_list, v_list):
    m = z_list[0].max(axis=0, keepdims=True)
    for z in z_list[1:]:
        m = jnp.maximum(m, z.max(axis=0, keepdims=True))
    e_list = [jnp.exp(z - m) for z in z_list]
    l = e_list[0].sum(axis=0, keepdims=True)
    for e in e_list[1:]:
        l = l + e.sum(axis=0, keepdims=True)
    inv = 1.0 / l
    o = None
    for e, v in zip(e_list, v_list):
        pv = _dot_tn((e * inv).astype(bf16), v)
        o = pv if o is None else o + pv
    return o


def _s_selwin_kernel(qbd_ref, saug_ref, eaug_ref, kvs_ref, news_ref, nbias_ref,
                     waug_ref, ewin_ref, wmask_ref, kvw_ref, neww_ref,
                     osel_ref, owin_ref):
    qbd = qbd_ref[0]
    kvs = kvs_ref[0]
    z = _dot(kvs[:, 0:LANES].astype(bf16), qbd) + _dot(eaug_ref[...], saug_ref[0])
    new = news_ref[0]
    z_new = _dot(new[:, 0:LANES].astype(bf16), qbd) + nbias_ref[...]
    osel_ref[0] = _col_softmax_pv([z, z_new], [kvs[:, LANES:].astype(bf16), new[:, LANES:].astype(bf16)])
    kvw = kvw_ref[0]
    z = _dot(kvw[:, 0:LANES].astype(bf16), qbd) + _dot(ewin_ref[...], waug_ref[...]) + wmask_ref[...]
    new = neww_ref[0]
    z_new = _dot(new[:, 0:LANES].astype(bf16), qbd) + nbias_ref[...]
    owin_ref[0] = _col_softmax_pv([z, z_new], [kvw[:, LANES:].astype(bf16), new[:, LANES:].astype(bf16)])


def _s_sb_kernel(qbd_ref, kv_ref, new_ref, nmask_ref, tri_ref, o_ref, carry_ref, acc_ref):
    i = pl.program_id(1)
    qbd = qbd_ref[0]
    tri = tri_ref[...]
    sub = tri.shape[0]
    w = SB_WIDTH

    def piece(rows, mask):
        z = _dot(rows[:, 0:w].astype(bf16), qbd)
        c = _neg_softplus(z)
        if mask is not None:
            c = c * mask
        c_hi, c_lo = _split(c)
        run = carry_ref[...]
        parts = []
        for s_i in reversed(range(rows.shape[0] // sub)):
            sl = slice(s_i * sub, (s_i + 1) * sub)
            suf = _dot(tri, jnp.concatenate([c_hi[sl], c_lo[sl]], axis=0))
            parts.append(z[sl] + suf + run)
            run = run + suf[0:1, :]
        a = jnp.exp(jnp.concatenate(parts[::-1], axis=0))
        if mask is not None:
            a = a * mask
        acc_ref[...] += _dot_tn(a.astype(bf16), rows[:, w:].astype(bf16))
        carry_ref[...] = run

    @pl.when(i == 0)
    def _():
        carry_ref[...] = jnp.zeros_like(carry_ref)
        acc_ref[...] = jnp.zeros_like(acc_ref)
        piece(new_ref[0], nmask_ref[...])

    piece(kv_ref[0], None)

    @pl.when(i == pl.num_programs(1) - 1)
    def _():
        o_ref[0] = acc_ref[...]


def _block_diag_q(q, n_groups):
    bs, h, tn, hd = q.shape
    per = h // n_groups
    eye = jnp.eye(n_groups, dtype=q.dtype)
    qg = q.reshape(bs, n_groups, per, tn, hd)
    out = jnp.einsum('bgrtd,gk->bgdkrt', qg, eye).reshape(bs, n_groups * hd, h * tn)
    return jnp.pad(out, ((0, 0), (0, 0), (0, LANES - h * tn)))


def kernel(x_prompt, x_sample, cache_nsa_cmp, cache_nsa_sel, cache_sb, state_nsa_win, page_table,
           c_prompt, c_sample, w_ada, b_ada, norm_attn, norm_ffn, w_in, cmp_pe_k, cmp_k_w1, cmp_k_w2,
           cmp_pe_v, cmp_v_w1, cmp_v_w2, norm_out_nsa, norm_out_sb, w_out, w_gate, w_up, w_down, norm_final):
    assert w_ada.shape[0] == 1, "single-layer trunk"
    bp, t, d = x_prompt.shape
    bs, tn, _ = x_sample.shape
    n_pages = page_table.shape[1]
    page = cache_nsa_cmp.shape[2]
    past = n_pages * page
    nb = t // CMP_BLOCK
    assert nb == LANES and past % CMP_BLOCK == 0 and NSA_HEADS * tn <= LANES
    slopes = _slopes()

    wi = w_in[0]
    c0 = NSA_WIDTH + 3 * KV_COLS
    w_cat = jnp.concatenate([wi[:, :c0], wi[:, c0 + 3 * NSA_HEADS:], wi[:, c0:c0 + 3 * NSA_HEADS],
                             jnp.zeros((d, _C_END - _C_G - 3 * NSA_HEADS), f32)], axis=1).astype(bf16)
    pe_row, w1_big, w2_big = _compress_weights(cmp_pe_k[0], cmp_k_w1[0], cmp_k_w2[0],
                                               cmp_pe_v[0], cmp_v_w1[0], cmp_v_w2[0])
    gexp = _gate_expand()
    wout_b, wg_b, wu_b, wd_b = (w_out[0].astype(bf16), w_gate[0].astype(bf16), w_up[0].astype(bf16),
                                w_down[0].astype(bf16))

    mods = _mods_call(jnp.concatenate([c_prompt, c_sample], axis=0), w_ada[0], b_ada[0])
    mods_p = mods[:bp].reshape(bp, 6, 1, d)
    mods_s = jnp.repeat(mods[bp:].reshape(bs, 6, d), tn, axis=0).reshape(bs * tn, 6, d)
    mp = [mods_p[:, i] for i in range(6)]
    ms = [mods_s[:, i].reshape(1, bs * tn, d) for i in range(6)]

    (kv_cmp, kv_sel, kv_win, kv_sb, gates, q_hm, qsb_hm, ksel_hm, kwin_hm, ksb_hm) = _proj_call(
        x_prompt, mp[0], mp[1], norm_attn[0], w_cat, 512)
    kcvc = _compress_call(kv_cmp.reshape(bp * nb, CMP_BLOCK * KV_COLS), pe_row, w1_big, w2_big)
    kcvc = kcvc.reshape(bp, nb, KV_COLS)
    o_cmp, selb, any_sel = _cmp_call(q_hm, kcvc)
    nq = t // Q_TILE
    any_tile = any_sel.reshape(bp, NSA_KV_HEADS, nq, nb * CMP_BLOCK // SEL_TK, SEL_TK // CMP_BLOCK).max(axis=-1)
    n_words = max(1, any_tile.shape[-1] // 32)
    bits = any_tile.reshape(bp, NSA_KV_HEADS, nq, n_words, -1).astype(jnp.uint32)
    flags = (bits << jnp.arange(bits.shape[-1], dtype=jnp.uint32)).sum(axis=-1, dtype=jnp.uint32)
    flags = lax.bitcast_convert_type(flags, jnp.int32).reshape(-1)
    pos_t = jnp.arange(t, dtype=jnp.int32)
    kconst = jnp.concatenate([jnp.zeros((t, HEAD_DIM), bf16), _alibi_key_cols(pos_t, HEAD_DIM),
                              (pos_t[:, None] // CMP_BLOCK == jnp.arange(nb)[None, :]).astype(bf16)], axis=1)
    qal = _alibi_query_cols(slopes, pos_t, HEAD_DIM)
    o_sel, o_win = _selwin_call(flags, q_hm, qal, selb, ksel_hm, kwin_hm, kconst)
    o_sb = _sb_call(qsb_hm, ksb_hm, _tri_weights(K_TILE))
    y_prompt = _out_ffn_call(x_prompt, o_cmp, o_sel, o_win, gates, o_sb, mp[2:], gexp, norm_out_nsa[0],
                             norm_out_sb[0], wout_b, norm_ffn[0], wg_b, wu_b, wd_b, norm_final, 256)
    win_len = min(WINDOW, t)
    new_win_prompt = kv_win[:, t - win_len:]

    n_tok = bs * tn
    (s_kv_cmp, s_kv_sel, s_kv_win, s_kv_sb, s_gates, s_q_hm, s_qsb_hm, _, _, _) = _proj_call(
        x_sample.reshape(1, n_tok, d), ms[0], ms[1], norm_attn[0], w_cat, n_tok)
    past_cmp, past_sel, past_sb = _gather_call(
        page_table, (cache_nsa_cmp[0].reshape(-1, page, KV_COLS), cache_nsa_sel[0].reshape(-1, page, KV_COLS),
                     cache_sb[0].reshape(-1, page, 2 * SB_WIDTH)))
    nb_past = past // CMP_BLOCK
    nbs = nb_past + 1
    nbp = 2 * LANES
    assert nbs <= nbp - 8
    kc_past = _compress_call(past_cmp.reshape(bs * nb_past, CMP_BLOCK * KV_COLS), pe_row, w1_big, w2_big)
    tail = jnp.pad(s_kv_cmp.reshape(bs, tn, KV_COLS), ((0, 0), (0, CMP_BLOCK - tn), (0, 0)))
    kc_tail = _compress_call(tail.reshape(bs, CMP_BLOCK * KV_COLS), pe_row, w1_big, w2_big)
    s_kcvc = jnp.concatenate([kc_past.reshape(bs, nb_past, KV_COLS), kc_tail[:, None, :],
                              jnp.zeros((bs, nbp - nbs, KV_COLS), f32)], axis=1)

    pos_q = past + jnp.arange(tn, dtype=jnp.int32)
    n_cols = NSA_HEADS * tn
    col_t = jnp.arange(LANES) % tn
    col_h = jnp.minimum(jnp.arange(LANES) // tn, NSA_HEADS - 1)
    col_slope = slopes.reshape(-1)[col_h]
    col_pos = past + col_t
    s_q = s_q_hm.reshape(NSA_HEADS, bs, tn, HEAD_DIM).transpose(1, 0, 2, 3)
    qbd = _block_diag_q(s_q, NSA_KV_HEADS)

    def per_b(shape):
        return pl.BlockSpec((1,) + shape, lambda i: (i,) + (0,) * len(shape))

    def const1(shape):
        return pl.BlockSpec(shape, lambda i: (0,) * len(shape))

    s_ocmp, s_sel = pl.pallas_call(
        functools.partial(_s_cmp_kernel, tn=tn),
        grid=(bs,),
        in_specs=[per_b((nbp, KV_COLS)), per_b((LANES, LANES)), const1((1, LANES)), const1((1, LANES))],
        out_specs=(per_b((LANES, LANES)), per_b((nbp, LANES))),
        out_shape=(jax.ShapeDtypeStruct((bs, LANES, LANES), f32), jax.ShapeDtypeStruct((bs, nbp, LANES), f32)),
        compiler_params=_params(1),
        name="s_cmp_select",
    )(s_kcvc, qbd, col_slope.reshape(1, LANES).astype(f32), col_pos.reshape(1, LANES).astype(jnp.int32))

    col_g0 = (jnp.arange(LANES) // (NSA_GROUP * tn)) * (NSA_GROUP * tn) + col_t
    sel_cols = s_sel[:, :, col_g0]
    n_hot = nbp // 2 + 8
    k_aug = 2 * LANES
    selb_rows = jnp.where(sel_cols[:, :n_hot] > 0.5, 0.0, MASK_BIAS)
    al_rows = jnp.stack([col_slope * CMP_BLOCK, col_slope,
                         -col_slope * CMP_BLOCK * (col_pos // CMP_BLOCK).astype(f32),
                         -col_slope * (col_pos % CMP_BLOCK).astype(f32)], axis=0)
    saug = jnp.concatenate([selb_rows, jnp.broadcast_to(al_rows, (bs, 4, LANES)),
                            jnp.zeros((bs, k_aug - n_hot - 4, LANES), f32)], axis=1).astype(bf16)
    pos_p = jnp.arange(past, dtype=jnp.int32)
    eaug = jnp.concatenate([(pos_p[:, None] // CMP_BLOCK == jnp.arange(n_hot)[None, :]).astype(bf16),
                            _alibi_key_cols(pos_p, k_aug - n_hot)], axis=1)
    win_rows = state_nsa_win.shape[2]
    pos_w = past - win_rows + jnp.arange(win_rows, dtype=jnp.int32)
    ewin = _alibi_key_cols(pos_w, LANES)
    waug = jnp.concatenate([al_rows, jnp.zeros((LANES - 4, LANES), f32)], axis=0).astype(bf16)
    dist_w = col_pos[None, :] - pos_w[:, None]
    wmask = jnp.where((dist_w >= 0) & (dist_w <= WINDOW) & (pos_w[:, None] >= 0), 0.0, MASK_BIAS).astype(f32)
    j_new = jnp.arange(K_TILE)[:, None]
    ok_new = (j_new <= col_t[None, :]) & (j_new < tn)
    nbias = jnp.where(ok_new, -col_slope[None, :] * (col_t[None, :] - j_new).astype(f32), MASK_BIAS).astype(f32)
    pad_new = ((0, 0), (0, K_TILE - tn), (0, 0))
    new_sel = jnp.pad(s_kv_sel.reshape(bs, tn, KV_COLS), pad_new)
    new_win = jnp.pad(s_kv_win.reshape(bs, tn, KV_COLS), pad_new)
    win_buf = state_nsa_win[0].reshape(bs, win_rows, KV_COLS)

    out_o = jax.ShapeDtypeStruct((bs, LANES, LANES), f32)
    s_osel, s_owin = pl.pallas_call(
        _s_selwin_kernel,
        grid=(bs,),
        in_specs=[per_b((LANES, LANES)), per_b((k_aug, LANES)), const1((past, k_aug)),
                  per_b((past, KV_COLS)), per_b((K_TILE, KV_COLS)), const1((K_TILE, LANES)),
                  const1((LANES, LANES)), const1((win_rows, LANES)), const1((win_rows, LANES)),
                  per_b((win_rows, KV_COLS)), per_b((K_TILE, KV_COLS))],
        out_specs=(per_b((LANES, LANES)), per_b((LANES, LANES))),
        out_shape=(out_o, out_o),
        compiler_params=_params(1),
        name="s_sel_win",
    )(qbd, saug, eaug, past_sel, new_sel, nbias, waug, ewin, wmask, win_buf, new_win)

    def nsa_heads(o_full):
        o = o_full[:, :n_cols, :].reshape(bs, NSA_KV_HEADS, NSA_GROUP, tn, NSA_KV_HEADS, HEAD_DIM)
        o = jnp.stack([o[:, gi, :, :, gi, :] for gi in range(NSA_KV_HEADS)], axis=1)
        return o.transpose(0, 3, 1, 2, 4).reshape(1, n_tok, NSA_WIDTH)

    so_cmp = nsa_heads(s_ocmp)
    so_sel = nsa_heads(s_osel)
    so_win = nsa_heads(s_owin)

    s_qsb = s_qsb_hm.reshape(SB_HEADS, bs, tn, HEAD_DIM).transpose(1, 0, 2, 3)
    qbd_sb = _block_diag_q(s_qsb, SB_HEADS)
    new_sb = jnp.pad(s_kv_sb.reshape(bs, tn, 2 * SB_WIDTH), pad_new)
    nmask = ((j_new < col_t[None, :]) & (j_new < tn)).astype(f32)
    chunk = 2048 if past % 2048 == 0 else past
    n_chunks = past // chunk
    tri_rows = (jnp.arange(K_TILE)[None, :] >= jnp.arange(K_TILE)[:, None]).astype(bf16)
    tri_rows = jnp.concatenate([tri_rows, tri_rows], axis=1)
    s_osb = pl.pallas_call(
        _s_sb_kernel,
        grid=(bs, n_chunks),
        in_specs=[pl.BlockSpec((1, SB_WIDTH, LANES), lambda i, j: (i, 0, 0)),
                  pl.BlockSpec((1, chunk, 2 * SB_WIDTH), lambda i, j: (i, n_chunks - 1 - j, 0)),
                  pl.BlockSpec((1, K_TILE, 2 * SB_WIDTH), lambda i, j: (i, 0, 0)),
                  pl.BlockSpec((K_TILE, LANES), lambda i, j: (0, 0)),
                  pl.BlockSpec((K_TILE, 2 * K_TILE), lambda i, j: (0, 0))],
        out_specs=pl.BlockSpec((1, LANES, SB_WIDTH), lambda i, j: (i, 0, 0)),
        out_shape=jax.ShapeDtypeStruct((bs, LANES, SB_WIDTH), f32),
        scratch_shapes=[pltpu.VMEM((1, LANES), f32), pltpu.VMEM((LANES, SB_WIDTH), f32)],
        compiler_params=_params(2),
        name="s_stick_breaking",
    )(qbd_sb, past_sb, new_sb, nmask, tri_rows)
    o = s_osb[:, :SB_HEADS * tn, :].reshape(bs, SB_HEADS, tn, SB_HEADS, HEAD_DIM)
    so_sb = jnp.stack([o[:, hi, :, hi, :] for hi in range(SB_HEADS)], axis=0)
    so_sb = so_sb.reshape(1, SB_HEADS, n_tok, HEAD_DIM)

    y_sample = _out_ffn_call(x_sample.reshape(1, n_tok, d), so_cmp, so_sel, so_win, s_gates, so_sb, ms[2:], gexp,
                             norm_out_nsa[0], norm_out_sb[0], wout_b, norm_ffn[0], wg_b, wu_b, wd_b, norm_final,
                             256).reshape(bs, tn, d)
    all_win = jnp.concatenate([win_buf, s_kv_win.reshape(bs, tn, KV_COLS)], axis=1)
    keep = min(WINDOW, win_rows + tn)
    new_win_sample = all_win[:, win_rows + tn - keep:]

    g_n = NSA_KV_HEADS
    return (y_prompt, y_sample,
            kv_cmp.reshape(1, bp, t, 2, g_n, HEAD_DIM), kv_sel.reshape(1, bp, t, 2, g_n, HEAD_DIM),
            kv_sb.reshape(1, bp, t, 2, SB_HEADS, HEAD_DIM), new_win_prompt.reshape(1, bp, win_len, 2, g_n, HEAD_DIM),
            s_kv_cmp.reshape(1, bs, tn, 2, g_n, HEAD_DIM), s_kv_sel.reshape(1, bs, tn, 2, g_n, HEAD_DIM),
            s_kv_sb.reshape(1, bs, tn, 2, SB_HEADS, HEAD_DIM), new_win_sample.reshape(1, bs, keep, 2, g_n, HEAD_DIM))
```

```python
import functools

import jax
import jax.numpy as jnp
from jax import lax
from jax.experimental import pallas as pl
from jax.experimental.pallas import tpu as pltpu

HEAD_DIM = 64
NSA_HEADS = 8
NSA_KV_HEADS = 2
NSA_GROUP = NSA_HEADS // NSA_KV_HEADS
SB_HEADS = 8
NSA_WIDTH = NSA_HEADS * HEAD_DIM
SB_WIDTH = SB_HEADS * HEAD_DIM
KV_COLS = 2 * NSA_KV_HEADS * HEAD_DIM
CMP_BLOCK = 64
CMP_HIDDEN = 128
N_SELECT = 16
WINDOW = 512
EPS = 1e-6
NEG = -1e30
MASK_BIAS = -1e9
LANES = 128
Q_TILE = 128
K_TILE = 128
QK_SCALE = HEAD_DIM ** -0.5
VMEM_LIMIT = 56 * 1024 * 1024

f32 = jnp.float32
bf16 = jnp.bfloat16


def _dot(a, b):
    return jnp.dot(a, b, preferred_element_type=f32)


def _dot_nt(a, b):
    return lax.dot_general(a, b, (((1,), (1,)), ((), ())), preferred_element_type=f32)


def _dot_tn(a, b):
    return lax.dot_general(a, b, (((0,), (0,)), ((), ())), preferred_element_type=f32)


def _split(x):
    hi = x.astype(bf16)
    lo = (x - hi.astype(f32)).astype(bf16)
    return hi, lo


def _rms(x, g):
    return x * lax.rsqrt(jnp.mean(x * x, axis=-1, keepdims=True) + EPS) * g


def _params(n_axes, vmem=VMEM_LIMIT):
    return pltpu.CompilerParams(dimension_semantics=("arbitrary",) * n_axes, vmem_limit_bytes=vmem)


def _mods_kernel(c_ref, w_ref, b_ref, o_ref):
    c = c_ref[...]
    a = c * (1.0 / (1.0 + jnp.exp(-c)))
    a_hi, a_lo = _split(a)
    w_hi, w_lo = _split(w_ref[...])
    o_ref[...] = _dot(a_hi, w_hi) + _dot(a_lo, w_hi) + _dot(a_hi, w_lo) + b_ref[...]


def _mods_call(c_all, w_ada, b_ada):
    m, d = c_all.shape
    n = w_ada.shape[1]
    tn = 1536 if n % 1536 == 0 else n
    return pl.pallas_call(
        _mods_kernel,
        grid=(n // tn,),
        in_specs=[pl.BlockSpec((m, d), lambda j: (0, 0)),
                  pl.BlockSpec((d, tn), lambda j: (0, j)),
                  pl.BlockSpec((1, tn), lambda j: (0, j))],
        out_specs=pl.BlockSpec((m, tn), lambda j: (0, j)),
        out_shape=jax.ShapeDtypeStruct((m, n), f32),
        compiler_params=_params(1),
        name="mods",
    )(c_all, w_ada, b_ada.reshape(1, n))


_C_Q, _C_KC, _C_KS, _C_KW, _C_QSB, _C_KVSB, _C_G, _C_END = 0, 512, 768, 1024, 1280, 1792, 2816, 2944


def _proj_kernel(x_ref, sh_ref, sc_ref, g_ref, w_ref,
                 kvc_ref, kvs_ref, kvw_ref, kvsb_ref, gate_ref,
                 qn_ref, qsb_ref, ksel_ref, kwin_ref, ksb_ref):
    x = x_ref[0]
    h = _rms(x, g_ref[...]) * (1.0 + sc_ref[0]) + sh_ref[0]
    r = _dot(h.astype(bf16), w_ref[...])
    kvc_ref[0] = r[:, _C_KC:_C_KS]
    kvs_ref[0] = r[:, _C_KS:_C_KW]
    kvw_ref[0] = r[:, _C_KW:_C_QSB]
    kvsb_ref[0] = r[:, _C_KVSB:_C_G]
    gate_ref[0] = 1.0 / (1.0 + jnp.exp(-r[:, _C_G:_C_END]))
    for h_i in range(NSA_HEADS):
        lo = _C_Q + h_i * HEAD_DIM
        qn_ref[0, h_i] = (r[:, lo:lo + HEAD_DIM] * QK_SCALE).astype(bf16)
    for h_i in range(SB_HEADS):
        lo = _C_QSB + h_i * HEAD_DIM
        qsb_ref[0, h_i] = (r[:, lo:lo + HEAD_DIM] * QK_SCALE).astype(bf16)
    for j in range(2 * NSA_KV_HEADS):
        lo = _C_KS + j * HEAD_DIM
        ksel_ref[0, j] = r[:, lo:lo + HEAD_DIM].astype(bf16)
        lo = _C_KW + j * HEAD_DIM
        kwin_ref[0, j] = r[:, lo:lo + HEAD_DIM].astype(bf16)
    for j in range(2 * SB_HEADS):
        lo = _C_KVSB + j * HEAD_DIM
        ksb_ref[0, j] = r[:, lo:lo + HEAD_DIM].astype(bf16)


def _proj_call(x, shift, scale, g, w_cat, tm):
    b, t, d = x.shape
    rows = shift.shape[1]
    mod_spec = (pl.BlockSpec((1, 1, d), lambda i, j: (i, 0, 0)) if rows == 1
                else pl.BlockSpec((1, tm, d), lambda i, j: (i, j, 0)))

    def flat(c):
        return pl.BlockSpec((1, tm, c), lambda i, j: (i, j, 0))

    def heads(n):
        return pl.BlockSpec((1, n, tm, HEAD_DIM), lambda i, j: (i, 0, j, 0))

    out_shape = (
        jax.ShapeDtypeStruct((b, t, KV_COLS), f32),
        jax.ShapeDtypeStruct((b, t, KV_COLS), f32),
        jax.ShapeDtypeStruct((b, t, KV_COLS), f32),
        jax.ShapeDtypeStruct((b, t, 2 * SB_WIDTH), f32),
        jax.ShapeDtypeStruct((b, t, LANES), f32),
        jax.ShapeDtypeStruct((b, NSA_HEADS, t, HEAD_DIM), bf16),
        jax.ShapeDtypeStruct((b, SB_HEADS, t, HEAD_DIM), bf16),
        jax.ShapeDtypeStruct((b, 2 * NSA_KV_HEADS, t, HEAD_DIM), bf16),
        jax.ShapeDtypeStruct((b, 2 * NSA_KV_HEADS, t, HEAD_DIM), bf16),
        jax.ShapeDtypeStruct((b, 2 * SB_HEADS, t, HEAD_DIM), bf16),
    )
    out_specs = (flat(KV_COLS), flat(KV_COLS), flat(KV_COLS), flat(2 * SB_WIDTH), flat(LANES),
                 heads(NSA_HEADS), heads(SB_HEADS), heads(2 * NSA_KV_HEADS), heads(2 * NSA_KV_HEADS),
                 heads(2 * SB_HEADS))
    return pl.pallas_call(
        _proj_kernel,
        grid=(b, t // tm),
        in_specs=[flat(d), mod_spec, mod_spec,
                  pl.BlockSpec((1, d), lambda i, j: (0, 0)),
                  pl.BlockSpec(w_cat.shape, lambda i, j: (0, 0))],
        out_specs=out_specs,
        out_shape=out_shape,
        compiler_params=_params(2),
        name="proj",
    )(x, shift, scale, g.reshape(1, d), w_cat)


def _gelu_tanh(x):
    return 0.5 * x * (1.0 + jnp.tanh(0.7978845608028654 * (x + 0.044715 * (x * x * x))))


def _compress_kernel(x_ref, pe_ref, w1_ref, w2_ref, o_ref, acc_ref):
    k = pl.program_id(1)

    @pl.when(k == 0)
    def _():
        acc_ref[...] = jnp.zeros_like(acc_ref)

    acc_ref[...] += _dot((x_ref[...] + pe_ref[...]).astype(bf16), w1_ref[...])

    @pl.when(k == pl.num_programs(1) - 1)
    def _():
        o_ref[...] = _dot(_gelu_tanh(acc_ref[...]).astype(bf16), w2_ref[...])


def _compress_call(x_rows, pe_row, w1_big, w2_big):
    m, kdim = x_rows.shape
    tm = 256 if m % 256 == 0 else m
    tk = 4096
    n1 = w1_big.shape[1]
    n2 = w2_big.shape[1]
    return pl.pallas_call(
        _compress_kernel,
        grid=(m // tm, kdim // tk),
        in_specs=[pl.BlockSpec((tm, tk), lambda i, k: (i, k)),
                  pl.BlockSpec((1, tk), lambda i, k: (0, k)),
                  pl.BlockSpec((tk, n1), lambda i, k: (k, 0)),
                  pl.BlockSpec((n1, n2), lambda i, k: (0, 0))],
        out_specs=pl.BlockSpec((tm, n2), lambda i, k: (i, 0)),
        out_shape=jax.ShapeDtypeStruct((m, n2), f32),
        scratch_shapes=[pltpu.VMEM((tm, n1), f32)],
        compiler_params=_params(2),
        name="compress",
    )(x_rows, pe_row, w1_big, w2_big)


def _compress_weights(pe_k, w1_k, w2_k, pe_v, w1_v, w2_v):
    g = NSA_KV_HEADS
    w1 = jnp.zeros((CMP_BLOCK, 2, g, HEAD_DIM, 2, g, CMP_HIDDEN), f32)
    w2 = jnp.zeros((2, g, CMP_HIDDEN, 2, g, HEAD_DIM), f32)
    for kv, (a1, a2) in enumerate(((w1_k, w2_k), (w1_v, w2_v))):
        a1 = a1.reshape(CMP_BLOCK, HEAD_DIM, CMP_HIDDEN)
        for gi in range(g):
            w1 = w1.at[:, kv, gi, :, kv, gi, :].set(a1)
            w2 = w2.at[kv, gi, :, kv, gi, :].set(a2)
    w1 = w1.reshape(CMP_BLOCK * 2 * g * HEAD_DIM, 2 * g * CMP_HIDDEN).astype(bf16)
    w2 = w2.reshape(2 * g * CMP_HIDDEN, 2 * g * HEAD_DIM).astype(bf16)
    pe = jnp.stack([jnp.broadcast_to(pe_k[:, None, :], (CMP_BLOCK, g, HEAD_DIM)),
                    jnp.broadcast_to(pe_v[:, None, :], (CMP_BLOCK, g, HEAD_DIM))], axis=1)
    return pe.reshape(1, -1), w1, w2


def _compress_pages_kernel(pt_ref, *refs, n_pages):
    pages = refs[:n_pages]
    pe_ref, w1_ref, w2_ref, o_ref, x_ref = refs[n_pages:]
    for k in range(n_pages):
        x_ref[k] = pages[k][0]
    g_n = NSA_KV_HEADS
    n_blk = w1_ref.shape[3] // CMP_HIDDEN
    for kv in range(2):
        def rows_of(d, kv=kv):
            rows = jnp.concatenate([x_ref[:, (kv * g_n + gi) * HEAD_DIM + d, :] for gi in range(g_n)], axis=0)
            return (rows + pe_ref[kv, pl.ds(d, 1), :]).astype(bf16)

        def body(i, h, kv=kv):
            lhs = jnp.concatenate([rows_of(2 * i), rows_of(2 * i + 1)], axis=1)
            return h + _dot(lhs, w1_ref[kv, i])

        h = lax.fori_loop(0, HEAD_DIM // 2, body, jnp.zeros((g_n * n_pages, n_blk * CMP_HIDDEN), f32), unroll=4)
        act = _gelu_tanh(h).astype(bf16)
        for blk in range(n_blk):
            res = _dot(act[:, blk * CMP_HIDDEN:(blk + 1) * CMP_HIDDEN], w2_ref[kv])
            for gi in range(g_n):
                lo = (kv * g_n + gi) * HEAD_DIM
                o_ref[0, blk, :, lo:lo + HEAD_DIM] = res[gi * n_pages:(gi + 1) * n_pages]


def _compress_pages_call(page_table, pages_t, pe_k, w1_k, w2_k, pe_v, w1_v, w2_v):
    b, n_pages = page_table.shape
    page = pages_t.shape[2]
    n_blk = page // CMP_BLOCK
    eye = jnp.eye(n_blk, dtype=f32)

    def w1_layout(w1):
        w = w1.reshape(CMP_BLOCK, HEAD_DIM, CMP_HIDDEN).transpose(1, 0, 2)
        w = jnp.einsum('djc,ab->dajbc', w, eye).reshape(HEAD_DIM, page, n_blk * CMP_HIDDEN)
        return w.reshape(HEAD_DIM // 2, 2 * page, n_blk * CMP_HIDDEN)

    w1 = jnp.stack([w1_layout(w1_k), w1_layout(w1_v)]).astype(bf16)
    w2 = jnp.stack([w2_k, w2_v]).astype(bf16)
    pe = jnp.stack([jnp.tile(pe_k.T, (1, n_blk)), jnp.tile(pe_v.T, (1, n_blk))])
    out = pl.pallas_call(
        functools.partial(_compress_pages_kernel, n_pages=n_pages),
        grid_spec=pltpu.PrefetchScalarGridSpec(
            num_scalar_prefetch=1, grid=(b,),
            in_specs=[pl.BlockSpec((1, KV_COLS, page), lambda i, pt, k=k: (pt[i, k], 0, 0)) for k in range(n_pages)]
            + [pl.BlockSpec(pe.shape, lambda i, pt: (0, 0, 0)),
               pl.BlockSpec(w1.shape, lambda i, pt: (0, 0, 0, 0)),
               pl.BlockSpec(w2.shape, lambda i, pt: (0, 0, 0))],
            out_specs=pl.BlockSpec((1, n_blk, n_pages, KV_COLS), lambda i, pt: (i, 0, 0, 0)),
            scratch_shapes=[pltpu.VMEM((n_pages, KV_COLS, page), f32)]),
        out_shape=jax.ShapeDtypeStruct((b, n_blk, n_pages, KV_COLS), f32),
        compiler_params=_params(1),
        name="compress_pages",
    )(page_table, *([pages_t] * n_pages), pe, w1, w2)
    return out.transpose(0, 2, 1, 3).reshape(b, n_pages * n_blk, KV_COLS)


def _select_blocks(imp, n_idx, cur, n_rounds):
    forced = (n_idx == 0) | (n_idx == cur) | (n_idx == cur - 1)
    in_range = n_idx <= cur
    v = jnp.where(forced, 8.0, imp)
    v = jnp.where(in_range, v, -1.0)
    sel = jnp.zeros(imp.shape, f32)
    big = jnp.int32(1 << 20)
    for _ in range(n_rounds):
        mx = jnp.max(v, axis=0, keepdims=True)
        idx = jnp.min(jnp.where(v == mx, n_idx, big), axis=0, keepdims=True)
        hit = n_idx == idx
        sel = jnp.where(hit, 1.0, sel)
        v = jnp.where(hit, -2.0, v)
    return jnp.where(in_range, sel, 0.0)


def _slope(g, r):
    return jnp.where(g == 0, 2.0 ** -(r + 1), 2.0 ** -(NSA_GROUP + r + 1)).astype(f32)


def _cmp_kernel(q_ref, kv_ref, ocmp_ref, selb_ref, any_ref):
    g = pl.program_id(1)
    qi = pl.program_id(2)
    tq = q_ref.shape[2]
    nb = kv_ref.shape[1]
    kvb = kv_ref[0]
    kc = jnp.where(g == 0, kvb[:, 0:HEAD_DIM], kvb[:, HEAD_DIM:2 * HEAD_DIM]).astype(bf16)
    vc = jnp.where(g == 0, kvb[:, 2 * HEAD_DIM:3 * HEAD_DIM], kvb[:, 3 * HEAD_DIM:4 * HEAD_DIM]).astype(bf16)
    n_idx = lax.broadcasted_iota(jnp.int32, (nb, tq), 0)
    pos = qi * tq + lax.broadcasted_iota(jnp.int32, (nb, tq), 1)
    dist_i = pos - (n_idx * CMP_BLOCK + (CMP_BLOCK - 1))
    valid = dist_i >= 0
    dist = dist_i.astype(f32)
    imp = jnp.zeros((nb, tq), f32)
    outs = []
    for r in range(NSA_GROUP):
        s = _dot_nt(kc, q_ref[0, r]) - _slope(g, r) * dist
        s = jnp.where(valid, s, NEG)
        e = jnp.exp(s - jnp.max(s, axis=0, keepdims=True))
        p = jnp.where(valid, e / jnp.sum(e, axis=0, keepdims=True), 0.0)
        imp = imp + p
        outs.append(_dot_tn(p.astype(bf16), vc))
    ocmp_ref[0] = jnp.concatenate(outs, axis=1)
    cur = pos // CMP_BLOCK
    sel = _select_blocks(imp, n_idx, cur, min(N_SELECT, nb))
    sel_t = sel.T
    selb_ref[0, 0] = jnp.where(sel_t > 0.5, 0.0, MASK_BIAS).astype(bf16)
    any_ref[0, 0, 0] = jnp.max(sel_t, axis=0, keepdims=True).astype(jnp.int32)


def _cmp_call(q_hm, kcvc):
    b, _, t, _ = q_hm.shape
    nb = kcvc.shape[1]
    tq = Q_TILE
    nq = t // tq
    return pl.pallas_call(
        _cmp_kernel,
        grid=(b, NSA_KV_HEADS, nq),
        in_specs=[pl.BlockSpec((1, NSA_GROUP, tq, HEAD_DIM), lambda i, g, j: (i, g, j, 0)),
                  pl.BlockSpec((1, nb, KV_COLS), lambda i, g, j: (i, 0, 0))],
        out_specs=(pl.BlockSpec((1, tq, NSA_GROUP * HEAD_DIM), lambda i, g, j: (i, j, g)),
                   pl.BlockSpec((1, 1, tq, nb), lambda i, g, j: (i, g, j, 0)),
                   pl.BlockSpec((1, 1, 1, 1, nb), lambda i, g, j: (i, g, j, 0, 0))),
        out_shape=(jax.ShapeDtypeStruct((b, t, NSA_WIDTH), f32),
                   jax.ShapeDtypeStruct((b, NSA_KV_HEADS, t, nb), bf16),
                   jax.ShapeDtypeStruct((b, NSA_KV_HEADS, nq, 1, nb), jnp.int32)),
        compiler_params=_params(3),
        name="cmp_select",
    )(q_hm, kcvc)


SEL_TK = 256


def _softmax_tile(k_tile, qa, v_tile, m_ref, l_ref, acc_ref, mask):
    s = _dot_nt(k_tile, qa)
    if mask is not None:
        s = jnp.where(mask, s, MASK_BIAS)
    m_old = m_ref[...]
    m_new = jnp.maximum(m_old, jnp.max(s, axis=0, keepdims=True))
    alpha = jnp.exp(m_old - m_new)
    p = jnp.exp(s - m_new)
    l_ref[...] = alpha * l_ref[...] + jnp.sum(p, axis=0, keepdims=True)
    acc_ref[...] = alpha * acc_ref[...] + _dot_tn(v_tile, p.astype(bf16))
    m_ref[...] = m_new


def _selwin_kernel(flag_ref, q_ref, qal_ref, selb_ref, ks_ref, vs_ref, kw_ref, vw_ref, kconst_ref,
                   osel_ref, owin_ref, kaug_s, kaug_w, qaug, m_ref, l_ref, acc_ref):
    b = pl.program_id(0)
    g = pl.program_id(1)
    qi = pl.program_id(2)
    tq = q_ref.shape[2]
    tk = SEL_TK
    rows = NSA_GROUP * tq
    t_all = ks_ref.shape[2]

    @pl.when(qi == 0)
    def _():
        kaug_s[...] = kconst_ref[...]
        kaug_s[:, 0:HEAD_DIM] = ks_ref[0, 0]
        kaug_w[...] = kconst_ref[:, 0:LANES]
        kaug_w[:, 0:HEAD_DIM] = kw_ref[0, 0]

    qaug[:, 0:HEAD_DIM] = q_ref[0].reshape(rows, HEAD_DIM)
    qaug[:, HEAD_DIM:LANES] = qal_ref[0].reshape(rows, HEAD_DIM)
    sb = selb_ref[0, 0]
    for r in range(NSA_GROUP):
        qaug[r * tq:(r + 1) * tq, LANES:] = sb
    qa = qaug[...]
    qa_w = qaug[:, 0:LANES]

    q0 = qi * tq
    jd = q0 // tk
    q_pos = q0 + (lax.broadcasted_iota(jnp.int32, (tk, rows), 1) & (tq - 1))
    k_rel = lax.broadcasted_iota(jnp.int32, (tk, rows), 0)

    def reset():
        m_ref[...] = jnp.full(m_ref.shape, NEG, f32)
        l_ref[...] = jnp.zeros(l_ref.shape, f32)
        acc_ref[...] = jnp.zeros(acc_ref.shape, f32)

    def finish(o_ref):
        o_t = acc_ref[...] / l_ref[...]
        o_ref[0] = jnp.concatenate([o_t[:, r * tq:(r + 1) * tq].T for r in range(NSA_GROUP)], axis=1)

    def tile(kaug, v_ref, qmat, j, mask):
        off = pl.multiple_of(j * tk, tk)
        _softmax_tile(kaug[pl.ds(off, tk), :], qmat, v_ref[0, 0, pl.ds(off, tk), :], m_ref, l_ref, acc_ref, mask)

    reset()
    n_words = max(1, (t_all // tk) // 32)
    base = ((b * NSA_KV_HEADS + g) * (t_all // tq) + qi) * n_words

    def sel_body(j, carry):
        word = flag_ref[base + (j >> 5)]

        @pl.when(((word >> (j & 31)) & 1) == 1)
        def _():
            tile(kaug_s, vs_ref, qa, j, None)

        return carry

    lax.fori_loop(0, jd, sel_body, 0)
    tile(kaug_s, vs_ref, qa, jd, jd * tk + k_rel <= q_pos)

    n_wt = WINDOW // tk + 1
    start = jnp.maximum(jd - (n_wt - 1), 0)
    off_w = pl.multiple_of(start * tk, tk)
    s_w = _dot_nt(kaug_w[pl.ds(off_w, n_wt * tk), :], qa_w)
    kp = start * tk + lax.broadcasted_iota(jnp.int32, s_w.shape, 0)
    qp = q0 + (lax.broadcasted_iota(jnp.int32, s_w.shape, 1) & (tq - 1))
    s_w = jnp.where(kp <= qp, jnp.where(kp >= qp - WINDOW, s_w, MASK_BIAS), MASK_BIAS)
    p_w = jnp.exp(s_w - jnp.max(s_w, axis=0, keepdims=True))
    p_w = p_w * (1.0 / jnp.sum(p_w, axis=0, keepdims=True))
    ow_t = _dot_tn(vw_ref[0, 0, pl.ds(off_w, n_wt * tk), :], p_w.astype(bf16))
    owin_ref[0] = jnp.concatenate([ow_t[:, r * tq:(r + 1) * tq].T for r in range(NSA_GROUP)], axis=1)
    finish(osel_ref)


def _selwin_call(flags, q_hm, qal, selb, ksel_hm, kwin_hm, kconst):
    b, _, t, _ = q_hm.shape
    tq = Q_TILE
    nq = t // tq
    ka = kconst.shape[1]
    g_n = NSA_KV_HEADS

    def kv_spec(off):
        return pl.BlockSpec((1, 1, t, HEAD_DIM), lambda i, g, j, fl: (i, off + g, 0, 0))

    out_spec = pl.BlockSpec((1, tq, NSA_GROUP * HEAD_DIM), lambda i, g, j, fl: (i, j, g))
    return pl.pallas_call(
        _selwin_kernel,
        grid_spec=pltpu.PrefetchScalarGridSpec(
            num_scalar_prefetch=1, grid=(b, g_n, nq),
            in_specs=[pl.BlockSpec((1, NSA_GROUP, tq, HEAD_DIM), lambda i, g, j, fl: (i, g, j, 0)),
                      pl.BlockSpec((1, NSA_GROUP, tq, HEAD_DIM), lambda i, g, j, fl: (g, 0, j, 0)),
                      pl.BlockSpec((1, 1, tq, ka - LANES), lambda i, g, j, fl: (i, g, j, 0)),
                      kv_spec(0), kv_spec(g_n), kv_spec(0), kv_spec(g_n),
                      pl.BlockSpec((t, ka), lambda i, g, j, fl: (0, 0))],
            out_specs=(out_spec, out_spec),
            scratch_shapes=[pltpu.VMEM((t, ka), bf16), pltpu.VMEM((t, LANES), bf16),
                            pltpu.VMEM((NSA_GROUP * tq, ka), bf16),
                            pltpu.VMEM((1, NSA_GROUP * tq), f32), pltpu.VMEM((1, NSA_GROUP * tq), f32),
                            pltpu.VMEM((HEAD_DIM, NSA_GROUP * tq), f32)]),
        out_shape=(jax.ShapeDtypeStruct((b, t, NSA_WIDTH), f32), jax.ShapeDtypeStruct((b, t, NSA_WIDTH), f32)),
        compiler_params=_params(3),
        name="sel_win",
    )(flags, q_hm, qal, selb, ksel_hm, ksel_hm, kwin_hm, kwin_hm, kconst)


def _alibi_key_cols(pos, width):
    cols = jnp.stack([pos // CMP_BLOCK, pos % CMP_BLOCK, jnp.ones_like(pos), jnp.ones_like(pos)], axis=1)
    return jnp.pad(cols.astype(f32), ((0, 0), (0, width - 4))).astype(bf16)


def _alibi_query_cols(slopes, pos, width):
    s = slopes[..., None]
    a = (pos // CMP_BLOCK).astype(f32)
    r = (pos % CMP_BLOCK).astype(f32)
    cols = jnp.stack([jnp.broadcast_to(s * CMP_BLOCK, s.shape[:-1] + pos.shape),
                      jnp.broadcast_to(s, s.shape[:-1] + pos.shape),
                      -s * CMP_BLOCK * a, -s * r], axis=-1)
    pad = [(0, 0)] * (cols.ndim - 1) + [(0, width - 4)]
    return jnp.pad(cols, pad).astype(bf16)


def _slopes():
    h = jnp.arange(1, NSA_HEADS + 1, dtype=f32)
    return jnp.exp2(-8.0 * h / NSA_HEADS).reshape(NSA_KV_HEADS, NSA_GROUP)


def _neg_softplus(z):
    return -(jnp.maximum(z, 0.0) + jnp.log(1.0 + jnp.exp(-jnp.abs(z))))


SB_TILE = 512
SB_DEAD_LOG = -105.0


def _sb_kernel(q_ref, k_ref, v_ref, tri_ref, o_ref, carry_ref, acc_ref):
    qi = pl.program_id(2)
    tq = q_ref.shape[2]
    tk = tq
    sub = tri_ref.shape[1] // 2
    q = q_ref[0, 0]
    tri = tri_ref[...]

    def tile(j, mask):
        off = pl.multiple_of(j * tk, tk)
        z = _dot_nt(q, k_ref[0, 0, pl.ds(off, tk), :])
        c = _neg_softplus(z)
        if mask is not None:
            c = jnp.where(mask, c, 0.0)
        c_hi, c_lo = _split(c)
        run = carry_ref[...]
        parts = [None] * (tk // sub)
        for i in reversed(range(tk // sub)):
            sl = slice(i * sub, (i + 1) * sub)
            res = _dot(jnp.concatenate([c_hi[:, sl], c_lo[:, sl]], axis=1), tri)
            parts[i] = z[:, sl] + res[:, :sub] + run
            run = run + res[:, sub:]
        a = jnp.exp(jnp.concatenate(parts, axis=1))
        if mask is not None:
            a = jnp.where(mask, a, 0.0)
        acc_ref[...] += _dot(a.astype(bf16), v_ref[0, 0, pl.ds(off, tk), :])
        carry_ref[...] = run

    carry_ref[...] = jnp.zeros(carry_ref.shape, f32)
    acc_ref[...] = jnp.zeros(acc_ref.shape, f32)
    t_idx = lax.broadcasted_iota(jnp.int32, (tq, tk), 0)
    k_idx = lax.broadcasted_iota(jnp.int32, (tq, tk), 1)
    tile(qi, k_idx < t_idx)

    def body(state):
        tile(qi - 1 - state[0], None)
        return state[0] + 1, jnp.max(carry_ref[...])

    lax.while_loop(lambda st: jnp.logical_and(st[0] < qi, st[1] > SB_DEAD_LOG), body,
                   (jnp.int32(0), jnp.max(carry_ref[...])))
    o_ref[0, 0] = acc_ref[...]


def _sb_call(q_hm, kv_hm, tri):
    b, h, t, _ = q_hm.shape
    tq = SB_TILE
    return pl.pallas_call(
        _sb_kernel,
        grid=(b, h, t // tq),
        in_specs=[pl.BlockSpec((1, 1, tq, HEAD_DIM), lambda i, hh, j: (i, hh, j, 0)),
                  pl.BlockSpec((1, 1, t, HEAD_DIM), lambda i, hh, j: (i, hh, 0, 0)),
                  pl.BlockSpec((1, 1, t, HEAD_DIM), lambda i, hh, j: (i, h + hh, 0, 0)),
                  pl.BlockSpec(tri.shape, lambda i, hh, j: (0, 0))],
        out_specs=pl.BlockSpec((1, 1, tq, HEAD_DIM), lambda i, hh, j: (i, hh, j, 0)),
        out_shape=jax.ShapeDtypeStruct((b, h, t, HEAD_DIM), f32),
        scratch_shapes=[pltpu.VMEM((tq, tri.shape[1] // 2), f32), pltpu.VMEM((tq, HEAD_DIM), f32)],
        compiler_params=_params(3),
        name="stick_breaking",
    )(q_hm, kv_hm, kv_hm, tri)


def _tri_weights(n):
    j = jnp.arange(n)[:, None]
    s = jnp.arange(n)[None, :]
    w = jnp.concatenate([(j >= s).astype(bf16), jnp.ones((n, n), bf16)], axis=1)
    return jnp.concatenate([w, w], axis=0)


def _out_ffn_kernel(x_ref, ocmp_ref, osel_ref, owin_ref, gate_ref, osb_ref,
                    gmsa_ref, shmlp_ref, scmlp_ref, gmlp_ref,
                    gexp_ref, nnsa_ref, nsb_ref, wout_ref, nffn_ref, wg_ref, wu_ref, wd_ref, nfin_ref,
                    y_ref, x1_ref, h2_ref, acc_ref):
    f = pl.program_id(2)

    @pl.when(f == 0)
    def _():
        g_hi, g_lo = _split(gate_ref[0])
        gx = _dot(g_hi, gexp_ref[...]) + _dot(g_lo, gexp_ref[...])
        w = NSA_WIDTH
        o_nsa = gx[:, 0:w] * ocmp_ref[0] + gx[:, w:2 * w] * osel_ref[0] + gx[:, 2 * w:3 * w] * owin_ref[0]
        o_sb = jnp.concatenate([osb_ref[0, h_i] for h_i in range(SB_HEADS)], axis=1)
        cat = jnp.concatenate([_rms(o_nsa, nnsa_ref[...]), _rms(o_sb, nsb_ref[...])], axis=1)
        x1 = x_ref[0] + gmsa_ref[0] * _dot(cat.astype(bf16), wout_ref[...])
        x1_ref[...] = x1
        h2_ref[...] = (_rms(x1, nffn_ref[...]) * (1.0 + scmlp_ref[0]) + shmlp_ref[0]).astype(bf16)
        acc_ref[...] = jnp.zeros_like(acc_ref)

    h2 = h2_ref[...]
    gt = _dot(h2, wg_ref[...])
    up = _dot(h2, wu_ref[...])
    act = gt * (1.0 / (1.0 + jnp.exp(-gt))) * up
    acc_ref[...] += _dot(act.astype(bf16), wd_ref[...])

    @pl.when(f == pl.num_programs(2) - 1)
    def _():
        y = x1_ref[...] + gmlp_ref[0] * acc_ref[...]
        y_ref[0] = _rms(y, nfin_ref[...])


def _out_ffn_call(x, ocmp, osel, owin, gates, osb_hm, mods4, gexp, nnsa, nsb, wout, nffn, wg, wu, wd, nfin, tm):
    b, t, d = x.shape
    tm = min(tm, t)
    dff = wg.shape[1]
    tf = dff // 2 if (dff // 2) % LANES == 0 else dff
    rows = mods4[0].shape[1]
    mod_spec = (pl.BlockSpec((1, 1, d), lambda i, j, f: (i, 0, 0)) if rows == 1
                else pl.BlockSpec((1, tm, d), lambda i, j, f: (i, j, 0)))

    def flat(c):
        return pl.BlockSpec((1, tm, c), lambda i, j, f: (i, j, 0))

    def const(shape):
        return pl.BlockSpec(shape, lambda i, j, f: (0,) * len(shape))

    return pl.pallas_call(
        _out_ffn_kernel,
        grid=(b, t // tm, dff // tf),
        in_specs=[flat(d), flat(NSA_WIDTH), flat(NSA_WIDTH), flat(NSA_WIDTH), flat(LANES),
                  pl.BlockSpec((1, SB_HEADS, tm, HEAD_DIM), lambda i, j, f: (i, 0, j, 0)),
                  mod_spec, mod_spec, mod_spec, mod_spec,
                  const(gexp.shape), const((1, NSA_WIDTH)), const((1, SB_WIDTH)), const(wout.shape),
                  const((1, d)),
                  pl.BlockSpec((d, tf), lambda i, j, f: (0, f)),
                  pl.BlockSpec((d, tf), lambda i, j, f: (0, f)),
                  pl.BlockSpec((tf, d), lambda i, j, f: (f, 0)),
                  const((1, d))],
        out_specs=flat(d),
        out_shape=jax.ShapeDtypeStruct((b, t, d), f32),
        scratch_shapes=[pltpu.VMEM((tm, d), f32), pltpu.VMEM((tm, d), bf16), pltpu.VMEM((tm, d), f32)],
        compiler_params=_params(3),
        name="out_ffn",
    )(x, ocmp, osel, owin, gates, osb_hm, *mods4, gexp, nnsa.reshape(1, -1), nsb.reshape(1, -1), wout,
      nffn.reshape(1, -1), wg, wu, wd, nfin.reshape(1, -1))


def _gate_expand():
    rows = jnp.arange(LANES)[:, None]
    cols = jnp.arange(3 * NSA_WIDTH)[None, :]
    return (rows == (cols // NSA_WIDTH) * NSA_HEADS + (cols % NSA_WIDTH) // HEAD_DIM).astype(bf16)


def _s_cmp_kernel(kv_ref, qbd_ref, qal_ref, pq_ref, ocmp_ref, sel_ref, *, tn):
    nbp = kv_ref.shape[1]
    kvb = kv_ref[0]
    kc = kvb[:, 0:LANES].astype(bf16)
    vc = kvb[:, LANES:2 * LANES].astype(bf16)
    n_idx = lax.broadcasted_iota(jnp.int32, (nbp, LANES), 0)
    pos = jnp.broadcast_to(pq_ref[...], (nbp, LANES))
    dist_i = pos - (n_idx * CMP_BLOCK + (CMP_BLOCK - 1))
    valid = dist_i >= 0
    slope = qal_ref[...]
    s = _dot(kc, qbd_ref[0]) - slope * dist_i.astype(f32)
    s = jnp.where(valid, s, NEG)
    e = jnp.exp(s - jnp.max(s, axis=0, keepdims=True))
    p = jnp.where(valid, e / jnp.sum(e, axis=0, keepdims=True), 0.0)
    ocmp_ref[0] = _dot_tn(p.astype(bf16), vc)
    imp = p
    for r in range(1, NSA_GROUP):
        imp = imp + pltpu.roll(p, LANES - r * tn, axis=1)
    cur = pos // CMP_BLOCK
    sel_ref[0] = _select_blocks(imp, n_idx, cur, min(N_SELECT, nbp))


S_SEL_PAGES = 16


def _row_softmax_pv(z_list, vt_list):
    m = z_list[0].max(axis=1, keepdims=True)
    for z in z_list[1:]:
        m = jnp.maximum(m, z.max(axis=1, keepdims=True))
    e_list = [jnp.exp(z - m) for z in z_list]
    l = e_list[0].sum(axis=1, keepdims=True)
    for e in e_list[1:]:
        l = l + e.sum(axis=1, keepdims=True)
    inv = 1.0 / l
    o = None
    for e, vt in zip(e_list, vt_list):
        pv = _dot_nt((e * inv).astype(bf16), vt)
        o = pv if o is None else o + pv
    return o


def _s_selwin_kernel(pt_ref, *refs, n_pages_step):
    pages = refs[:n_pages_step]
    (qt_ref, saug_ref, eaug_ref, news_ref, nbias_ref, waug_ref, ewin_ref, wmask_ref, kvw_ref, neww_ref,
     osel_ref, owin_ref, m_ref, l_ref, acc_ref) = refs[n_pages_step:]
    j = pl.program_id(1)
    qt = qt_ref[0]
    half = LANES

    def online(z, vt):
        m_old = m_ref[...]
        m_new = jnp.maximum(m_old, z.max(axis=1, keepdims=True))
        alpha = jnp.exp(m_old - m_new)
        p = jnp.exp(z - m_new)
        l_ref[...] = alpha * l_ref[...] + p.sum(axis=1, keepdims=True)
        acc_ref[...] = alpha * acc_ref[...] + _dot_nt(p.astype(bf16), vt)
        m_ref[...] = m_new

    @pl.when(j == 0)
    def _():
        kvw = kvw_ref[0]
        z_w = _dot(qt, kvw[0:half].astype(bf16)) + _dot(waug_ref[...], ewin_ref[...]) + wmask_ref[...]
        new_w = neww_ref[0]
        z_n = _dot(qt, new_w[0:half].astype(bf16)) + nbias_ref[...]
        owin_ref[0] = _row_softmax_pv([z_w, z_n], [kvw[half:].astype(bf16), new_w[half:].astype(bf16)])
        m_ref[...] = jnp.full(m_ref.shape, NEG, f32)
        l_ref[...] = jnp.zeros(l_ref.shape, f32)
        acc_ref[...] = jnp.zeros(acc_ref.shape, f32)
        new_s = news_ref[0]
        online(_dot(qt, new_s[0:half].astype(bf16)) + nbias_ref[...], new_s[half:].astype(bf16))

    kt = jnp.concatenate([p_ref[0, 0:half, :].astype(bf16) for p_ref in pages], axis=1)
    vt = jnp.concatenate([p_ref[0, half:, :].astype(bf16) for p_ref in pages], axis=1)
    online(_dot(qt, kt) + _dot(saug_ref[0], eaug_ref[...]), vt)

    @pl.when(j == pl.num_programs(1) - 1)
    def _():
        osel_ref[0] = acc_ref[...] / l_ref[...]


S_SB_PAGES = 8


def _s_sb_kernel(pt_ref, *refs, n_pages_step):
    pages = refs[:n_pages_step]
    qt_ref, new_ref, nmask_ref, tri_ref, o_ref, carry_ref, acc_ref = refs[n_pages_step:]
    j = pl.program_id(1)
    qt = qt_ref[0]
    tri = tri_ref[...]
    w = SB_WIDTH
    n_tok = LANES

    def pieces(tiles, mask):
        zs, ress = [], []
        for t_f in tiles:
            z = _dot(qt, t_f[0:w].astype(bf16))
            c = _neg_softplus(z)
            if mask is not None:
                c = c * mask
            c_hi, c_lo = _split(c)
            zs.append(z)
            ress.append(_dot(jnp.concatenate([c_hi, c_lo], axis=1), tri))
        run = carry_ref[...]
        acc = acc_ref[...]
        for t_f, z, res in zip(tiles, zs, ress):
            a = jnp.exp(z + res[:, :n_tok] + run)
            if mask is not None:
                a = a * mask
            acc = acc + _dot_nt(a.astype(bf16), t_f[w:].astype(bf16))
            run = run + res[:, n_tok:]
        carry_ref[...] = run
        acc_ref[...] = acc

    @pl.when(j == 0)
    def _():
        carry_ref[...] = jnp.zeros_like(carry_ref)
        acc_ref[...] = jnp.zeros_like(acc_ref)
        pieces([new_ref[0]], nmask_ref[...])

    @pl.when(jnp.max(carry_ref[...]) > SB_DEAD_LOG)
    def _():
        pieces([p_ref[0] for p_ref in pages], None)

    @pl.when(j == pl.num_programs(1) - 1)
    def _():
        o_ref[0] = acc_ref[...]


def _block_diag_q(q, n_groups):
    bs, h, tn, hd = q.shape
    per = h // n_groups
    eye = jnp.eye(n_groups, dtype=q.dtype)
    qg = q.reshape(bs, n_groups, per, tn, hd)
    out = jnp.einsum('bgrtd,gk->bgdkrt', qg, eye).reshape(bs, n_groups * hd, h * tn)
    return jnp.pad(out, ((0, 0), (0, 0), (0, LANES - h * tn)))


def kernel(x_prompt, x_sample, cache_nsa_cmp, cache_nsa_sel, cache_sb, state_nsa_win, page_table,
           c_prompt, c_sample, w_ada, b_ada, norm_attn, norm_ffn, w_in, cmp_pe_k, cmp_k_w1, cmp_k_w2,
           cmp_pe_v, cmp_v_w1, cmp_v_w2, norm_out_nsa, norm_out_sb, w_out, w_gate, w_up, w_down, norm_final):
    assert w_ada.shape[0] == 1, "single-layer trunk"
    bp, t, d = x_prompt.shape
    bs, tn, _ = x_sample.shape
    n_pages = page_table.shape[1]
    page = cache_nsa_cmp.shape[2]
    past = n_pages * page
    nb = t // CMP_BLOCK
    assert nb == LANES and past % CMP_BLOCK == 0 and NSA_HEADS * tn <= LANES
    slopes = _slopes()

    wi = w_in[0]
    c0 = NSA_WIDTH + 3 * KV_COLS
    w_cat = jnp.concatenate([wi[:, :c0], wi[:, c0 + 3 * NSA_HEADS:], wi[:, c0:c0 + 3 * NSA_HEADS],
                             jnp.zeros((d, _C_END - _C_G - 3 * NSA_HEADS), f32)], axis=1).astype(bf16)
    pe_row, w1_big, w2_big = _compress_weights(cmp_pe_k[0], cmp_k_w1[0], cmp_k_w2[0],
                                               cmp_pe_v[0], cmp_v_w1[0], cmp_v_w2[0])
    gexp = _gate_expand()
    wout_b, wg_b, wu_b, wd_b = (w_out[0].astype(bf16), w_gate[0].astype(bf16), w_up[0].astype(bf16),
                                w_down[0].astype(bf16))

    mods = _mods_call(jnp.concatenate([c_prompt, c_sample], axis=0), w_ada[0], b_ada[0])
    mods_p = mods[:bp].reshape(bp, 6, 1, d)
    mods_s = jnp.repeat(mods[bp:].reshape(bs, 6, d), tn, axis=0).reshape(bs * tn, 6, d)
    mp = [mods_p[:, i] for i in range(6)]
    ms = [mods_s[:, i].reshape(1, bs * tn, d) for i in range(6)]

    (kv_cmp, kv_sel, kv_win, kv_sb, gates, q_hm, qsb_hm, ksel_hm, kwin_hm, ksb_hm) = _proj_call(
        x_prompt, mp[0], mp[1], norm_attn[0], w_cat, 512)
    kcvc = _compress_call(kv_cmp.reshape(bp * nb, CMP_BLOCK * KV_COLS), pe_row, w1_big, w2_big)
    kcvc = kcvc.reshape(bp, nb, KV_COLS)
    o_cmp, selb, any_sel = _cmp_call(q_hm, kcvc)
    nq = t // Q_TILE
    any_tile = any_sel.reshape(bp, NSA_KV_HEADS, nq, nb * CMP_BLOCK // SEL_TK, SEL_TK // CMP_BLOCK).max(axis=-1)
    n_words = max(1, any_tile.shape[-1] // 32)
    bits = any_tile.reshape(bp, NSA_KV_HEADS, nq, n_words, -1).astype(jnp.uint32)
    flags = (bits << jnp.arange(bits.shape[-1], dtype=jnp.uint32)).sum(axis=-1, dtype=jnp.uint32)
    flags = lax.bitcast_convert_type(flags, jnp.int32).reshape(-1)
    pos_t = jnp.arange(t, dtype=jnp.int32)
    kconst = jnp.concatenate([jnp.zeros((t, HEAD_DIM), bf16), _alibi_key_cols(pos_t, HEAD_DIM),
                              (pos_t[:, None] // CMP_BLOCK == jnp.arange(nb)[None, :]).astype(bf16)], axis=1)
    qal = _alibi_query_cols(slopes, pos_t, HEAD_DIM)
    o_sel, o_win = _selwin_call(flags, q_hm, qal, selb, ksel_hm, kwin_hm, kconst)
    o_sb = _sb_call(qsb_hm, ksb_hm, _tri_weights(K_TILE))
    y_prompt = _out_ffn_call(x_prompt, o_cmp, o_sel, o_win, gates, o_sb, mp[2:], gexp, norm_out_nsa[0],
                             norm_out_sb[0], wout_b, norm_ffn[0], wg_b, wu_b, wd_b, norm_final, 256)
    win_len = min(WINDOW, t)
    new_win_prompt = kv_win[:, t - win_len:]

    n_tok = bs * tn
    (s_kv_cmp, s_kv_sel, s_kv_win, s_kv_sb, s_gates, s_q_hm, s_qsb_hm, _, _, _) = _proj_call(
        x_sample.reshape(1, n_tok, d), ms[0], ms[1], norm_attn[0], w_cat, n_tok)
    nb_past = past // CMP_BLOCK
    nbs = nb_past + 1
    nbp = 2 * LANES
    assert nbs <= nbp - 8
    cmp_pages = cache_nsa_cmp[0].transpose(0, 2, 3, 4, 1).reshape(-1, KV_COLS, page)
    kc_past = _compress_pages_call(page_table, cmp_pages, cmp_pe_k[0], cmp_k_w1[0], cmp_k_w2[0],
                                   cmp_pe_v[0], cmp_v_w1[0], cmp_v_w2[0])
    tail = jnp.pad(s_kv_cmp.reshape(bs, tn, KV_COLS), ((0, 0), (0, CMP_BLOCK - tn), (0, 0)))
    kc_tail = _compress_call(tail.reshape(bs, CMP_BLOCK * KV_COLS), pe_row, w1_big, w2_big)
    s_kcvc = jnp.concatenate([kc_past.reshape(bs, nb_past, KV_COLS), kc_tail[:, None, :],
                              jnp.zeros((bs, nbp - nbs, KV_COLS), f32)], axis=1)

    pos_q = past + jnp.arange(tn, dtype=jnp.int32)
    n_cols = NSA_HEADS * tn
    col_t = jnp.arange(LANES) % tn
    col_h = jnp.minimum(jnp.arange(LANES) // tn, NSA_HEADS - 1)
    col_slope = slopes.reshape(-1)[col_h]
    col_pos = past + col_t
    s_q = s_q_hm.reshape(NSA_HEADS, bs, tn, HEAD_DIM).transpose(1, 0, 2, 3)
    qbd = _block_diag_q(s_q, NSA_KV_HEADS)

    def per_b(shape):
        return pl.BlockSpec((1,) + shape, lambda i: (i,) + (0,) * len(shape))

    def const1(shape):
        return pl.BlockSpec(shape, lambda i: (0,) * len(shape))

    s_ocmp, s_sel = pl.pallas_call(
        functools.partial(_s_cmp_kernel, tn=tn),
        grid=(bs,),
        in_specs=[per_b((nbp, KV_COLS)), per_b((LANES, LANES)), const1((1, LANES)), const1((1, LANES))],
        out_specs=(per_b((LANES, LANES)), per_b((nbp, LANES))),
        out_shape=(jax.ShapeDtypeStruct((bs, LANES, LANES), f32), jax.ShapeDtypeStruct((bs, nbp, LANES), f32)),
        compiler_params=_params(1),
        name="s_cmp_select",
    )(s_kcvc, qbd, col_slope.reshape(1, LANES).astype(f32), col_pos.reshape(1, LANES).astype(jnp.int32))

    def pages_t(cache, width):
        return cache[0].transpose(0, 2, 3, 4, 1).reshape(-1, width, cache.shape[2])

    def new_t(x, width):
        return jnp.pad(x.reshape(bs, tn, width).transpose(0, 2, 1), ((0, 0), (0, 0), (0, K_TILE - tn)))

    col_g0 = (jnp.arange(LANES) // (NSA_GROUP * tn)) * (NSA_GROUP * tn) + col_t
    sel_cols = s_sel[:, :, col_g0]
    n_hot = nbp // 2 + 8
    k_aug = 2 * LANES
    selb_rows = jnp.where(sel_cols[:, :n_hot] > 0.5, 0.0, MASK_BIAS)
    al_rows = jnp.stack([col_slope * CMP_BLOCK, col_slope,
                         -col_slope * CMP_BLOCK * (col_pos // CMP_BLOCK).astype(f32),
                         -col_slope * (col_pos % CMP_BLOCK).astype(f32)], axis=0)
    saug_t = jnp.concatenate([selb_rows, jnp.broadcast_to(al_rows, (bs, 4, LANES)),
                              jnp.zeros((bs, k_aug - n_hot - 4, LANES), f32)], axis=1).astype(bf16).transpose(0, 2, 1)
    pos_p = jnp.arange(past, dtype=jnp.int32)
    eaug_t = jnp.concatenate([(pos_p[:, None] // CMP_BLOCK == jnp.arange(n_hot)[None, :]).astype(bf16),
                              _alibi_key_cols(pos_p, k_aug - n_hot)], axis=1).T
    win_rows = state_nsa_win.shape[2]
    pos_w = past - win_rows + jnp.arange(win_rows, dtype=jnp.int32)
    ewin_t = _alibi_key_cols(pos_w, LANES).T
    waug_t = jnp.concatenate([al_rows, jnp.zeros((LANES - 4, LANES), f32)], axis=0).astype(bf16).T
    dist_w = col_pos[:, None] - pos_w[None, :]
    wmask_t = jnp.where((dist_w >= 0) & (dist_w <= WINDOW) & (pos_w[None, :] >= 0), 0.0, MASK_BIAS).astype(f32)
    j_new = jnp.arange(K_TILE)[None, :]
    ok_new = (j_new <= col_t[:, None]) & (j_new < tn)
    nbias_t = jnp.where(ok_new, -col_slope[:, None] * (col_t[:, None] - j_new).astype(f32), MASK_BIAS).astype(f32)
    sel_pages = pages_t(cache_nsa_sel, KV_COLS)
    win_t = state_nsa_win[0].transpose(0, 2, 3, 4, 1).reshape(bs, KV_COLS, win_rows)
    win_buf = state_nsa_win[0].reshape(bs, win_rows, KV_COLS)
    pps = S_SEL_PAGES if n_pages % S_SEL_PAGES == 0 else 1

    def b2(shape):
        return pl.BlockSpec((1,) + shape, lambda i, j, pt: (i,) + (0,) * len(shape))

    def c2(shape):
        return pl.BlockSpec(shape, lambda i, j, pt: (0,) * len(shape))

    out_o = jax.ShapeDtypeStruct((bs, LANES, LANES), f32)
    s_osel, s_owin = pl.pallas_call(
        functools.partial(_s_selwin_kernel, n_pages_step=pps),
        grid_spec=pltpu.PrefetchScalarGridSpec(
            num_scalar_prefetch=1, grid=(bs, n_pages // pps),
            in_specs=[pl.BlockSpec((1, KV_COLS, page), lambda i, j, pt, k=k: (pt[i, j * pps + k], 0, 0))
                      for k in range(pps)]
            + [b2((LANES, LANES)), b2((LANES, k_aug)),
               pl.BlockSpec((k_aug, pps * page), lambda i, j, pt: (0, j)),
               b2((KV_COLS, K_TILE)), c2((LANES, K_TILE)), c2((LANES, LANES)), c2((LANES, win_rows)),
               c2((LANES, win_rows)), b2((KV_COLS, win_rows)), b2((KV_COLS, K_TILE))],
            out_specs=(b2((LANES, LANES)), b2((LANES, LANES))),
            scratch_shapes=[pltpu.VMEM((LANES, 1), f32), pltpu.VMEM((LANES, 1), f32),
                            pltpu.VMEM((LANES, LANES), f32)]),
        out_shape=(out_o, out_o),
        compiler_params=_params(2),
        name="s_sel_win",
    )(page_table, *([sel_pages] * pps), qbd.transpose(0, 2, 1), saug_t, eaug_t, new_t(s_kv_sel, KV_COLS), nbias_t,
      waug_t, ewin_t, wmask_t, win_t, new_t(s_kv_win, KV_COLS))

    def nsa_heads(o_full):
        o = o_full[:, :n_cols, :].reshape(bs, NSA_KV_HEADS, NSA_GROUP, tn, NSA_KV_HEADS, HEAD_DIM)
        o = jnp.stack([o[:, gi, :, :, gi, :] for gi in range(NSA_KV_HEADS)], axis=1)
        return o.transpose(0, 3, 1, 2, 4).reshape(1, n_tok, NSA_WIDTH)

    so_cmp = nsa_heads(s_ocmp)
    so_sel = nsa_heads(s_osel)
    so_win = nsa_heads(s_owin)

    s_qsb = s_qsb_hm.reshape(SB_HEADS, bs, tn, HEAD_DIM).transpose(1, 0, 2, 3)
    qbd_sb = _block_diag_q(s_qsb, SB_HEADS)
    nmask_t = ((j_new < col_t[:, None]) & (j_new < tn)).astype(f32)
    sb_pages = pages_t(cache_sb, 2 * SB_WIDTH)
    ppb = S_SB_PAGES if n_pages % S_SB_PAGES == 0 else 1
    s_osb = pl.pallas_call(
        functools.partial(_s_sb_kernel, n_pages_step=ppb),
        grid_spec=pltpu.PrefetchScalarGridSpec(
            num_scalar_prefetch=1, grid=(bs, n_pages // ppb),
            in_specs=[pl.BlockSpec((1, 2 * SB_WIDTH, page),
                                   lambda i, j, pt, k=k: (pt[i, n_pages - 1 - (j * ppb + k)], 0, 0))
                      for k in range(ppb)]
            + [b2((LANES, SB_WIDTH)), b2((2 * SB_WIDTH, K_TILE)), c2((LANES, K_TILE)),
               c2((2 * K_TILE, 2 * K_TILE))],
            out_specs=b2((LANES, SB_WIDTH)),
            scratch_shapes=[pltpu.VMEM((LANES, K_TILE), f32), pltpu.VMEM((LANES, SB_WIDTH), f32)]),
        out_shape=jax.ShapeDtypeStruct((bs, LANES, SB_WIDTH), f32),
        compiler_params=_params(2),
        name="s_stick_breaking",
    )(page_table, *([sb_pages] * ppb), qbd_sb.transpose(0, 2, 1), new_t(s_kv_sb, 2 * SB_WIDTH), nmask_t,
      _tri_weights(K_TILE))
    o = s_osb[:, :SB_HEADS * tn, :].reshape(bs, SB_HEADS, tn, SB_HEADS, HEAD_DIM)
    so_sb = jnp.stack([o[:, hi, :, hi, :] for hi in range(SB_HEADS)], axis=0)
    so_sb = so_sb.reshape(1, SB_HEADS, n_tok, HEAD_DIM)

    y_sample = _out_ffn_call(x_sample.reshape(1, n_tok, d), so_cmp, so_sel, so_win, s_gates, so_sb, ms[2:], gexp,
                             norm_out_nsa[0], norm_out_sb[0], wout_b, norm_ffn[0], wg_b, wu_b, wd_b, norm_final,
                             256).reshape(bs, tn, d)
    all_win = jnp.concatenate([win_buf, s_kv_win.reshape(bs, tn, KV_COLS)], axis=1)
    keep = min(WINDOW, win_rows + tn)
    new_win_sample = all_win[:, win_rows + tn - keep:]

    g_n = NSA_KV_HEADS
    return (y_prompt, y_sample,
            kv_cmp.reshape(1, bp, t, 2, g_n, HEAD_DIM), kv_sel.reshape(1, bp, t, 2, g_n, HEAD_DIM),
            kv_sb.reshape(1, bp, t, 2, SB_HEADS, HEAD_DIM), new_win_prompt.reshape(1, bp, win_len, 2, g_n, HEAD_DIM),
            s_kv_cmp.reshape(1, bs, tn, 2, g_n, HEAD_DIM), s_kv_sel.reshape(1, bs, tn, 2, g_n, HEAD_DIM),
            s_kv_sb.reshape(1, bs, tn, 2, SB_HEADS, HEAD_DIM), new_win_sample.reshape(1, bs, keep, 2, g_n, HEAD_DIM))
```

```python
import functools

import jax
import jax.numpy as jnp
from jax import lax
from jax.experimental import pallas as pl
from jax.experimental.pallas import tpu as pltpu

HEAD_DIM = 64
NSA_HEADS = 8
NSA_KV_HEADS = 2
NSA_GROUP = NSA_HEADS // NSA_KV_HEADS
SB_HEADS = 8
NSA_WIDTH = NSA_HEADS * HEAD_DIM
SB_WIDTH = SB_HEADS * HEAD_DIM
KV_COLS = 2 * NSA_KV_HEADS * HEAD_DIM
CMP_BLOCK = 64
CMP_HIDDEN = 128
N_SELECT = 16
WINDOW = 512
EPS = 1e-6
NEG = -1e30
MASK_BIAS = -1e9
LANES = 128
Q_TILE = 128
K_TILE = 128
QK_SCALE = HEAD_DIM ** -0.5
VMEM_LIMIT = 56 * 1024 * 1024

f32 = jnp.float32
bf16 = jnp.bfloat16


def _dot(a, b):
    return jnp.dot(a, b, preferred_element_type=f32)


def _dot_nt(a, b):
    return lax.dot_general(a, b, (((1,), (1,)), ((), ())), preferred_element_type=f32)


def _dot_tn(a, b):
    return lax.dot_general(a, b, (((0,), (0,)), ((), ())), preferred_element_type=f32)


def _split(x):
    hi = x.astype(bf16)
    lo = (x - hi.astype(f32)).astype(bf16)
    return hi, lo


def _rms(x, g):
    return x * lax.rsqrt(jnp.mean(x * x, axis=-1, keepdims=True) + EPS) * g


def _params(n_axes, vmem=VMEM_LIMIT):
    return pltpu.CompilerParams(dimension_semantics=("arbitrary",) * n_axes, vmem_limit_bytes=vmem)


def _mods_kernel(c_ref, w_ref, b_ref, o_ref):
    c = c_ref[...]
    a = c * (1.0 / (1.0 + jnp.exp(-c)))
    a_hi, a_lo = _split(a)
    w_hi, w_lo = _split(w_ref[...])
    o_ref[...] = _dot(a_hi, w_hi) + _dot(a_lo, w_hi) + _dot(a_hi, w_lo) + b_ref[...]


def _mods_call(c_all, w_ada, b_ada):
    m, d = c_all.shape
    n = w_ada.shape[1]
    tn = 1536 if n % 1536 == 0 else n
    return pl.pallas_call(
        _mods_kernel,
        grid=(n // tn,),
        in_specs=[pl.BlockSpec((m, d), lambda j: (0, 0)),
                  pl.BlockSpec((d, tn), lambda j: (0, j)),
                  pl.BlockSpec((1, tn), lambda j: (0, j))],
        out_specs=pl.BlockSpec((m, tn), lambda j: (0, j)),
        out_shape=jax.ShapeDtypeStruct((m, n), f32),
        compiler_params=_params(1),
        name="mods",
    )(c_all, w_ada, b_ada.reshape(1, n))


_C_Q, _C_KC, _C_KS, _C_KW, _C_QSB, _C_KVSB, _C_G, _C_END = 0, 512, 768, 1024, 1280, 1792, 2816, 2944


def _proj_kernel(x_ref, sh_ref, sc_ref, g_ref, w_ref,
                 kvc_ref, kvs_ref, kvw_ref, kvsb_ref, gate_ref,
                 qn_ref, qsb_ref, ksel_ref, kwin_ref, ksb_ref):
    x = x_ref[0]
    h = _rms(x, g_ref[...]) * (1.0 + sc_ref[0]) + sh_ref[0]
    r = _dot(h.astype(bf16), w_ref[...])
    kvc_ref[0] = r[:, _C_KC:_C_KS]
    kvs_ref[0] = r[:, _C_KS:_C_KW]
    kvw_ref[0] = r[:, _C_KW:_C_QSB]
    kvsb_ref[0] = r[:, _C_KVSB:_C_G]
    gate_ref[0] = 1.0 / (1.0 + jnp.exp(-r[:, _C_G:_C_END]))
    for h_i in range(NSA_HEADS):
        lo = _C_Q + h_i * HEAD_DIM
        qn_ref[0, h_i] = (r[:, lo:lo + HEAD_DIM] * QK_SCALE).astype(bf16)
    for h_i in range(SB_HEADS):
        lo = _C_QSB + h_i * HEAD_DIM
        qsb_ref[0, h_i] = (r[:, lo:lo + HEAD_DIM] * QK_SCALE).astype(bf16)
    for j in range(2 * NSA_KV_HEADS):
        lo = _C_KS + j * HEAD_DIM
        ksel_ref[0, j] = r[:, lo:lo + HEAD_DIM].astype(bf16)
        lo = _C_KW + j * HEAD_DIM
        kwin_ref[0, j] = r[:, lo:lo + HEAD_DIM].astype(bf16)
    for j in range(2 * SB_HEADS):
        lo = _C_KVSB + j * HEAD_DIM
        ksb_ref[0, j] = r[:, lo:lo + HEAD_DIM].astype(bf16)


def _proj_call(x, shift, scale, g, w_cat, tm):
    b, t, d = x.shape
    rows = shift.shape[1]
    mod_spec = (pl.BlockSpec((1, 1, d), lambda i, j: (i, 0, 0)) if rows == 1
                else pl.BlockSpec((1, tm, d), lambda i, j: (i, j, 0)))

    def flat(c):
        return pl.BlockSpec((1, tm, c), lambda i, j: (i, j, 0))

    def heads(n):
        return pl.BlockSpec((1, n, tm, HEAD_DIM), lambda i, j: (i, 0, j, 0))

    out_shape = (
        jax.ShapeDtypeStruct((b, t, KV_COLS), f32),
        jax.ShapeDtypeStruct((b, t, KV_COLS), f32),
        jax.ShapeDtypeStruct((b, t, KV_COLS), f32),
        jax.ShapeDtypeStruct((b, t, 2 * SB_WIDTH), f32),
        jax.ShapeDtypeStruct((b, t, LANES), f32),
        jax.ShapeDtypeStruct((b, NSA_HEADS, t, HEAD_DIM), bf16),
        jax.ShapeDtypeStruct((b, SB_HEADS, t, HEAD_DIM), bf16),
        jax.ShapeDtypeStruct((b, 2 * NSA_KV_HEADS, t, HEAD_DIM), bf16),
        jax.ShapeDtypeStruct((b, 2 * NSA_KV_HEADS, t, HEAD_DIM), bf16),
        jax.ShapeDtypeStruct((b, 2 * SB_HEADS, t, HEAD_DIM), bf16),
    )
    out_specs = (flat(KV_COLS), flat(KV_COLS), flat(KV_COLS), flat(2 * SB_WIDTH), flat(LANES),
                 heads(NSA_HEADS), heads(SB_HEADS), heads(2 * NSA_KV_HEADS), heads(2 * NSA_KV_HEADS),
                 heads(2 * SB_HEADS))
    return pl.pallas_call(
        _proj_kernel,
        grid=(b, t // tm),
        in_specs=[flat(d), mod_spec, mod_spec,
                  pl.BlockSpec((1, d), lambda i, j: (0, 0)),
                  pl.BlockSpec(w_cat.shape, lambda i, j: (0, 0))],
        out_specs=out_specs,
        out_shape=out_shape,
        compiler_params=_params(2),
        name="proj",
    )(x, shift, scale, g.reshape(1, d), w_cat)


def _gelu_tanh(x):
    return 0.5 * x * (1.0 + jnp.tanh(0.7978845608028654 * (x + 0.044715 * (x * x * x))))


def _compress_kernel(x_ref, pe_ref, w1_ref, w2_ref, o_ref, acc_ref):
    k = pl.program_id(1)

    @pl.when(k == 0)
    def _():
        acc_ref[...] = jnp.zeros_like(acc_ref)

    acc_ref[...] += _dot((x_ref[...] + pe_ref[...]).astype(bf16), w1_ref[...])

    @pl.when(k == pl.num_programs(1) - 1)
    def _():
        o_ref[...] = _dot(_gelu_tanh(acc_ref[...]).astype(bf16), w2_ref[...])


def _compress_call(x_rows, pe_row, w1_big, w2_big):
    m, kdim = x_rows.shape
    tm = 256 if m % 256 == 0 else m
    tk = 4096
    n1 = w1_big.shape[1]
    n2 = w2_big.shape[1]
    return pl.pallas_call(
        _compress_kernel,
        grid=(m // tm, kdim // tk),
        in_specs=[pl.BlockSpec((tm, tk), lambda i, k: (i, k)),
                  pl.BlockSpec((1, tk), lambda i, k: (0, k)),
                  pl.BlockSpec((tk, n1), lambda i, k: (k, 0)),
                  pl.BlockSpec((n1, n2), lambda i, k: (0, 0))],
        out_specs=pl.BlockSpec((tm, n2), lambda i, k: (i, 0)),
        out_shape=jax.ShapeDtypeStruct((m, n2), f32),
        scratch_shapes=[pltpu.VMEM((tm, n1), f32)],
        compiler_params=_params(2),
        name="compress",
    )(x_rows, pe_row, w1_big, w2_big)


def _compress_weights(pe_k, w1_k, w2_k, pe_v, w1_v, w2_v):
    g = NSA_KV_HEADS
    w1 = jnp.zeros((CMP_BLOCK, 2, g, HEAD_DIM, 2, g, CMP_HIDDEN), f32)
    w2 = jnp.zeros((2, g, CMP_HIDDEN, 2, g, HEAD_DIM), f32)
    for kv, (a1, a2) in enumerate(((w1_k, w2_k), (w1_v, w2_v))):
        a1 = a1.reshape(CMP_BLOCK, HEAD_DIM, CMP_HIDDEN)
        for gi in range(g):
            w1 = w1.at[:, kv, gi, :, kv, gi, :].set(a1)
            w2 = w2.at[kv, gi, :, kv, gi, :].set(a2)
    w1 = w1.reshape(CMP_BLOCK * 2 * g * HEAD_DIM, 2 * g * CMP_HIDDEN).astype(bf16)
    w2 = w2.reshape(2 * g * CMP_HIDDEN, 2 * g * HEAD_DIM).astype(bf16)
    pe = jnp.stack([jnp.broadcast_to(pe_k[:, None, :], (CMP_BLOCK, g, HEAD_DIM)),
                    jnp.broadcast_to(pe_v[:, None, :], (CMP_BLOCK, g, HEAD_DIM))], axis=1)
    return pe.reshape(1, -1), w1, w2


def _compress_pages_kernel(pt_ref, *refs, n_pages):
    pages = refs[:n_pages]
    pe_ref, w1_ref, w2_ref, o_ref, x_ref = refs[n_pages:]
    for k in range(n_pages):
        x_ref[k] = pages[k][0]
    g_n = NSA_KV_HEADS
    n_blk = w1_ref.shape[3] // CMP_HIDDEN
    for kv in range(2):
        def rows_of(d, kv=kv):
            rows = jnp.concatenate([x_ref[:, (kv * g_n + gi) * HEAD_DIM + d, :] for gi in range(g_n)], axis=0)
            return (rows + pe_ref[kv, pl.ds(d, 1), :]).astype(bf16)

        def body(i, h, kv=kv):
            lhs = jnp.concatenate([rows_of(2 * i), rows_of(2 * i + 1)], axis=1)
            return h + _dot(lhs, w1_ref[kv, i])

        h = lax.fori_loop(0, HEAD_DIM // 2, body, jnp.zeros((g_n * n_pages, n_blk * CMP_HIDDEN), f32), unroll=4)
        act = _gelu_tanh(h).astype(bf16)
        for blk in range(n_blk):
            res = _dot(act[:, blk * CMP_HIDDEN:(blk + 1) * CMP_HIDDEN], w2_ref[kv])
            for gi in range(g_n):
                lo = (kv * g_n + gi) * HEAD_DIM
                o_ref[0, blk, :, lo:lo + HEAD_DIM] = res[gi * n_pages:(gi + 1) * n_pages]


def _compress_pages_call(page_table, pages_t, pe_k, w1_k, w2_k, pe_v, w1_v, w2_v):
    b, n_pages = page_table.shape
    page = pages_t.shape[2]
    n_blk = page // CMP_BLOCK
    eye = jnp.eye(n_blk, dtype=f32)

    def w1_layout(w1):
        w = w1.reshape(CMP_BLOCK, HEAD_DIM, CMP_HIDDEN).transpose(1, 0, 2)
        w = jnp.einsum('djc,ab->dajbc', w, eye).reshape(HEAD_DIM, page, n_blk * CMP_HIDDEN)
        return w.reshape(HEAD_DIM // 2, 2 * page, n_blk * CMP_HIDDEN)

    w1 = jnp.stack([w1_layout(w1_k), w1_layout(w1_v)]).astype(bf16)
    w2 = jnp.stack([w2_k, w2_v]).astype(bf16)
    pe = jnp.stack([jnp.tile(pe_k.T, (1, n_blk)), jnp.tile(pe_v.T, (1, n_blk))])
    out = pl.pallas_call(
        functools.partial(_compress_pages_kernel, n_pages=n_pages),
        grid_spec=pltpu.PrefetchScalarGridSpec(
            num_scalar_prefetch=1, grid=(b,),
            in_specs=[pl.BlockSpec((1, KV_COLS, page), lambda i, pt, k=k: (pt[i, k], 0, 0)) for k in range(n_pages)]
            + [pl.BlockSpec(pe.shape, lambda i, pt: (0, 0, 0)),
               pl.BlockSpec(w1.shape, lambda i, pt: (0, 0, 0, 0)),
               pl.BlockSpec(w2.shape, lambda i, pt: (0, 0, 0))],
            out_specs=pl.BlockSpec((1, n_blk, n_pages, KV_COLS), lambda i, pt: (i, 0, 0, 0)),
            scratch_shapes=[pltpu.VMEM((n_pages, KV_COLS, page), f32)]),
        out_shape=jax.ShapeDtypeStruct((b, n_blk, n_pages, KV_COLS), f32),
        compiler_params=_params(1),
        name="compress_pages",
    )(page_table, *([pages_t] * n_pages), pe, w1, w2)
    return out.transpose(0, 2, 1, 3).reshape(b, n_pages * n_blk, KV_COLS)


def _select_blocks(imp, n_idx, cur, n_rounds):
    forced = (n_idx == 0) | (n_idx == cur) | (n_idx == cur - 1)
    in_range = n_idx <= cur
    v = jnp.where(forced, 8.0, imp)
    v = jnp.where(in_range, v, -1.0)
    sel = jnp.zeros(imp.shape, f32)
    big = jnp.int32(1 << 20)
    for _ in range(n_rounds):
        mx = jnp.max(v, axis=0, keepdims=True)
        idx = jnp.min(jnp.where(v == mx, n_idx, big), axis=0, keepdims=True)
        hit = n_idx == idx
        sel = jnp.where(hit, 1.0, sel)
        v = jnp.where(hit, -2.0, v)
    return jnp.where(in_range, sel, 0.0)


def _cmp_kernel(q_ref, kv_ref, ocmp_ref, selb_ref, any_ref):
    qi = pl.program_id(1)
    tq = q_ref.shape[2]
    nb = kv_ref.shape[1]
    kvb = kv_ref[0]
    n_idx = lax.broadcasted_iota(jnp.int32, (nb, tq), 0)
    pos = qi * tq + lax.broadcasted_iota(jnp.int32, (nb, tq), 1)
    dist_i = pos - (n_idx * CMP_BLOCK + (CMP_BLOCK - 1))
    valid = dist_i >= 0
    dist = dist_i.astype(f32)
    imps, outs = [], []
    for g in range(NSA_KV_HEADS):
        kc = kvb[:, g * HEAD_DIM:(g + 1) * HEAD_DIM].astype(bf16)
        vc = kvb[:, (NSA_KV_HEADS + g) * HEAD_DIM:(NSA_KV_HEADS + g + 1) * HEAD_DIM].astype(bf16)
        imp = jnp.zeros((nb, tq), f32)
        for r in range(NSA_GROUP):
            slope = 2.0 ** -(g * NSA_GROUP + r + 1)
            s = _dot_nt(kc, q_ref[0, g * NSA_GROUP + r]) - slope * dist
            s = jnp.where(valid, s, NEG)
            e = jnp.exp(s - jnp.max(s, axis=0, keepdims=True))
            p = jnp.where(valid, e / jnp.sum(e, axis=0, keepdims=True), 0.0)
            imp = imp + p
            outs.append(_dot_tn(p.astype(bf16), vc))
        imps.append(imp)
    ocmp_ref[0] = jnp.concatenate(outs, axis=1)
    g_n = NSA_KV_HEADS
    n_idx2 = jnp.concatenate([n_idx] * g_n, axis=1)
    cur2 = jnp.concatenate([pos // CMP_BLOCK] * g_n, axis=1)
    sel = _select_blocks(jnp.concatenate(imps, axis=1), n_idx2, cur2, min(N_SELECT, nb))
    for g in range(g_n):
        sel_t = sel[:, g * tq:(g + 1) * tq].T
        selb_ref[0, g] = jnp.where(sel_t > 0.5, 0.0, MASK_BIAS).astype(bf16)
        any_ref[0, g, 0] = jnp.max(sel_t, axis=0, keepdims=True).astype(jnp.int32)


def _cmp_call(q_hm, kcvc):
    b, _, t, _ = q_hm.shape
    nb = kcvc.shape[1]
    tq = Q_TILE
    nq = t // tq
    g_n = NSA_KV_HEADS
    return pl.pallas_call(
        _cmp_kernel,
        grid=(b, nq),
        in_specs=[pl.BlockSpec((1, NSA_HEADS, tq, HEAD_DIM), lambda i, j: (i, 0, j, 0)),
                  pl.BlockSpec((1, nb, KV_COLS), lambda i, j: (i, 0, 0))],
        out_specs=(pl.BlockSpec((1, tq, NSA_WIDTH), lambda i, j: (i, j, 0)),
                   pl.BlockSpec((1, g_n, tq, nb), lambda i, j: (i, 0, j, 0)),
                   pl.BlockSpec((1, g_n, 1, 1, nb), lambda i, j: (i, 0, j, 0, 0))),
        out_shape=(jax.ShapeDtypeStruct((b, t, NSA_WIDTH), f32),
                   jax.ShapeDtypeStruct((b, g_n, t, nb), bf16),
                   jax.ShapeDtypeStruct((b, g_n, nq, 1, nb), jnp.int32)),
        compiler_params=_params(2),
        name="cmp_select",
    )(q_hm, kcvc)


SEL_TK = 256


def _softmax_tile(k_tile, qa, v_tile, m_ref, l_ref, acc_ref, mask):
    s = _dot_nt(k_tile, qa)
    if mask is not None:
        s = jnp.where(mask, s, MASK_BIAS)
    m_old = m_ref[...]
    m_new = jnp.maximum(m_old, jnp.max(s, axis=0, keepdims=True))
    alpha = jnp.exp(m_old - m_new)
    p = jnp.exp(s - m_new)
    l_ref[...] = alpha * l_ref[...] + jnp.sum(p, axis=0, keepdims=True)
    acc_ref[...] = alpha * acc_ref[...] + _dot_tn(v_tile, p.astype(bf16))
    m_ref[...] = m_new


def _selwin_kernel(flag_ref, q_ref, qal_ref, selb_ref, ks_ref, vs_ref, kw_ref, vw_ref, kconst_ref,
                   osel_ref, owin_ref, kaug_s, kaug_w, qaug, m_ref, l_ref, acc_ref):
    b = pl.program_id(0)
    g = pl.program_id(1)
    qi = pl.program_id(2)
    tq = q_ref.shape[2]
    tk = SEL_TK
    rows = NSA_GROUP * tq
    t_all = ks_ref.shape[2]

    @pl.when(qi == 0)
    def _():
        kaug_s[...] = kconst_ref[...]
        kaug_s[:, 0:HEAD_DIM] = ks_ref[0, 0]
        kaug_w[...] = kconst_ref[:, 0:LANES]
        kaug_w[:, 0:HEAD_DIM] = kw_ref[0, 0]

    qaug[:, 0:HEAD_DIM] = q_ref[0].reshape(rows, HEAD_DIM)
    qaug[:, HEAD_DIM:LANES] = qal_ref[0].reshape(rows, HEAD_DIM)
    sb = selb_ref[0, 0]
    for r in range(NSA_GROUP):
        qaug[r * tq:(r + 1) * tq, LANES:] = sb
    qa = qaug[...]
    qa_w = qaug[:, 0:LANES]

    q0 = qi * tq
    jd = q0 // tk
    q_pos = q0 + (lax.broadcasted_iota(jnp.int32, (tk, rows), 1) & (tq - 1))
    k_rel = lax.broadcasted_iota(jnp.int32, (tk, rows), 0)

    def reset():
        m_ref[...] = jnp.full(m_ref.shape, NEG, f32)
        l_ref[...] = jnp.zeros(l_ref.shape, f32)
        acc_ref[...] = jnp.zeros(acc_ref.shape, f32)

    def finish(o_ref):
        o_t = acc_ref[...] / l_ref[...]
        o_ref[0] = jnp.concatenate([o_t[:, r * tq:(r + 1) * tq].T for r in range(NSA_GROUP)], axis=1)

    def tile(kaug, v_ref, qmat, j, mask):
        off = pl.multiple_of(j * tk, tk)
        _softmax_tile(kaug[pl.ds(off, tk), :], qmat, v_ref[0, 0, pl.ds(off, tk), :], m_ref, l_ref, acc_ref, mask)

    reset()
    n_words = max(1, (t_all // tk) // 32)
    base = ((b * NSA_KV_HEADS + g) * (t_all // tq) + qi) * n_words

    def next_hit(j):
        def miss(t):
            return jnp.logical_and(t < jd, ((flag_ref[base + (t >> 5)] >> (t & 31)) & 1) == 0)

        return lax.while_loop(miss, lambda t: t + 1, j)

    def sel_body(j):
        j1 = next_hit(j)
        j2 = next_hit(jnp.minimum(j1 + 1, jd))

        @pl.when(j2 < jd)
        def _():
            off1 = pl.multiple_of(j1 * tk, tk)
            off2 = pl.multiple_of(j2 * tk, tk)
            k2 = jnp.concatenate([kaug_s[pl.ds(off1, tk), :], kaug_s[pl.ds(off2, tk), :]], axis=0)
            v2 = jnp.concatenate([vs_ref[0, 0, pl.ds(off1, tk), :], vs_ref[0, 0, pl.ds(off2, tk), :]], axis=0)
            _softmax_tile(k2, qa, v2, m_ref, l_ref, acc_ref, None)

        @pl.when(jnp.logical_and(j1 < jd, j2 >= jd))
        def _():
            tile(kaug_s, vs_ref, qa, j1, None)

        return j2 + 1

    lax.while_loop(lambda j: j < jd, sel_body, jnp.int32(0))
    tile(kaug_s, vs_ref, qa, jd, jd * tk + k_rel <= q_pos)

    n_wt = WINDOW // tk + 1
    start = jnp.maximum(jd - (n_wt - 1), 0)
    off_w = pl.multiple_of(start * tk, tk)
    s_w = _dot_nt(kaug_w[pl.ds(off_w, n_wt * tk), :], qa_w)
    kp = start * tk + lax.broadcasted_iota(jnp.int32, s_w.shape, 0)
    qp = q0 + (lax.broadcasted_iota(jnp.int32, s_w.shape, 1) & (tq - 1))
    s_w = jnp.where(kp <= qp, jnp.where(kp >= qp - WINDOW, s_w, MASK_BIAS), MASK_BIAS)
    p_w = jnp.exp(s_w - jnp.max(s_w, axis=0, keepdims=True))
    p_w = p_w * (1.0 / jnp.sum(p_w, axis=0, keepdims=True))
    ow_t = _dot_tn(vw_ref[0, 0, pl.ds(off_w, n_wt * tk), :], p_w.astype(bf16))
    owin_ref[0] = jnp.concatenate([ow_t[:, r * tq:(r + 1) * tq].T for r in range(NSA_GROUP)], axis=1)
    finish(osel_ref)


def _selwin_call(flags, q_hm, qal, selb, ksel_hm, kwin_hm, kconst):
    b, _, t, _ = q_hm.shape
    tq = Q_TILE
    nq = t // tq
    ka = kconst.shape[1]
    g_n = NSA_KV_HEADS

    def kv_spec(off):
        return pl.BlockSpec((1, 1, t, HEAD_DIM), lambda i, g, j, fl: (i, off + g, 0, 0))

    out_spec = pl.BlockSpec((1, tq, NSA_GROUP * HEAD_DIM), lambda i, g, j, fl: (i, j, g))
    return pl.pallas_call(
        _selwin_kernel,
        grid_spec=pltpu.PrefetchScalarGridSpec(
            num_scalar_prefetch=1, grid=(b, g_n, nq),
            in_specs=[pl.BlockSpec((1, NSA_GROUP, tq, HEAD_DIM), lambda i, g, j, fl: (i, g, j, 0)),
                      pl.BlockSpec((1, NSA_GROUP, tq, HEAD_DIM), lambda i, g, j, fl: (g, 0, j, 0)),
                      pl.BlockSpec((1, 1, tq, ka - LANES), lambda i, g, j, fl: (i, g, j, 0)),
                      kv_spec(0), kv_spec(g_n), kv_spec(0), kv_spec(g_n),
                      pl.BlockSpec((t, ka), lambda i, g, j, fl: (0, 0))],
            out_specs=(out_spec, out_spec),
            scratch_shapes=[pltpu.VMEM((t, ka), bf16), pltpu.VMEM((t, LANES), bf16),
                            pltpu.VMEM((NSA_GROUP * tq, ka), bf16),
                            pltpu.VMEM((1, NSA_GROUP * tq), f32), pltpu.VMEM((1, NSA_GROUP * tq), f32),
                            pltpu.VMEM((HEAD_DIM, NSA_GROUP * tq), f32)]),
        out_shape=(jax.ShapeDtypeStruct((b, t, NSA_WIDTH), f32), jax.ShapeDtypeStruct((b, t, NSA_WIDTH), f32)),
        compiler_params=_params(3),
        name="sel_win",
    )(flags, q_hm, qal, selb, ksel_hm, ksel_hm, kwin_hm, kwin_hm, kconst)


def _alibi_key_cols(pos, width):
    cols = jnp.stack([pos // CMP_BLOCK, pos % CMP_BLOCK, jnp.ones_like(pos), jnp.ones_like(pos)], axis=1)
    return jnp.pad(cols.astype(f32), ((0, 0), (0, width - 4))).astype(bf16)


def _alibi_query_cols(slopes, pos, width):
    s = slopes[..., None]
    a = (pos // CMP_BLOCK).astype(f32)
    r = (pos % CMP_BLOCK).astype(f32)
    cols = jnp.stack([jnp.broadcast_to(s * CMP_BLOCK, s.shape[:-1] + pos.shape),
                      jnp.broadcast_to(s, s.shape[:-1] + pos.shape),
                      -s * CMP_BLOCK * a, -s * r], axis=-1)
    pad = [(0, 0)] * (cols.ndim - 1) + [(0, width - 4)]
    return jnp.pad(cols, pad).astype(bf16)


def _slopes():
    h = jnp.arange(1, NSA_HEADS + 1, dtype=f32)
    return jnp.exp2(-8.0 * h / NSA_HEADS).reshape(NSA_KV_HEADS, NSA_GROUP)


def _neg_softplus(z):
    return -(jnp.maximum(z, 0.0) + jnp.log(1.0 + jnp.exp(-jnp.abs(z))))


SB_TILE = 512
SB_DEAD_LOG = -105.0


def _sb_kernel(q_ref, k_ref, v_ref, tri_ref, o_ref, carry_ref, acc_ref):
    qi = pl.program_id(2)
    tq = q_ref.shape[2]
    tk = tq
    sub = tri_ref.shape[1] // 2
    q = q_ref[0, 0]
    tri = tri_ref[...]

    def tile(j, mask):
        off = pl.multiple_of(j * tk, tk)
        z = _dot_nt(q, k_ref[0, 0, pl.ds(off, tk), :])
        c = _neg_softplus(z)
        if mask is not None:
            c = jnp.where(mask, c, 0.0)
        c_hi, c_lo = _split(c)
        run = carry_ref[...]
        parts = [None] * (tk // sub)
        for i in reversed(range(tk // sub)):
            sl = slice(i * sub, (i + 1) * sub)
            res = _dot(jnp.concatenate([c_hi[:, sl], c_lo[:, sl]], axis=1), tri)
            parts[i] = z[:, sl] + res[:, :sub] + run
            run = run + res[:, sub:]
        a = jnp.exp(jnp.concatenate(parts, axis=1))
        if mask is not None:
            a = jnp.where(mask, a, 0.0)
        acc_ref[...] += _dot(a.astype(bf16), v_ref[0, 0, pl.ds(off, tk), :])
        carry_ref[...] = run

    carry_ref[...] = jnp.zeros(carry_ref.shape, f32)
    acc_ref[...] = jnp.zeros(acc_ref.shape, f32)
    t_idx = lax.broadcasted_iota(jnp.int32, (tq, tk), 0)
    k_idx = lax.broadcasted_iota(jnp.int32, (tq, tk), 1)
    tile(qi, k_idx < t_idx)

    def body(state):
        tile(qi - 1 - state[0], None)
        return state[0] + 1, jnp.max(carry_ref[...])

    lax.while_loop(lambda st: jnp.logical_and(st[0] < qi, st[1] > SB_DEAD_LOG), body,
                   (jnp.int32(0), jnp.max(carry_ref[...])))
    o_ref[0, 0] = acc_ref[...]


def _sb_call(q_hm, kv_hm, tri):
    b, h, t, _ = q_hm.shape
    tq = SB_TILE
    return pl.pallas_call(
        _sb_kernel,
        grid=(b, h, t // tq),
        in_specs=[pl.BlockSpec((1, 1, tq, HEAD_DIM), lambda i, hh, j: (i, hh, j, 0)),
                  pl.BlockSpec((1, 1, t, HEAD_DIM), lambda i, hh, j: (i, hh, 0, 0)),
                  pl.BlockSpec((1, 1, t, HEAD_DIM), lambda i, hh, j: (i, h + hh, 0, 0)),
                  pl.BlockSpec(tri.shape, lambda i, hh, j: (0, 0))],
        out_specs=pl.BlockSpec((1, 1, tq, HEAD_DIM), lambda i, hh, j: (i, hh, j, 0)),
        out_shape=jax.ShapeDtypeStruct((b, h, t, HEAD_DIM), f32),
        scratch_shapes=[pltpu.VMEM((tq, tri.shape[1] // 2), f32), pltpu.VMEM((tq, HEAD_DIM), f32)],
        compiler_params=_params(3),
        name="stick_breaking",
    )(q_hm, kv_hm, kv_hm, tri)


def _tri_weights(n):
    j = jnp.arange(n)[:, None]
    s = jnp.arange(n)[None, :]
    w = jnp.concatenate([(j >= s).astype(bf16), jnp.ones((n, n), bf16)], axis=1)
    return jnp.concatenate([w, w], axis=0)


def _out_ffn_kernel(x_ref, ocmp_ref, osel_ref, owin_ref, gate_ref, osb_ref,
                    gmsa_ref, shmlp_ref, scmlp_ref, gmlp_ref,
                    gexp_ref, nnsa_ref, nsb_ref, wout_ref, nffn_ref, wg_ref, wu_ref, wd_ref, nfin_ref,
                    y_ref, x1_ref, h2_ref, acc_ref):
    f = pl.program_id(2)

    @pl.when(f == 0)
    def _():
        g_hi, g_lo = _split(gate_ref[0])
        gx = _dot(g_hi, gexp_ref[...]) + _dot(g_lo, gexp_ref[...])
        w = NSA_WIDTH
        o_nsa = gx[:, 0:w] * ocmp_ref[0] + gx[:, w:2 * w] * osel_ref[0] + gx[:, 2 * w:3 * w] * owin_ref[0]
        o_sb = jnp.concatenate([osb_ref[0, h_i] for h_i in range(SB_HEADS)], axis=1)
        cat = jnp.concatenate([_rms(o_nsa, nnsa_ref[...]), _rms(o_sb, nsb_ref[...])], axis=1)
        x1 = x_ref[0] + gmsa_ref[0] * _dot(cat.astype(bf16), wout_ref[...])
        x1_ref[...] = x1
        h2_ref[...] = (_rms(x1, nffn_ref[...]) * (1.0 + scmlp_ref[0]) + shmlp_ref[0]).astype(bf16)
        acc_ref[...] = jnp.zeros_like(acc_ref)

    h2 = h2_ref[...]
    gt = _dot(h2, wg_ref[...])
    up = _dot(h2, wu_ref[...])
    act = gt * (1.0 / (1.0 + jnp.exp(-gt))) * up
    acc_ref[...] += _dot(act.astype(bf16), wd_ref[...])

    @pl.when(f == pl.num_programs(2) - 1)
    def _():
        y = x1_ref[...] + gmlp_ref[0] * acc_ref[...]
        y_ref[0] = _rms(y, nfin_ref[...])


def _out_ffn_call(x, ocmp, osel, owin, gates, osb_hm, mods4, gexp, nnsa, nsb, wout, nffn, wg, wu, wd, nfin, tm):
    b, t, d = x.shape
    tm = min(tm, t)
    dff = wg.shape[1]
    tf = dff // 2 if (dff // 2) % LANES == 0 else dff
    rows = mods4[0].shape[1]
    mod_spec = (pl.BlockSpec((1, 1, d), lambda i, j, f: (i, 0, 0)) if rows == 1
                else pl.BlockSpec((1, tm, d), lambda i, j, f: (i, j, 0)))

    def flat(c):
        return pl.BlockSpec((1, tm, c), lambda i, j, f: (i, j, 0))

    def const(shape):
        return pl.BlockSpec(shape, lambda i, j, f: (0,) * len(shape))

    return pl.pallas_call(
        _out_ffn_kernel,
        grid=(b, t // tm, dff // tf),
        in_specs=[flat(d), flat(NSA_WIDTH), flat(NSA_WIDTH), flat(NSA_WIDTH), flat(LANES),
                  pl.BlockSpec((1, SB_HEADS, tm, HEAD_DIM), lambda i, j, f: (i, 0, j, 0)),
                  mod_spec, mod_spec, mod_spec, mod_spec,
                  const(gexp.shape), const((1, NSA_WIDTH)), const((1, SB_WIDTH)), const(wout.shape),
                  const((1, d)),
                  pl.BlockSpec((d, tf), lambda i, j, f: (0, f)),
                  pl.BlockSpec((d, tf), lambda i, j, f: (0, f)),
                  pl.BlockSpec((tf, d), lambda i, j, f: (f, 0)),
                  const((1, d))],
        out_specs=flat(d),
        out_shape=jax.ShapeDtypeStruct((b, t, d), f32),
        scratch_shapes=[pltpu.VMEM((tm, d), f32), pltpu.VMEM((tm, d), bf16), pltpu.VMEM((tm, d), f32)],
        compiler_params=_params(3),
        name="out_ffn",
    )(x, ocmp, osel, owin, gates, osb_hm, *mods4, gexp, nnsa.reshape(1, -1), nsb.reshape(1, -1), wout,
      nffn.reshape(1, -1), wg, wu, wd, nfin.reshape(1, -1))


def _gate_expand():
    rows = jnp.arange(LANES)[:, None]
    cols = jnp.arange(3 * NSA_WIDTH)[None, :]
    return (rows == (cols // NSA_WIDTH) * NSA_HEADS + (cols % NSA_WIDTH) // HEAD_DIM).astype(bf16)


def _s_cmp_kernel(kv_ref, qbd_ref, qal_ref, pq_ref, ocmp_ref, sel_ref, *, tn):
    nbp = kv_ref.shape[1]
    kvb = kv_ref[0]
    kc = kvb[:, 0:LANES].astype(bf16)
    vc = kvb[:, LANES:2 * LANES].astype(bf16)
    n_idx = lax.broadcasted_iota(jnp.int32, (nbp, LANES), 0)
    pos = jnp.broadcast_to(pq_ref[...], (nbp, LANES))
    dist_i = pos - (n_idx * CMP_BLOCK + (CMP_BLOCK - 1))
    valid = dist_i >= 0
    slope = qal_ref[...]
    s = _dot(kc, qbd_ref[0]) - slope * dist_i.astype(f32)
    s = jnp.where(valid, s, NEG)
    e = jnp.exp(s - jnp.max(s, axis=0, keepdims=True))
    p = jnp.where(valid, e / jnp.sum(e, axis=0, keepdims=True), 0.0)
    ocmp_ref[0] = _dot_tn(p.astype(bf16), vc)
    imp = p
    for r in range(1, NSA_GROUP):
        imp = imp + pltpu.roll(p, LANES - r * tn, axis=1)
    cur = pos // CMP_BLOCK
    sel_ref[0] = _select_blocks(imp, n_idx, cur, min(N_SELECT, nbp))


S_SEL_PAGES = 16


def _row_softmax_pv(z_list, vt_list):
    m = z_list[0].max(axis=1, keepdims=True)
    for z in z_list[1:]:
        m = jnp.maximum(m, z.max(axis=1, keepdims=True))
    e_list = [jnp.exp(z - m) for z in z_list]
    l = e_list[0].sum(axis=1, keepdims=True)
    for e in e_list[1:]:
        l = l + e.sum(axis=1, keepdims=True)
    inv = 1.0 / l
    o = None
    for e, vt in zip(e_list, vt_list):
        pv = _dot_nt((e * inv).astype(bf16), vt)
        o = pv if o is None else o + pv
    return o


def _s_selwin_kernel(pt_ref, *refs, n_pages_step):
    pages = refs[:n_pages_step]
    (qt_ref, saug_ref, eaug_ref, news_ref, nbias_ref, waug_ref, ewin_ref, wmask_ref, kvw_ref, neww_ref,
     osel_ref, owin_ref, m_ref, l_ref, acc_ref) = refs[n_pages_step:]
    j = pl.program_id(1)
    qt = qt_ref[0]
    half = LANES

    def online(z, vt):
        m_old = m_ref[...]
        m_new = jnp.maximum(m_old, z.max(axis=1, keepdims=True))
        alpha = jnp.exp(m_old - m_new)
        p = jnp.exp(z - m_new)
        l_ref[...] = alpha * l_ref[...] + p.sum(axis=1, keepdims=True)
        acc_ref[...] = alpha * acc_ref[...] + _dot_nt(p.astype(bf16), vt)
        m_ref[...] = m_new

    @pl.when(j == 0)
    def _():
        kvw = kvw_ref[0]
        z_w = _dot(qt, kvw[0:half].astype(bf16)) + _dot(waug_ref[...], ewin_ref[...]) + wmask_ref[...]
        new_w = neww_ref[0]
        z_n = _dot(qt, new_w[0:half].astype(bf16)) + nbias_ref[...]
        owin_ref[0] = _row_softmax_pv([z_w, z_n], [kvw[half:].astype(bf16), new_w[half:].astype(bf16)])
        m_ref[...] = jnp.full(m_ref.shape, NEG, f32)
        l_ref[...] = jnp.zeros(l_ref.shape, f32)
        acc_ref[...] = jnp.zeros(acc_ref.shape, f32)
        new_s = news_ref[0]
        online(_dot(qt, new_s[0:half].astype(bf16)) + nbias_ref[...], new_s[half:].astype(bf16))

    kt = jnp.concatenate([p_ref[0, 0:half, :].astype(bf16) for p_ref in pages], axis=1)
    vt = jnp.concatenate([p_ref[0, half:, :].astype(bf16) for p_ref in pages], axis=1)
    online(_dot(qt, kt) + _dot(saug_ref[0], eaug_ref[...]), vt)

    @pl.when(j == pl.num_programs(1) - 1)
    def _():
        osel_ref[0] = acc_ref[...] / l_ref[...]


S_SB_HEAD_PAGES = 8
S_SB_TAIL_PAGES = 14


def _sb_pieces(qt, tri, tiles, mask, carry_ref, acc_ref):
    w = SB_WIDTH
    n_tok = tri.shape[1] // 2
    zs, ress = [], []
    for t_f in tiles:
        z = _dot(qt, t_f[0:w].astype(bf16))
        c = _neg_softplus(z)
        if mask is not None:
            c = c * mask
        c_hi, c_lo = _split(c)
        zs.append(z)
        ress.append(_dot(jnp.concatenate([c_hi, c_lo], axis=1), tri))
    run = carry_ref[...]
    acc = acc_ref[...]
    for t_f, z, res in zip(tiles, zs, ress):
        a = jnp.exp(z + res[:, :n_tok] + run)
        if mask is not None:
            a = a * mask
        acc = acc + _dot_nt(a.astype(bf16), t_f[w:].astype(bf16))
        run = run + res[:, n_tok:]
    carry_ref[...] = run
    acc_ref[...] = acc


def _s_sb_head_kernel(pt_ref, *refs, n_pages_step):
    pages = refs[:n_pages_step]
    qt_ref, new_ref, nmask_ref, tri_ref, acc_out, carry_out, alive_out, carry_ref, acc_ref = refs[n_pages_step:]
    carry_ref[...] = jnp.zeros_like(carry_ref)
    acc_ref[...] = jnp.zeros_like(acc_ref)
    qt = qt_ref[0]
    tri = tri_ref[...]
    _sb_pieces(qt, tri, [new_ref[0]], nmask_ref[...], carry_ref, acc_ref)
    _sb_pieces(qt, tri, [p_ref[0] for p_ref in pages], None, carry_ref, acc_ref)
    acc_out[0] = acc_ref[...]
    carry = carry_ref[...]
    carry_out[0] = carry
    top = jnp.max(jnp.max(carry, axis=1, keepdims=True), axis=0, keepdims=True)
    alive_out[0] = jnp.broadcast_to(jnp.where(top > SB_DEAD_LOG, 1, 0).astype(jnp.int32), alive_out.shape[1:])


def _s_sb_tail_kernel(pt_ref, alive_ref, *refs, n_pages_step):
    pages = refs[:n_pages_step]
    qt_ref, acc_in, carry_in, tri_ref, o_ref, carry_ref, acc_ref = refs[n_pages_step:]
    i = pl.program_id(0)
    j = pl.program_id(1)

    @pl.when(j == 0)
    def _():
        carry_ref[...] = carry_in[0]
        acc_ref[...] = acc_in[0]

    @pl.when(jnp.logical_and(alive_ref[i] > 0, jnp.max(carry_ref[...]) > SB_DEAD_LOG))
    def _():
        _sb_pieces(qt_ref[0], tri_ref[...], [p_ref[0] for p_ref in pages], None, carry_ref, acc_ref)

    @pl.when(j == pl.num_programs(1) - 1)
    def _():
        o_ref[0] = acc_ref[...]


def _block_diag_q(q, n_groups):
    bs, h, tn, hd = q.shape
    per = h // n_groups
    eye = jnp.eye(n_groups, dtype=q.dtype)
    qg = q.reshape(bs, n_groups, per, tn, hd)
    out = jnp.einsum('bgrtd,gk->bgdkrt', qg, eye).reshape(bs, n_groups * hd, h * tn)
    return jnp.pad(out, ((0, 0), (0, 0), (0, LANES - h * tn)))


def kernel(x_prompt, x_sample, cache_nsa_cmp, cache_nsa_sel, cache_sb, state_nsa_win, page_table,
           c_prompt, c_sample, w_ada, b_ada, norm_attn, norm_ffn, w_in, cmp_pe_k, cmp_k_w1, cmp_k_w2,
           cmp_pe_v, cmp_v_w1, cmp_v_w2, norm_out_nsa, norm_out_sb, w_out, w_gate, w_up, w_down, norm_final):
    assert w_ada.shape[0] == 1, "single-layer trunk"
    bp, t, d = x_prompt.shape
    bs, tn, _ = x_sample.shape
    n_pages = page_table.shape[1]
    page = cache_nsa_cmp.shape[2]
    past = n_pages * page
    nb = t // CMP_BLOCK
    assert nb == LANES and past % CMP_BLOCK == 0 and NSA_HEADS * tn <= LANES
    slopes = _slopes()

    wi = w_in[0]
    c0 = NSA_WIDTH + 3 * KV_COLS
    w_cat = jnp.concatenate([wi[:, :c0], wi[:, c0 + 3 * NSA_HEADS:], wi[:, c0:c0 + 3 * NSA_HEADS],
                             jnp.zeros((d, _C_END - _C_G - 3 * NSA_HEADS), f32)], axis=1).astype(bf16)
    pe_row, w1_big, w2_big = _compress_weights(cmp_pe_k[0], cmp_k_w1[0], cmp_k_w2[0],
                                               cmp_pe_v[0], cmp_v_w1[0], cmp_v_w2[0])
    gexp = _gate_expand()
    wout_b, wg_b, wu_b, wd_b = (w_out[0].astype(bf16), w_gate[0].astype(bf16), w_up[0].astype(bf16),
                                w_down[0].astype(bf16))

    mods = _mods_call(jnp.concatenate([c_prompt, c_sample], axis=0), w_ada[0], b_ada[0])
    mods_p = mods[:bp].reshape(bp, 6, 1, d)
    mods_s = jnp.repeat(mods[bp:].reshape(bs, 6, d), tn, axis=0).reshape(bs * tn, 6, d)
    mp = [mods_p[:, i] for i in range(6)]
    ms = [mods_s[:, i].reshape(1, bs * tn, d) for i in range(6)]

    (kv_cmp, kv_sel, kv_win, kv_sb, gates, q_hm, qsb_hm, ksel_hm, kwin_hm, ksb_hm) = _proj_call(
        x_prompt, mp[0], mp[1], norm_attn[0], w_cat, 512)
    kcvc = _compress_call(kv_cmp.reshape(bp * nb, CMP_BLOCK * KV_COLS), pe_row, w1_big, w2_big)
    kcvc = kcvc.reshape(bp, nb, KV_COLS)
    o_cmp, selb, any_sel = _cmp_call(q_hm, kcvc)
    nq = t // Q_TILE
    any_tile = any_sel.reshape(bp, NSA_KV_HEADS, nq, nb * CMP_BLOCK // SEL_TK, SEL_TK // CMP_BLOCK).max(axis=-1)
    n_words = max(1, any_tile.shape[-1] // 32)
    bits = any_tile.reshape(bp, NSA_KV_HEADS, nq, n_words, -1).astype(jnp.uint32)
    flags = (bits << jnp.arange(bits.shape[-1], dtype=jnp.uint32)).sum(axis=-1, dtype=jnp.uint32)
    flags = lax.bitcast_convert_type(flags, jnp.int32).reshape(-1)
    pos_t = jnp.arange(t, dtype=jnp.int32)
    kconst = jnp.concatenate([jnp.zeros((t, HEAD_DIM), bf16), _alibi_key_cols(pos_t, HEAD_DIM),
                              (pos_t[:, None] // CMP_BLOCK == jnp.arange(nb)[None, :]).astype(bf16)], axis=1)
    qal = _alibi_query_cols(slopes, pos_t, HEAD_DIM)
    o_sel, o_win = _selwin_call(flags, q_hm, qal, selb, ksel_hm, kwin_hm, kconst)
    o_sb = _sb_call(qsb_hm, ksb_hm, _tri_weights(K_TILE))
    y_prompt = _out_ffn_call(x_prompt, o_cmp, o_sel, o_win, gates, o_sb, mp[2:], gexp, norm_out_nsa[0],
                             norm_out_sb[0], wout_b, norm_ffn[0], wg_b, wu_b, wd_b, norm_final, 512)
    win_len = min(WINDOW, t)
    new_win_prompt = kv_win[:, t - win_len:]

    n_tok = bs * tn
    (s_kv_cmp, s_kv_sel, s_kv_win, s_kv_sb, s_gates, s_q_hm, s_qsb_hm, _, _, _) = _proj_call(
        x_sample.reshape(1, n_tok, d), ms[0], ms[1], norm_attn[0], w_cat, n_tok)
    nb_past = past // CMP_BLOCK
    nbs = nb_past + 1
    nbp = 2 * LANES
    assert nbs <= nbp - 8
    cmp_pages = cache_nsa_cmp[0].transpose(0, 2, 3, 4, 1).reshape(-1, KV_COLS, page)
    kc_past = _compress_pages_call(page_table, cmp_pages, cmp_pe_k[0], cmp_k_w1[0], cmp_k_w2[0],
                                   cmp_pe_v[0], cmp_v_w1[0], cmp_v_w2[0])
    tail = jnp.pad(s_kv_cmp.reshape(bs, tn, KV_COLS), ((0, 0), (0, CMP_BLOCK - tn), (0, 0)))
    kc_tail = _compress_call(tail.reshape(bs, CMP_BLOCK * KV_COLS), pe_row, w1_big, w2_big)
    s_kcvc = jnp.concatenate([kc_past.reshape(bs, nb_past, KV_COLS), kc_tail[:, None, :],
                              jnp.zeros((bs, nbp - nbs, KV_COLS), f32)], axis=1)

    pos_q = past + jnp.arange(tn, dtype=jnp.int32)
    n_cols = NSA_HEADS * tn
    col_t = jnp.arange(LANES) % tn
    col_h = jnp.minimum(jnp.arange(LANES) // tn, NSA_HEADS - 1)
    col_slope = slopes.reshape(-1)[col_h]
    col_pos = past + col_t
    s_q = s_q_hm.reshape(NSA_HEADS, bs, tn, HEAD_DIM).transpose(1, 0, 2, 3)
    qbd = _block_diag_q(s_q, NSA_KV_HEADS)

    def per_b(shape):
        return pl.BlockSpec((1,) + shape, lambda i: (i,) + (0,) * len(shape))

    def const1(shape):
        return pl.BlockSpec(shape, lambda i: (0,) * len(shape))

    s_ocmp, s_sel = pl.pallas_call(
        functools.partial(_s_cmp_kernel, tn=tn),
        grid=(bs,),
        in_specs=[per_b((nbp, KV_COLS)), per_b((LANES, LANES)), const1((1, LANES)), const1((1, LANES))],
        out_specs=(per_b((LANES, LANES)), per_b((nbp, LANES))),
        out_shape=(jax.ShapeDtypeStruct((bs, LANES, LANES), f32), jax.ShapeDtypeStruct((bs, nbp, LANES), f32)),
        compiler_params=_params(1),
        name="s_cmp_select",
    )(s_kcvc, qbd, col_slope.reshape(1, LANES).astype(f32), col_pos.reshape(1, LANES).astype(jnp.int32))

    def pages_t(cache, width):
        return cache[0].transpose(0, 2, 3, 4, 1).reshape(-1, width, cache.shape[2])

    def new_t(x, width):
        return jnp.pad(x.reshape(bs, tn, width).transpose(0, 2, 1), ((0, 0), (0, 0), (0, K_TILE - tn)))

    col_g0 = (jnp.arange(LANES) // (NSA_GROUP * tn)) * (NSA_GROUP * tn) + col_t
    sel_cols = s_sel[:, :, col_g0]
    n_hot = nbp // 2 + 8
    k_aug = 2 * LANES
    selb_rows = jnp.where(sel_cols[:, :n_hot] > 0.5, 0.0, MASK_BIAS)
    al_rows = jnp.stack([col_slope * CMP_BLOCK, col_slope,
                         -col_slope * CMP_BLOCK * (col_pos // CMP_BLOCK).astype(f32),
                         -col_slope * (col_pos % CMP_BLOCK).astype(f32)], axis=0)
    saug_t = jnp.concatenate([selb_rows, jnp.broadcast_to(al_rows, (bs, 4, LANES)),
                              jnp.zeros((bs, k_aug - n_hot - 4, LANES), f32)], axis=1).astype(bf16).transpose(0, 2, 1)
    pos_p = jnp.arange(past, dtype=jnp.int32)
    eaug_t = jnp.concatenate([(pos_p[:, None] // CMP_BLOCK == jnp.arange(n_hot)[None, :]).astype(bf16),
                              _alibi_key_cols(pos_p, k_aug - n_hot)], axis=1).T
    win_rows = state_nsa_win.shape[2]
    pos_w = past - win_rows + jnp.arange(win_rows, dtype=jnp.int32)
    ewin_t = _alibi_key_cols(pos_w, LANES).T
    waug_t = jnp.concatenate([al_rows, jnp.zeros((LANES - 4, LANES), f32)], axis=0).astype(bf16).T
    dist_w = col_pos[:, None] - pos_w[None, :]
    wmask_t = jnp.where((dist_w >= 0) & (dist_w <= WINDOW) & (pos_w[None, :] >= 0), 0.0, MASK_BIAS).astype(f32)
    j_new = jnp.arange(K_TILE)[None, :]
    ok_new = (j_new <= col_t[:, None]) & (j_new < tn)
    nbias_t = jnp.where(ok_new, -col_slope[:, None] * (col_t[:, None] - j_new).astype(f32), MASK_BIAS).astype(f32)
    sel_pages = pages_t(cache_nsa_sel, KV_COLS)
    win_t = state_nsa_win[0].transpose(0, 2, 3, 4, 1).reshape(bs, KV_COLS, win_rows)
    win_buf = state_nsa_win[0].reshape(bs, win_rows, KV_COLS)
    pps = S_SEL_PAGES if n_pages % S_SEL_PAGES == 0 else 1

    def b2(shape):
        return pl.BlockSpec((1,) + shape, lambda i, j, pt: (i,) + (0,) * len(shape))

    def c2(shape):
        return pl.BlockSpec(shape, lambda i, j, pt: (0,) * len(shape))

    out_o = jax.ShapeDtypeStruct((bs, LANES, LANES), f32)
    s_osel, s_owin = pl.pallas_call(
        functools.partial(_s_selwin_kernel, n_pages_step=pps),
        grid_spec=pltpu.PrefetchScalarGridSpec(
            num_scalar_prefetch=1, grid=(bs, n_pages // pps),
            in_specs=[pl.BlockSpec((1, KV_COLS, page), lambda i, j, pt, k=k: (pt[i, j * pps + k], 0, 0))
                      for k in range(pps)]
            + [b2((LANES, LANES)), b2((LANES, k_aug)),
               pl.BlockSpec((k_aug, pps * page), lambda i, j, pt: (0, j)),
               b2((KV_COLS, K_TILE)), c2((LANES, K_TILE)), c2((LANES, LANES)), c2((LANES, win_rows)),
               c2((LANES, win_rows)), b2((KV_COLS, win_rows)), b2((KV_COLS, K_TILE))],
            out_specs=(b2((LANES, LANES)), b2((LANES, LANES))),
            scratch_shapes=[pltpu.VMEM((LANES, 1), f32), pltpu.VMEM((LANES, 1), f32),
                            pltpu.VMEM((LANES, LANES), f32)]),
        out_shape=(out_o, out_o),
        compiler_params=_params(2),
        name="s_sel_win",
    )(page_table, *([sel_pages] * pps), qbd.transpose(0, 2, 1), saug_t, eaug_t, new_t(s_kv_sel, KV_COLS), nbias_t,
      waug_t, ewin_t, wmask_t, win_t, new_t(s_kv_win, KV_COLS))

    def nsa_heads(o_full):
        o = o_full[:, :n_cols, :].reshape(bs, NSA_KV_HEADS, NSA_GROUP, tn, NSA_KV_HEADS, HEAD_DIM)
        o = jnp.stack([o[:, gi, :, :, gi, :] for gi in range(NSA_KV_HEADS)], axis=1)
        return o.transpose(0, 3, 1, 2, 4).reshape(1, n_tok, NSA_WIDTH)

    so_cmp = nsa_heads(s_ocmp)
    so_sel = nsa_heads(s_osel)
    so_win = nsa_heads(s_owin)

    s_qsb = s_qsb_hm.reshape(SB_HEADS, bs, tn, HEAD_DIM).transpose(1, 0, 2, 3)
    qbd_sb = _block_diag_q(s_qsb, SB_HEADS)
    nmask_t = ((j_new < col_t[:, None]) & (j_new < tn)).astype(f32)
    sb_pages = pages_t(cache_sb, 2 * SB_WIDTH)
    n_head = min(S_SB_HEAD_PAGES, n_pages)
    n_rest = n_pages - n_head
    pp2 = S_SB_TAIL_PAGES if n_rest % S_SB_TAIL_PAGES == 0 else 1
    sb_page_shape = (1, 2 * SB_WIDTH, page)
    qt_sb = qbd_sb.transpose(0, 2, 1)
    tri_sb = _tri_weights(K_TILE)

    def b1(shape):
        return pl.BlockSpec((1,) + shape, lambda i, pt: (i,) + (0,) * len(shape))

    def c1(shape):
        return pl.BlockSpec(shape, lambda i, pt: (0,) * len(shape))

    sb_state = (jax.ShapeDtypeStruct((bs, LANES, SB_WIDTH), f32), jax.ShapeDtypeStruct((bs, LANES, K_TILE), f32),
                jax.ShapeDtypeStruct((bs, 8, LANES), jnp.int32))
    sb_scratch = [pltpu.VMEM((LANES, K_TILE), f32), pltpu.VMEM((LANES, SB_WIDTH), f32)]
    s_osb, sb_carry, sb_alive = pl.pallas_call(
        functools.partial(_s_sb_head_kernel, n_pages_step=n_head),
        grid_spec=pltpu.PrefetchScalarGridSpec(
            num_scalar_prefetch=1, grid=(bs,),
            in_specs=[pl.BlockSpec(sb_page_shape, lambda i, pt, k=k: (pt[i, n_pages - 1 - k], 0, 0))
                      for k in range(n_head)]
            + [b1((LANES, SB_WIDTH)), b1((2 * SB_WIDTH, K_TILE)), c1((LANES, K_TILE)), c1((2 * K_TILE, 2 * K_TILE))],
            out_specs=(b1((LANES, SB_WIDTH)), b1((LANES, K_TILE)), b1((8, LANES))),
            scratch_shapes=sb_scratch),
        out_shape=sb_state,
        compiler_params=_params(1),
        name="s_stick_breaking",
    )(page_table, *([sb_pages] * n_head), qt_sb, new_t(s_kv_sb, 2 * SB_WIDTH), nmask_t, tri_sb)
    if n_rest:
        def old_page(i, j, pt, al, k):
            return (jnp.where(al[i] > 0, pt[i, n_rest - 1 - (j * pp2 + k)], 0), 0, 0)

        s_osb = pl.pallas_call(
            functools.partial(_s_sb_tail_kernel, n_pages_step=pp2),
            grid_spec=pltpu.PrefetchScalarGridSpec(
                num_scalar_prefetch=2, grid=(bs, n_rest // pp2),
                in_specs=[pl.BlockSpec(sb_page_shape, functools.partial(old_page, k=k)) for k in range(pp2)]
                + [pl.BlockSpec((1, LANES, SB_WIDTH), lambda i, j, pt, al: (i, 0, 0)),
                   pl.BlockSpec((1, LANES, SB_WIDTH), lambda i, j, pt, al: (i, 0, 0)),
                   pl.BlockSpec((1, LANES, K_TILE), lambda i, j, pt, al: (i, 0, 0)),
                   pl.BlockSpec((2 * K_TILE, 2 * K_TILE), lambda i, j, pt, al: (0, 0))],
                out_specs=pl.BlockSpec((1, LANES, SB_WIDTH), lambda i, j, pt, al: (i, 0, 0)),
                scratch_shapes=sb_scratch),
            out_shape=sb_state[0],
            compiler_params=_params(2),
            name="s_stick_breaking_tail",
        )(page_table, sb_alive[:, 0, 0], *([sb_pages] * pp2), qt_sb, s_osb, sb_carry, tri_sb)
    o = s_osb[:, :SB_HEADS * tn, :].reshape(bs, SB_HEADS, tn, SB_HEADS, HEAD_DIM)
    so_sb = jnp.stack([o[:, hi, :, hi, :] for hi in range(SB_HEADS)], axis=0)
    so_sb = so_sb.reshape(1, SB_HEADS, n_tok, HEAD_DIM)

    y_sample = _out_ffn_call(x_sample.reshape(1, n_tok, d), so_cmp, so_sel, so_win, s_gates, so_sb, ms[2:], gexp,
                             norm_out_nsa[0], norm_out_sb[0], wout_b, norm_ffn[0], wg_b, wu_b, wd_b, norm_final,
                             256).reshape(bs, tn, d)
    all_win = jnp.concatenate([win_buf, s_kv_win.reshape(bs, tn, KV_COLS)], axis=1)
    keep = min(WINDOW, win_rows + tn)
    new_win_sample = all_win[:, win_rows + tn - keep:]

    g_n = NSA_KV_HEADS
    return (y_prompt, y_sample,
            kv_cmp.reshape(1, bp, t, 2, g_n, HEAD_DIM), kv_sel.reshape(1, bp, t, 2, g_n, HEAD_DIM),
            kv_sb.reshape(1, bp, t, 2, SB_HEADS, HEAD_DIM), new_win_prompt.reshape(1, bp, win_len, 2, g_n, HEAD_DIM),
            s_kv_cmp.reshape(1, bs, tn, 2, g_n, HEAD_DIM), s_kv_sel.reshape(1, bs, tn, 2, g_n, HEAD_DIM),
            s_kv_sb.reshape(1, bs, tn, 2, SB_HEADS, HEAD_DIM), new_win_sample.reshape(1, bs, keep, 2, g_n, HEAD_DIM))
```

```python
import functools

import jax
import jax.numpy as jnp
from jax import lax
from jax.experimental import pallas as pl
from jax.experimental.pallas import tpu as pltpu

HEAD_DIM = 64
NSA_HEADS = 8
NSA_KV_HEADS = 2
NSA_GROUP = NSA_HEADS // NSA_KV_HEADS
SB_HEADS = 8
NSA_WIDTH = NSA_HEADS * HEAD_DIM
SB_WIDTH = SB_HEADS * HEAD_DIM
KV_COLS = 2 * NSA_KV_HEADS * HEAD_DIM
CMP_BLOCK = 64
CMP_HIDDEN = 128
N_SELECT = 16
WINDOW = 512
EPS = 1e-6
NEG = -1e30
MASK_BIAS = -1e9
LANES = 128
Q_TILE = 128
K_TILE = 128
QK_SCALE = HEAD_DIM ** -0.5
VMEM_LIMIT = 56 * 1024 * 1024

f32 = jnp.float32
bf16 = jnp.bfloat16


def _dot(a, b):
    return jnp.dot(a, b, preferred_element_type=f32)


def _dot_nt(a, b):
    return lax.dot_general(a, b, (((1,), (1,)), ((), ())), preferred_element_type=f32)


def _dot_tn(a, b):
    return lax.dot_general(a, b, (((0,), (0,)), ((), ())), preferred_element_type=f32)


def _split(x):
    hi = x.astype(bf16)
    lo = (x - hi.astype(f32)).astype(bf16)
    return hi, lo


def _rms(x, g):
    return x * lax.rsqrt(jnp.mean(x * x, axis=-1, keepdims=True) + EPS) * g


def _params(n_axes, vmem=VMEM_LIMIT):
    return pltpu.CompilerParams(dimension_semantics=("arbitrary",) * n_axes, vmem_limit_bytes=vmem)


def _mods_kernel(c_ref, w_ref, b_ref, o_ref):
    c = c_ref[...]
    a = c * (1.0 / (1.0 + jnp.exp(-c)))
    a_hi, a_lo = _split(a)
    w_hi, w_lo = _split(w_ref[...])
    o_ref[...] = _dot(a_hi, w_hi) + _dot(a_lo, w_hi) + _dot(a_hi, w_lo) + b_ref[...]


def _mods_call(c_all, w_ada, b_ada):
    m, d = c_all.shape
    n = w_ada.shape[1]
    tn = 1536 if n % 1536 == 0 else n
    return pl.pallas_call(
        _mods_kernel,
        grid=(n // tn,),
        in_specs=[pl.BlockSpec((m, d), lambda j: (0, 0)),
                  pl.BlockSpec((d, tn), lambda j: (0, j)),
                  pl.BlockSpec((1, tn), lambda j: (0, j))],
        out_specs=pl.BlockSpec((m, tn), lambda j: (0, j)),
        out_shape=jax.ShapeDtypeStruct((m, n), f32),
        compiler_params=_params(1),
        name="mods",
    )(c_all, w_ada, b_ada.reshape(1, n))


_C_Q, _C_KC, _C_KS, _C_KW, _C_QSB, _C_KVSB, _C_G, _C_END = 0, 512, 768, 1024, 1280, 1792, 2816, 2944


def _proj_kernel(x_ref, sh_ref, sc_ref, g_ref, w_ref,
                 kvc_ref, kvs_ref, kvw_ref, kvsb_ref, gate_ref,
                 qn_ref, qsb_ref, ksel_ref, kwin_ref, ksb_ref):
    x = x_ref[0]
    h = _rms(x, g_ref[...]) * (1.0 + sc_ref[0]) + sh_ref[0]
    r = _dot(h.astype(bf16), w_ref[...])
    kvc_ref[0] = r[:, _C_KC:_C_KS]
    kvs_ref[0] = r[:, _C_KS:_C_KW]
    kvw_ref[0] = r[:, _C_KW:_C_QSB]
    kvsb_ref[0] = r[:, _C_KVSB:_C_G]
    gate_ref[0] = 1.0 / (1.0 + jnp.exp(-r[:, _C_G:_C_END]))
    for h_i in range(NSA_HEADS):
        lo = _C_Q + h_i * HEAD_DIM
        qn_ref[0, h_i] = (r[:, lo:lo + HEAD_DIM] * QK_SCALE).astype(bf16)
    for h_i in range(SB_HEADS):
        lo = _C_QSB + h_i * HEAD_DIM
        qsb_ref[0, h_i] = (r[:, lo:lo + HEAD_DIM] * QK_SCALE).astype(bf16)
    for j in range(2 * NSA_KV_HEADS):
        lo = _C_KS + j * HEAD_DIM
        ksel_ref[0, j] = r[:, lo:lo + HEAD_DIM].astype(bf16)
        lo = _C_KW + j * HEAD_DIM
        kwin_ref[0, j] = r[:, lo:lo + HEAD_DIM].astype(bf16)
    for j in range(2 * SB_HEADS):
        lo = _C_KVSB + j * HEAD_DIM
        ksb_ref[0, j] = r[:, lo:lo + HEAD_DIM].astype(bf16)


def _proj_call(x, shift, scale, g, w_cat, tm):
    b, t, d = x.shape
    rows = shift.shape[1]
    mod_spec = (pl.BlockSpec((1, 1, d), lambda i, j: (i, 0, 0)) if rows == 1
                else pl.BlockSpec((1, tm, d), lambda i, j: (i, j, 0)))

    def flat(c):
        return pl.BlockSpec((1, tm, c), lambda i, j: (i, j, 0))

    def heads(n):
        return pl.BlockSpec((1, n, tm, HEAD_DIM), lambda i, j: (i, 0, j, 0))

    out_shape = (
        jax.ShapeDtypeStruct((b, t, KV_COLS), f32),
        jax.ShapeDtypeStruct((b, t, KV_COLS), f32),
        jax.ShapeDtypeStruct((b, t, KV_COLS), f32),
        jax.ShapeDtypeStruct((b, t, 2 * SB_WIDTH), f32),
        jax.ShapeDtypeStruct((b, t, LANES), f32),
        jax.ShapeDtypeStruct((b, NSA_HEADS, t, HEAD_DIM), bf16),
        jax.ShapeDtypeStruct((b, SB_HEADS, t, HEAD_DIM), bf16),
        jax.ShapeDtypeStruct((b, 2 * NSA_KV_HEADS, t, HEAD_DIM), bf16),
        jax.ShapeDtypeStruct((b, 2 * NSA_KV_HEADS, t, HEAD_DIM), bf16),
        jax.ShapeDtypeStruct((b, 2 * SB_HEADS, t, HEAD_DIM), bf16),
    )
    out_specs = (flat(KV_COLS), flat(KV_COLS), flat(KV_COLS), flat(2 * SB_WIDTH), flat(LANES),
                 heads(NSA_HEADS), heads(SB_HEADS), heads(2 * NSA_KV_HEADS), heads(2 * NSA_KV_HEADS),
                 heads(2 * SB_HEADS))
    return pl.pallas_call(
        _proj_kernel,
        grid=(b, t // tm),
        in_specs=[flat(d), mod_spec, mod_spec,
                  pl.BlockSpec((1, d), lambda i, j: (0, 0)),
                  pl.BlockSpec(w_cat.shape, lambda i, j: (0, 0))],
        out_specs=out_specs,
        out_shape=out_shape,
        compiler_params=_params(2),
        name="proj",
    )(x, shift, scale, g.reshape(1, d), w_cat)


def _gelu_tanh(x):
    return 0.5 * x * (1.0 + jnp.tanh(0.7978845608028654 * (x + 0.044715 * (x * x * x))))


def _compress_kernel(x_ref, pe_ref, w1_ref, w2_ref, o_ref, acc_ref):
    k = pl.program_id(1)

    @pl.when(k == 0)
    def _():
        acc_ref[...] = jnp.zeros_like(acc_ref)

    acc_ref[...] += _dot((x_ref[...] + pe_ref[...]).astype(bf16), w1_ref[...])

    @pl.when(k == pl.num_programs(1) - 1)
    def _():
        o_ref[...] = _dot(_gelu_tanh(acc_ref[...]).astype(bf16), w2_ref[...])


def _compress_call(x_rows, pe_row, w1_big, w2_big):
    m, kdim = x_rows.shape
    tm = 256 if m % 256 == 0 else m
    tk = 4096
    n1 = w1_big.shape[1]
    n2 = w2_big.shape[1]
    return pl.pallas_call(
        _compress_kernel,
        grid=(m // tm, kdim // tk),
        in_specs=[pl.BlockSpec((tm, tk), lambda i, k: (i, k)),
                  pl.BlockSpec((1, tk), lambda i, k: (0, k)),
                  pl.BlockSpec((tk, n1), lambda i, k: (k, 0)),
                  pl.BlockSpec((n1, n2), lambda i, k: (0, 0))],
        out_specs=pl.BlockSpec((tm, n2), lambda i, k: (i, 0)),
        out_shape=jax.ShapeDtypeStruct((m, n2), f32),
        scratch_shapes=[pltpu.VMEM((tm, n1), f32)],
        compiler_params=_params(2),
        name="compress",
    )(x_rows, pe_row, w1_big, w2_big)


def _compress_weights(pe_k, w1_k, w2_k, pe_v, w1_v, w2_v):
    g = NSA_KV_HEADS
    w1 = jnp.zeros((CMP_BLOCK, 2, g, HEAD_DIM, 2, g, CMP_HIDDEN), f32)
    w2 = jnp.zeros((2, g, CMP_HIDDEN, 2, g, HEAD_DIM), f32)
    for kv, (a1, a2) in enumerate(((w1_k, w2_k), (w1_v, w2_v))):
        a1 = a1.reshape(CMP_BLOCK, HEAD_DIM, CMP_HIDDEN)
        for gi in range(g):
            w1 = w1.at[:, kv, gi, :, kv, gi, :].set(a1)
            w2 = w2.at[kv, gi, :, kv, gi, :].set(a2)
    w1 = w1.reshape(CMP_BLOCK * 2 * g * HEAD_DIM, 2 * g * CMP_HIDDEN).astype(bf16)
    w2 = w2.reshape(2 * g * CMP_HIDDEN, 2 * g * HEAD_DIM).astype(bf16)
    pe = jnp.stack([jnp.broadcast_to(pe_k[:, None, :], (CMP_BLOCK, g, HEAD_DIM)),
                    jnp.broadcast_to(pe_v[:, None, :], (CMP_BLOCK, g, HEAD_DIM))], axis=1)
    return pe.reshape(1, -1), w1, w2


def _compress_pages_kernel(pt_ref, *refs, n_pages):
    pages = refs[:n_pages]
    pe_ref, w1_ref, w2_ref, o_ref, x_ref = refs[n_pages:]
    for k in range(n_pages):
        x_ref[k] = pages[k][0]
    g_n = NSA_KV_HEADS
    n_blk = w1_ref.shape[3] // CMP_HIDDEN
    for kv in range(2):
        def rows_of(d, kv=kv):
            rows = jnp.concatenate([x_ref[:, (kv * g_n + gi) * HEAD_DIM + d, :] for gi in range(g_n)], axis=0)
            return (rows + pe_ref[kv, pl.ds(d, 1), :]).astype(bf16)

        def body(i, h, kv=kv):
            lhs = jnp.concatenate([rows_of(2 * i), rows_of(2 * i + 1)], axis=1)
            return h + _dot(lhs, w1_ref[kv, i])

        h = lax.fori_loop(0, HEAD_DIM // 2, body, jnp.zeros((g_n * n_pages, n_blk * CMP_HIDDEN), f32), unroll=4)
        act = _gelu_tanh(h).astype(bf16)
        for blk in range(n_blk):
            res = _dot(act[:, blk * CMP_HIDDEN:(blk + 1) * CMP_HIDDEN], w2_ref[kv])
            for gi in range(g_n):
                lo = (kv * g_n + gi) * HEAD_DIM
                o_ref[0, blk, :, lo:lo + HEAD_DIM] = res[gi * n_pages:(gi + 1) * n_pages]


def _compress_pages_call(page_table, pages_t, pe_k, w1_k, w2_k, pe_v, w1_v, w2_v):
    b, n_pages = page_table.shape
    page = pages_t.shape[2]
    n_blk = page // CMP_BLOCK
    eye = jnp.eye(n_blk, dtype=f32)

    def w1_layout(w1):
        w = w1.reshape(CMP_BLOCK, HEAD_DIM, CMP_HIDDEN).transpose(1, 0, 2)
        w = jnp.einsum('djc,ab->dajbc', w, eye).reshape(HEAD_DIM, page, n_blk * CMP_HIDDEN)
        return w.reshape(HEAD_DIM // 2, 2 * page, n_blk * CMP_HIDDEN)

    w1 = jnp.stack([w1_layout(w1_k), w1_layout(w1_v)]).astype(bf16)
    w2 = jnp.stack([w2_k, w2_v]).astype(bf16)
    pe = jnp.stack([jnp.tile(pe_k.T, (1, n_blk)), jnp.tile(pe_v.T, (1, n_blk))])
    out = pl.pallas_call(
        functools.partial(_compress_pages_kernel, n_pages=n_pages),
        grid_spec=pltpu.PrefetchScalarGridSpec(
            num_scalar_prefetch=1, grid=(b,),
            in_specs=[pl.BlockSpec((1, KV_COLS, page), lambda i, pt, k=k: (pt[i, k], 0, 0)) for k in range(n_pages)]
            + [pl.BlockSpec(pe.shape, lambda i, pt: (0, 0, 0)),
               pl.BlockSpec(w1.shape, lambda i, pt: (0, 0, 0, 0)),
               pl.BlockSpec(w2.shape, lambda i, pt: (0, 0, 0))],
            out_specs=pl.BlockSpec((1, n_blk, n_pages, KV_COLS), lambda i, pt: (i, 0, 0, 0)),
            scratch_shapes=[pltpu.VMEM((n_pages, KV_COLS, page), f32)]),
        out_shape=jax.ShapeDtypeStruct((b, n_blk, n_pages, KV_COLS), f32),
        compiler_params=_params(1),
        name="compress_pages",
    )(page_table, *([pages_t] * n_pages), pe, w1, w2)
    return out.transpose(0, 2, 1, 3).reshape(b, n_pages * n_blk, KV_COLS)


def _select_blocks(imp, n_idx, cur, n_rounds):
    forced = (n_idx == 0) | (n_idx == cur) | (n_idx == cur - 1)
    in_range = n_idx <= cur
    v = jnp.where(forced, 8.0, imp)
    v = jnp.where(in_range, v, -1.0)
    sel = jnp.zeros(imp.shape, f32)
    big = jnp.int32(1 << 20)
    for _ in range(n_rounds):
        mx = jnp.max(v, axis=0, keepdims=True)
        idx = jnp.min(jnp.where(v == mx, n_idx, big), axis=0, keepdims=True)
        hit = n_idx == idx
        sel = jnp.where(hit, 1.0, sel)
        v = jnp.where(hit, -2.0, v)
    return jnp.where(in_range, sel, 0.0)


CMP_TQ = 256


def _cmp_kernel(q_ref, kv_ref, ocmp_ref, selb_ref, any_ref):
    qi = pl.program_id(1)
    tq = q_ref.shape[2]
    nb = kv_ref.shape[1]
    kvb = kv_ref[0]
    n_idx = lax.broadcasted_iota(jnp.int32, (nb, tq), 0)
    pos = qi * tq + lax.broadcasted_iota(jnp.int32, (nb, tq), 1)
    dist_i = pos - (n_idx * CMP_BLOCK + (CMP_BLOCK - 1))
    valid = dist_i >= 0
    dist = dist_i.astype(f32)
    imps, outs = [], []
    for g in range(NSA_KV_HEADS):
        kc = kvb[:, g * HEAD_DIM:(g + 1) * HEAD_DIM].astype(bf16)
        vc = kvb[:, (NSA_KV_HEADS + g) * HEAD_DIM:(NSA_KV_HEADS + g + 1) * HEAD_DIM].astype(bf16)
        imp = jnp.zeros((nb, tq), f32)
        for r in range(NSA_GROUP):
            slope = 2.0 ** -(g * NSA_GROUP + r + 1)
            s = _dot_nt(kc, q_ref[0, g * NSA_GROUP + r]) - slope * dist
            s = jnp.where(valid, s, NEG)
            e = jnp.exp(s - jnp.max(s, axis=0, keepdims=True))
            p = jnp.where(valid, e / jnp.sum(e, axis=0, keepdims=True), 0.0)
            imp = imp + p
            outs.append(_dot_tn(p.astype(bf16), vc))
        imps.append(imp)
    ocmp_ref[0] = jnp.concatenate(outs, axis=1)
    g_n = NSA_KV_HEADS
    n_idx2 = jnp.concatenate([n_idx] * g_n, axis=1)
    cur2 = jnp.concatenate([pos // CMP_BLOCK] * g_n, axis=1)
    sel = _select_blocks(jnp.concatenate(imps, axis=1), n_idx2, cur2, min(N_SELECT, nb))
    for g in range(g_n):
        sel_t = sel[:, g * tq:(g + 1) * tq].T
        selb_ref[0, g] = jnp.where(sel_t > 0.5, 0.0, MASK_BIAS).astype(bf16)
        for h in range(tq // Q_TILE):
            part = sel_t[h * Q_TILE:(h + 1) * Q_TILE]
            any_ref[0, g, h] = jnp.max(part, axis=0, keepdims=True).astype(jnp.int32)


def _cmp_call(q_hm, kcvc):
    b, _, t, _ = q_hm.shape
    nb = kcvc.shape[1]
    tq = CMP_TQ
    nq = t // tq
    g_n = NSA_KV_HEADS
    return pl.pallas_call(
        _cmp_kernel,
        grid=(b, nq),
        in_specs=[pl.BlockSpec((1, NSA_HEADS, tq, HEAD_DIM), lambda i, j: (i, 0, j, 0)),
                  pl.BlockSpec((1, nb, KV_COLS), lambda i, j: (i, 0, 0))],
        out_specs=(pl.BlockSpec((1, tq, NSA_WIDTH), lambda i, j: (i, j, 0)),
                   pl.BlockSpec((1, g_n, tq, nb), lambda i, j: (i, 0, j, 0)),
                   pl.BlockSpec((1, g_n, tq // Q_TILE, 1, nb), lambda i, j: (i, 0, j, 0, 0))),
        out_shape=(jax.ShapeDtypeStruct((b, t, NSA_WIDTH), f32),
                   jax.ShapeDtypeStruct((b, g_n, t, nb), bf16),
                   jax.ShapeDtypeStruct((b, g_n, t // Q_TILE, 1, nb), jnp.int32)),
        compiler_params=_params(2),
        name="cmp_select",
    )(q_hm, kcvc)


SEL_TK = 256


def _softmax_tile(k_tile, qa, v_tile, m_ref, l_ref, acc_ref, mask):
    s = _dot_nt(k_tile, qa)
    if mask is not None:
        s = jnp.where(mask, s, MASK_BIAS)
    m_old = m_ref[...]
    m_new = jnp.maximum(m_old, jnp.max(s, axis=0, keepdims=True))
    alpha = jnp.exp(m_old - m_new)
    p = jnp.exp(s - m_new)
    l_ref[...] = alpha * l_ref[...] + jnp.sum(p, axis=0, keepdims=True)
    acc_ref[...] = alpha * acc_ref[...] + _dot_tn(v_tile, p.astype(bf16))
    m_ref[...] = m_new


def _selwin_kernel(flag_ref, q_ref, qal_ref, selb_ref, ks_ref, vs_ref, kw_ref, vw_ref, kconst_ref,
                   osel_ref, owin_ref, kaug_s, kaug_w, qaug, m_ref, l_ref, acc_ref):
    b = pl.program_id(0)
    g = pl.program_id(1)
    qi = pl.program_id(2)
    tq = q_ref.shape[2]
    tk = SEL_TK
    rows = NSA_GROUP * tq
    t_all = ks_ref.shape[2]

    @pl.when(qi == 0)
    def _():
        kaug_s[...] = kconst_ref[...]
        kaug_s[:, 0:HEAD_DIM] = ks_ref[0, 0]
        kaug_w[...] = kconst_ref[:, 0:LANES]
        kaug_w[:, 0:HEAD_DIM] = kw_ref[0, 0]

    qaug[:, 0:HEAD_DIM] = q_ref[0].reshape(rows, HEAD_DIM)
    qaug[:, HEAD_DIM:LANES] = qal_ref[0].reshape(rows, HEAD_DIM)
    sb = selb_ref[0, 0]
    for r in range(NSA_GROUP):
        qaug[r * tq:(r + 1) * tq, LANES:] = sb
    qa = qaug[...]
    qa_w = qaug[:, 0:LANES]

    q0 = qi * tq
    jd = q0 // tk
    q_pos = q0 + (lax.broadcasted_iota(jnp.int32, (tk, rows), 1) & (tq - 1))
    k_rel = lax.broadcasted_iota(jnp.int32, (tk, rows), 0)

    def reset():
        m_ref[...] = jnp.full(m_ref.shape, NEG, f32)
        l_ref[...] = jnp.zeros(l_ref.shape, f32)
        acc_ref[...] = jnp.zeros(acc_ref.shape, f32)

    def finish(o_ref):
        o_t = acc_ref[...] / l_ref[...]
        o_ref[0] = jnp.concatenate([o_t[:, r * tq:(r + 1) * tq].T for r in range(NSA_GROUP)], axis=1)

    def tile(kaug, v_ref, qmat, j, mask):
        off = pl.multiple_of(j * tk, tk)
        _softmax_tile(kaug[pl.ds(off, tk), :], qmat, v_ref[0, 0, pl.ds(off, tk), :], m_ref, l_ref, acc_ref, mask)

    reset()
    n_words = max(1, (t_all // tk) // 32)
    base = ((b * NSA_KV_HEADS + g) * (t_all // tq) + qi) * n_words

    def next_hit(j):
        def miss(t):
            return jnp.logical_and(t < jd, ((flag_ref[base + (t >> 5)] >> (t & 31)) & 1) == 0)

        return lax.while_loop(miss, lambda t: t + 1, j)

    def sel_body(j):
        j1 = next_hit(j)
        j2 = next_hit(jnp.minimum(j1 + 1, jd))

        @pl.when(j2 < jd)
        def _():
            off1 = pl.multiple_of(j1 * tk, tk)
            off2 = pl.multiple_of(j2 * tk, tk)
            k2 = jnp.concatenate([kaug_s[pl.ds(off1, tk), :], kaug_s[pl.ds(off2, tk), :]], axis=0)
            v2 = jnp.concatenate([vs_ref[0, 0, pl.ds(off1, tk), :], vs_ref[0, 0, pl.ds(off2, tk), :]], axis=0)
            _softmax_tile(k2, qa, v2, m_ref, l_ref, acc_ref, None)

        @pl.when(jnp.logical_and(j1 < jd, j2 >= jd))
        def _():
            tile(kaug_s, vs_ref, qa, j1, None)

        return j2 + 1

    lax.while_loop(lambda j: j < jd, sel_body, jnp.int32(0))
    tile(kaug_s, vs_ref, qa, jd, jd * tk + k_rel <= q_pos)

    n_wt = WINDOW // tk + 1
    start = jnp.maximum(jd - (n_wt - 1), 0)
    off_w = pl.multiple_of(start * tk, tk)
    s_w = _dot_nt(kaug_w[pl.ds(off_w, n_wt * tk), :], qa_w)
    kp = start * tk + lax.broadcasted_iota(jnp.int32, s_w.shape, 0)
    qp = q0 + (lax.broadcasted_iota(jnp.int32, s_w.shape, 1) & (tq - 1))
    s_w = jnp.where(kp <= qp, jnp.where(kp >= qp - WINDOW, s_w, MASK_BIAS), MASK_BIAS)
    p_w = jnp.exp(s_w - jnp.max(s_w, axis=0, keepdims=True))
    p_w = p_w * (1.0 / jnp.sum(p_w, axis=0, keepdims=True))
    ow_t = _dot_tn(vw_ref[0, 0, pl.ds(off_w, n_wt * tk), :], p_w.astype(bf16))
    owin_ref[0] = jnp.concatenate([ow_t[:, r * tq:(r + 1) * tq].T for r in range(NSA_GROUP)], axis=1)
    finish(osel_ref)


def _selwin_call(flags, q_hm, qal, selb, ksel_hm, kwin_hm, kconst):
    b, _, t, _ = q_hm.shape
    tq = Q_TILE
    nq = t // tq
    ka = kconst.shape[1]
    g_n = NSA_KV_HEADS

    def kv_spec(off):
        return pl.BlockSpec((1, 1, t, HEAD_DIM), lambda i, g, j, fl: (i, off + g, 0, 0))

    out_spec = pl.BlockSpec((1, tq, NSA_GROUP * HEAD_DIM), lambda i, g, j, fl: (i, j, g))
    return pl.pallas_call(
        _selwin_kernel,
        grid_spec=pltpu.PrefetchScalarGridSpec(
            num_scalar_prefetch=1, grid=(b, g_n, nq),
            in_specs=[pl.BlockSpec((1, NSA_GROUP, tq, HEAD_DIM), lambda i, g, j, fl: (i, g, j, 0)),
                      pl.BlockSpec((1, NSA_GROUP, tq, HEAD_DIM), lambda i, g, j, fl: (g, 0, j, 0)),
                      pl.BlockSpec((1, 1, tq, ka - LANES), lambda i, g, j, fl: (i, g, j, 0)),
                      kv_spec(0), kv_spec(g_n), kv_spec(0), kv_spec(g_n),
                      pl.BlockSpec((t, ka), lambda i, g, j, fl: (0, 0))],
            out_specs=(out_spec, out_spec),
            scratch_shapes=[pltpu.VMEM((t, ka), bf16), pltpu.VMEM((t, LANES), bf16),
                            pltpu.VMEM((NSA_GROUP * tq, ka), bf16),
                            pltpu.VMEM((1, NSA_GROUP * tq), f32), pltpu.VMEM((1, NSA_GROUP * tq), f32),
                            pltpu.VMEM((HEAD_DIM, NSA_GROUP * tq), f32)]),
        out_shape=(jax.ShapeDtypeStruct((b, t, NSA_WIDTH), f32), jax.ShapeDtypeStruct((b, t, NSA_WIDTH), f32)),
        compiler_params=_params(3),
        name="sel_win",
    )(flags, q_hm, qal, selb, ksel_hm, ksel_hm, kwin_hm, kwin_hm, kconst)


def _alibi_key_cols(pos, width):
    cols = jnp.stack([pos // CMP_BLOCK, pos % CMP_BLOCK, jnp.ones_like(pos), jnp.ones_like(pos)], axis=1)
    return jnp.pad(cols.astype(f32), ((0, 0), (0, width - 4))).astype(bf16)


def _alibi_query_cols(slopes, pos, width):
    s = slopes[..., None]
    a = (pos // CMP_BLOCK).astype(f32)
    r = (pos % CMP_BLOCK).astype(f32)
    cols = jnp.stack([jnp.broadcast_to(s * CMP_BLOCK, s.shape[:-1] + pos.shape),
                      jnp.broadcast_to(s, s.shape[:-1] + pos.shape),
                      -s * CMP_BLOCK * a, -s * r], axis=-1)
    pad = [(0, 0)] * (cols.ndim - 1) + [(0, width - 4)]
    return jnp.pad(cols, pad).astype(bf16)


def _slopes():
    h = jnp.arange(1, NSA_HEADS + 1, dtype=f32)
    return jnp.exp2(-8.0 * h / NSA_HEADS).reshape(NSA_KV_HEADS, NSA_GROUP)


def _neg_softplus(z):
    return -(jnp.maximum(z, 0.0) + jnp.log(1.0 + jnp.exp(-jnp.abs(z))))


SB_TILE = 512
SB_DEAD_LOG = -105.0


SB_BAND = 128


def _sb_kernel(q_ref, k_ref, v_ref, tri_ref, o_ref, carry_ref, acc_ref):
    qi = pl.program_id(2)
    tq = q_ref.shape[2]
    tk = tq
    sub = tri_ref.shape[1] // 2
    tri = tri_ref[...]

    def tile(j, mask, r0, nr):
        off = pl.multiple_of(j * tk, tk)
        z = _dot_nt(q_ref[0, 0, r0:r0 + nr, :], k_ref[0, 0, pl.ds(off, tk), :])
        c = _neg_softplus(z)
        if mask is not None:
            c = jnp.where(mask, c, 0.0)
        c_hi, c_lo = _split(c)
        run = carry_ref[r0:r0 + nr, :]
        parts = [None] * (tk // sub)
        for i in reversed(range(tk // sub)):
            sl = slice(i * sub, (i + 1) * sub)
            res = _dot(jnp.concatenate([c_hi[:, sl], c_lo[:, sl]], axis=1), tri)
            parts[i] = z[:, sl] + res[:, :sub] + run
            run = run + res[:, sub:]
        a = jnp.exp(jnp.concatenate(parts, axis=1))
        if mask is not None:
            a = jnp.where(mask, a, 0.0)
        acc_ref[r0:r0 + nr, :] += _dot(a.astype(bf16), v_ref[0, 0, pl.ds(off, tk), :])
        carry_ref[r0:r0 + nr, :] = run

    carry_ref[...] = jnp.zeros(carry_ref.shape, f32)
    acc_ref[...] = jnp.zeros(acc_ref.shape, f32)
    t_idx = lax.broadcasted_iota(jnp.int32, (tq, tk), 0)
    k_idx = lax.broadcasted_iota(jnp.int32, (tq, tk), 1)
    tile(qi, k_idx < t_idx, 0, tq)

    for r0 in range(0, tq, SB_BAND):
        def body(state, r0=r0):
            tile(qi - 1 - state[0], None, r0, SB_BAND)
            return state[0] + 1, jnp.max(carry_ref[r0:r0 + SB_BAND, :])

        lax.while_loop(lambda st: jnp.logical_and(st[0] < qi, st[1] > SB_DEAD_LOG), body,
                       (jnp.int32(0), jnp.max(carry_ref[r0:r0 + SB_BAND, :])))
    o_ref[0, 0] = acc_ref[...]


def _sb_call(q_hm, kv_hm, tri):
    b, h, t, _ = q_hm.shape
    tq = SB_TILE
    return pl.pallas_call(
        _sb_kernel,
        grid=(b, h, t // tq),
        in_specs=[pl.BlockSpec((1, 1, tq, HEAD_DIM), lambda i, hh, j: (i, hh, j, 0)),
                  pl.BlockSpec((1, 1, t, HEAD_DIM), lambda i, hh, j: (i, hh, 0, 0)),
                  pl.BlockSpec((1, 1, t, HEAD_DIM), lambda i, hh, j: (i, h + hh, 0, 0)),
                  pl.BlockSpec(tri.shape, lambda i, hh, j: (0, 0))],
        out_specs=pl.BlockSpec((1, 1, tq, HEAD_DIM), lambda i, hh, j: (i, hh, j, 0)),
        out_shape=jax.ShapeDtypeStruct((b, h, t, HEAD_DIM), f32),
        scratch_shapes=[pltpu.VMEM((tq, tri.shape[1] // 2), f32), pltpu.VMEM((tq, HEAD_DIM), f32)],
        compiler_params=_params(3),
        name="stick_breaking",
    )(q_hm, kv_hm, kv_hm, tri)


def _tri_weights(n):
    j = jnp.arange(n)[:, None]
    s = jnp.arange(n)[None, :]
    w = jnp.concatenate([(j >= s).astype(bf16), jnp.ones((n, n), bf16)], axis=1)
    return jnp.concatenate([w, w], axis=0)


def _out_ffn_kernel(x_ref, ocmp_ref, osel_ref, owin_ref, gate_ref, osb_ref,
                    gmsa_ref, shmlp_ref, scmlp_ref, gmlp_ref,
                    gexp_ref, nnsa_ref, nsb_ref, wout_ref, nffn_ref, wg_ref, wu_ref, wd_ref, nfin_ref,
                    y_ref, x1_ref, h2_ref, acc_ref):
    f = pl.program_id(2)

    @pl.when(f == 0)
    def _():
        g_hi, g_lo = _split(gate_ref[0])
        gx = _dot(g_hi, gexp_ref[...]) + _dot(g_lo, gexp_ref[...])
        w = NSA_WIDTH
        o_nsa = gx[:, 0:w] * ocmp_ref[0] + gx[:, w:2 * w] * osel_ref[0] + gx[:, 2 * w:3 * w] * owin_ref[0]
        o_sb = jnp.concatenate([osb_ref[0, h_i] for h_i in range(SB_HEADS)], axis=1)
        cat = jnp.concatenate([_rms(o_nsa, nnsa_ref[...]), _rms(o_sb, nsb_ref[...])], axis=1)
        x1 = x_ref[0] + gmsa_ref[0] * _dot(cat.astype(bf16), wout_ref[...])
        x1_ref[...] = x1
        h2_ref[...] = (_rms(x1, nffn_ref[...]) * (1.0 + scmlp_ref[0]) + shmlp_ref[0]).astype(bf16)
        acc_ref[...] = jnp.zeros_like(acc_ref)

    h2 = h2_ref[...]
    gt = _dot(h2, wg_ref[...])
    up = _dot(h2, wu_ref[...])
    act = gt * (1.0 / (1.0 + jnp.exp(-gt))) * up
    acc_ref[...] += _dot(act.astype(bf16), wd_ref[...])

    @pl.when(f == pl.num_programs(2) - 1)
    def _():
        y = x1_ref[...] + gmlp_ref[0] * acc_ref[...]
        y_ref[0] = _rms(y, nfin_ref[...])


def _out_ffn_call(x, ocmp, osel, owin, gates, osb_hm, mods4, gexp, nnsa, nsb, wout, nffn, wg, wu, wd, nfin, tm):
    b, t, d = x.shape
    tm = min(tm, t)
    dff = wg.shape[1]
    tf = dff // 2 if (dff // 2) % LANES == 0 else dff
    rows = mods4[0].shape[1]
    mod_spec = (pl.BlockSpec((1, 1, d), lambda i, j, f: (i, 0, 0)) if rows == 1
                else pl.BlockSpec((1, tm, d), lambda i, j, f: (i, j, 0)))

    def flat(c):
        return pl.BlockSpec((1, tm, c), lambda i, j, f: (i, j, 0))

    def const(shape):
        return pl.BlockSpec(shape, lambda i, j, f: (0,) * len(shape))

    return pl.pallas_call(
        _out_ffn_kernel,
        grid=(b, t // tm, dff // tf),
        in_specs=[flat(d), flat(NSA_WIDTH), flat(NSA_WIDTH), flat(NSA_WIDTH), flat(LANES),
                  pl.BlockSpec((1, SB_HEADS, tm, HEAD_DIM), lambda i, j, f: (i, 0, j, 0)),
                  mod_spec, mod_spec, mod_spec, mod_spec,
                  const(gexp.shape), const((1, NSA_WIDTH)), const((1, SB_WIDTH)), const(wout.shape),
                  const((1, d)),
                  pl.BlockSpec((d, tf), lambda i, j, f: (0, f)),
                  pl.BlockSpec((d, tf), lambda i, j, f: (0, f)),
                  pl.BlockSpec((tf, d), lambda i, j, f: (f, 0)),
                  const((1, d))],
        out_specs=flat(d),
        out_shape=jax.ShapeDtypeStruct((b, t, d), f32),
        scratch_shapes=[pltpu.VMEM((tm, d), f32), pltpu.VMEM((tm, d), bf16), pltpu.VMEM((tm, d), f32)],
        compiler_params=_params(3),
        name="out_ffn",
    )(x, ocmp, osel, owin, gates, osb_hm, *mods4, gexp, nnsa.reshape(1, -1), nsb.reshape(1, -1), wout,
      nffn.reshape(1, -1), wg, wu, wd, nfin.reshape(1, -1))


def _gate_expand():
    rows = jnp.arange(LANES)[:, None]
    cols = jnp.arange(3 * NSA_WIDTH)[None, :]
    return (rows == (cols // NSA_WIDTH) * NSA_HEADS + (cols % NSA_WIDTH) // HEAD_DIM).astype(bf16)


def _s_cmp_kernel(kv_ref, qbd_ref, qal_ref, pq_ref, ocmp_ref, sel_ref, *, tn):
    nbp = kv_ref.shape[1]
    kvb = kv_ref[0]
    kc = kvb[:, 0:LANES].astype(bf16)
    vc = kvb[:, LANES:2 * LANES].astype(bf16)
    n_idx = lax.broadcasted_iota(jnp.int32, (nbp, LANES), 0)
    pos = jnp.broadcast_to(pq_ref[...], (nbp, LANES))
    dist_i = pos - (n_idx * CMP_BLOCK + (CMP_BLOCK - 1))
    valid = dist_i >= 0
    slope = qal_ref[...]
    s = _dot(kc, qbd_ref[0]) - slope * dist_i.astype(f32)
    s = jnp.where(valid, s, NEG)
    e = jnp.exp(s - jnp.max(s, axis=0, keepdims=True))
    p = jnp.where(valid, e / jnp.sum(e, axis=0, keepdims=True), 0.0)
    ocmp_ref[0] = _dot_tn(p.astype(bf16), vc)
    imp = p
    for r in range(1, NSA_GROUP):
        imp = imp + pltpu.roll(p, LANES - r * tn, axis=1)
    cur = pos // CMP_BLOCK
    sel_ref[0] = _select_blocks(imp, n_idx, cur, min(N_SELECT, nbp))


S_SEL_PAGES = 16


def _row_softmax_pv(z_list, vt_list):
    m = z_list[0].max(axis=1, keepdims=True)
    for z in z_list[1:]:
        m = jnp.maximum(m, z.max(axis=1, keepdims=True))
    e_list = [jnp.exp(z - m) for z in z_list]
    l = e_list[0].sum(axis=1, keepdims=True)
    for e in e_list[1:]:
        l = l + e.sum(axis=1, keepdims=True)
    inv = 1.0 / l
    o = None
    for e, vt in zip(e_list, vt_list):
        pv = _dot_nt((e * inv).astype(bf16), vt)
        o = pv if o is None else o + pv
    return o


def _s_selwin_kernel(pt_ref, *refs, n_pages_step):
    pages = refs[:n_pages_step]
    (qt_ref, saug_ref, eaug_ref, news_ref, nbias_ref, waug_ref, ewin_ref, wmask_ref, kvw_ref, neww_ref,
     osel_ref, owin_ref, m_ref, l_ref, acc_ref) = refs[n_pages_step:]
    j = pl.program_id(1)
    qt = qt_ref[0]
    half = LANES

    def online(z, vt):
        m_old = m_ref[...]
        m_new = jnp.maximum(m_old, z.max(axis=1, keepdims=True))
        alpha = jnp.exp(m_old - m_new)
        p = jnp.exp(z - m_new)
        l_ref[...] = alpha * l_ref[...] + p.sum(axis=1, keepdims=True)
        acc_ref[...] = alpha * acc_ref[...] + _dot_nt(p.astype(bf16), vt)
        m_ref[...] = m_new

    @pl.when(j == 0)
    def _():
        kvw = kvw_ref[0]
        z_w = _dot(qt, kvw[0:half].astype(bf16)) + _dot(waug_ref[...], ewin_ref[...]) + wmask_ref[...]
        new_w = neww_ref[0]
        z_n = _dot(qt, new_w[0:half].astype(bf16)) + nbias_ref[...]
        owin_ref[0] = _row_softmax_pv([z_w, z_n], [kvw[half:].astype(bf16), new_w[half:].astype(bf16)])
        m_ref[...] = jnp.full(m_ref.shape, NEG, f32)
        l_ref[...] = jnp.zeros(l_ref.shape, f32)
        acc_ref[...] = jnp.zeros(acc_ref.shape, f32)
        new_s = news_ref[0]
        online(_dot(qt, new_s[0:half].astype(bf16)) + nbias_ref[...], new_s[half:].astype(bf16))

    kt = jnp.concatenate([p_ref[0, 0:half, :].astype(bf16) for p_ref in pages], axis=1)
    vt = jnp.concatenate([p_ref[0, half:, :].astype(bf16) for p_ref in pages], axis=1)
    online(_dot(qt, kt) + _dot(saug_ref[0], eaug_ref[...]), vt)

    @pl.when(j == pl.num_programs(1) - 1)
    def _():
        osel_ref[0] = acc_ref[...] / l_ref[...]


S_SB_HEAD_PAGES = 8
S_SB_TAIL_PAGES = 14


def _sb_pieces(qt, tri, tiles, mask, carry_ref, acc_ref):
    w = SB_WIDTH
    n_tok = tri.shape[1] // 2
    zs, ress = [], []
    for t_f in tiles:
        z = _dot(qt, t_f[0:w].astype(bf16))
        c = _neg_softplus(z)
        if mask is not None:
            c = c * mask
        c_hi, c_lo = _split(c)
        zs.append(z)
        ress.append(_dot(jnp.concatenate([c_hi, c_lo], axis=1), tri))
    run = carry_ref[...]
    acc = acc_ref[...]
    for t_f, z, res in zip(tiles, zs, ress):
        a = jnp.exp(z + res[:, :n_tok] + run)
        if mask is not None:
            a = a * mask
        acc = acc + _dot_nt(a.astype(bf16), t_f[w:].astype(bf16))
        run = run + res[:, n_tok:]
    carry_ref[...] = run
    acc_ref[...] = acc


def _s_sb_head_kernel(pt_ref, *refs, n_pages_step):
    pages = refs[:n_pages_step]
    qt_ref, new_ref, nmask_ref, tri_ref, acc_out, carry_out, alive_out, carry_ref, acc_ref = refs[n_pages_step:]
    carry_ref[...] = jnp.zeros_like(carry_ref)
    acc_ref[...] = jnp.zeros_like(acc_ref)
    qt = qt_ref[0]
    tri = tri_ref[...]
    _sb_pieces(qt, tri, [new_ref[0]], nmask_ref[...], carry_ref, acc_ref)
    _sb_pieces(qt, tri, [p_ref[0] for p_ref in pages], None, carry_ref, acc_ref)
    acc_out[0] = acc_ref[...]
    carry = carry_ref[...]
    carry_out[0] = carry
    top = jnp.max(jnp.max(carry, axis=1, keepdims=True), axis=0, keepdims=True)
    alive_out[0] = jnp.broadcast_to(jnp.where(top > SB_DEAD_LOG, 1, 0).astype(jnp.int32), alive_out.shape[1:])


def _s_sb_tail_kernel(pt_ref, alive_ref, *refs, n_pages_step):
    pages = refs[:n_pages_step]
    qt_ref, acc_in, carry_in, tri_ref, o_ref, carry_ref, acc_ref = refs[n_pages_step:]
    i = pl.program_id(0)
    j = pl.program_id(1)

    @pl.when(j == 0)
    def _():
        carry_ref[...] = carry_in[0]
        acc_ref[...] = acc_in[0]

    @pl.when(jnp.logical_and(alive_ref[i] > 0, jnp.max(carry_ref[...]) > SB_DEAD_LOG))
    def _():
        _sb_pieces(qt_ref[0], tri_ref[...], [p_ref[0] for p_ref in pages], None, carry_ref, acc_ref)

    @pl.when(j == pl.num_programs(1) - 1)
    def _():
        o_ref[0] = acc_ref[...]


def _block_diag_q(q, n_groups):
    bs, h, tn, hd = q.shape
    per = h // n_groups
    eye = jnp.eye(n_groups, dtype=q.dtype)
    qg = q.reshape(bs, n_groups, per, tn, hd)
    out = jnp.einsum('bgrtd,gk->bgdkrt', qg, eye).reshape(bs, n_groups * hd, h * tn)
    return jnp.pad(out, ((0, 0), (0, 0), (0, LANES - h * tn)))


def kernel(x_prompt, x_sample, cache_nsa_cmp, cache_nsa_sel, cache_sb, state_nsa_win, page_table,
           c_prompt, c_sample, w_ada, b_ada, norm_attn, norm_ffn, w_in, cmp_pe_k, cmp_k_w1, cmp_k_w2,
           cmp_pe_v, cmp_v_w1, cmp_v_w2, norm_out_nsa, norm_out_sb, w_out, w_gate, w_up, w_down, norm_final):
    assert w_ada.shape[0] == 1, "single-layer trunk"
    bp, t, d = x_prompt.shape
    bs, tn, _ = x_sample.shape
    n_pages = page_table.shape[1]
    page = cache_nsa_cmp.shape[2]
    past = n_pages * page
    nb = t // CMP_BLOCK
    assert nb == LANES and past % CMP_BLOCK == 0 and NSA_HEADS * tn <= LANES
    slopes = _slopes()

    wi = w_in[0]
    c0 = NSA_WIDTH + 3 * KV_COLS
    w_cat = jnp.concatenate([wi[:, :c0], wi[:, c0 + 3 * NSA_HEADS:], wi[:, c0:c0 + 3 * NSA_HEADS],
                             jnp.zeros((d, _C_END - _C_G - 3 * NSA_HEADS), f32)], axis=1).astype(bf16)
    pe_row, w1_big, w2_big = _compress_weights(cmp_pe_k[0], cmp_k_w1[0], cmp_k_w2[0],
                                               cmp_pe_v[0], cmp_v_w1[0], cmp_v_w2[0])
    gexp = _gate_expand()
    wout_b, wg_b, wu_b, wd_b = (w_out[0].astype(bf16), w_gate[0].astype(bf16), w_up[0].astype(bf16),
                                w_down[0].astype(bf16))

    mods = _mods_call(jnp.concatenate([c_prompt, c_sample], axis=0), w_ada[0], b_ada[0])
    mods_p = mods[:bp].reshape(bp, 6, 1, d)
    mods_s = jnp.repeat(mods[bp:].reshape(bs, 6, d), tn, axis=0).reshape(bs * tn, 6, d)
    mp = [mods_p[:, i] for i in range(6)]
    ms = [mods_s[:, i].reshape(1, bs * tn, d) for i in range(6)]

    (kv_cmp, kv_sel, kv_win, kv_sb, gates, q_hm, qsb_hm, ksel_hm, kwin_hm, ksb_hm) = _proj_call(
        x_prompt, mp[0], mp[1], norm_attn[0], w_cat, 512)
    kcvc = _compress_call(kv_cmp.reshape(bp * nb, CMP_BLOCK * KV_COLS), pe_row, w1_big, w2_big)
    kcvc = kcvc.reshape(bp, nb, KV_COLS)
    o_cmp, selb, any_sel = _cmp_call(q_hm, kcvc)
    nq = t // Q_TILE
    any_tile = any_sel.reshape(bp, NSA_KV_HEADS, nq, nb * CMP_BLOCK // SEL_TK, SEL_TK // CMP_BLOCK).max(axis=-1)
    n_words = max(1, any_tile.shape[-1] // 32)
    bits = any_tile.reshape(bp, NSA_KV_HEADS, nq, n_words, -1).astype(jnp.uint32)
    flags = (bits << jnp.arange(bits.shape[-1], dtype=jnp.uint32)).sum(axis=-1, dtype=jnp.uint32)
    flags = lax.bitcast_convert_type(flags, jnp.int32).reshape(-1)
    pos_t = jnp.arange(t, dtype=jnp.int32)
    kconst = jnp.concatenate([jnp.zeros((t, HEAD_DIM), bf16), _alibi_key_cols(pos_t, HEAD_DIM),
                              (pos_t[:, None] // CMP_BLOCK == jnp.arange(nb)[None, :]).astype(bf16)], axis=1)
    qal = _alibi_query_cols(slopes, pos_t, HEAD_DIM)
    o_sel, o_win = _selwin_call(flags, q_hm, qal, selb, ksel_hm, kwin_hm, kconst)
    o_sb = _sb_call(qsb_hm, ksb_hm, _tri_weights(K_TILE))
    y_prompt = _out_ffn_call(x_prompt, o_cmp, o_sel, o_win, gates, o_sb, mp[2:], gexp, norm_out_nsa[0],
                             norm_out_sb[0], wout_b, norm_ffn[0], wg_b, wu_b, wd_b, norm_final, 512)
    win_len = min(WINDOW, t)
    new_win_prompt = kv_win[:, t - win_len:]

    n_tok = bs * tn
    (s_kv_cmp, s_kv_sel, s_kv_win, s_kv_sb, s_gates, s_q_hm, s_qsb_hm, _, _, _) = _proj_call(
        x_sample.reshape(1, n_tok, d), ms[0], ms[1], norm_attn[0], w_cat, n_tok)
    nb_past = past // CMP_BLOCK
    nbs = nb_past + 1
    nbp = 2 * LANES
    assert nbs <= nbp - 8
    cmp_pages = cache_nsa_cmp[0].transpose(0, 2, 3, 4, 1).reshape(-1, KV_COLS, page)
    kc_past = _compress_pages_call(page_table, cmp_pages, cmp_pe_k[0], cmp_k_w1[0], cmp_k_w2[0],
                                   cmp_pe_v[0], cmp_v_w1[0], cmp_v_w2[0])
    tail = jnp.pad(s_kv_cmp.reshape(bs, tn, KV_COLS), ((0, 0), (0, CMP_BLOCK - tn), (0, 0)))
    kc_tail = _compress_call(tail.reshape(bs, CMP_BLOCK * KV_COLS), pe_row, w1_big, w2_big)
    s_kcvc = jnp.concatenate([kc_past.reshape(bs, nb_past, KV_COLS), kc_tail[:, None, :],
                              jnp.zeros((bs, nbp - nbs, KV_COLS), f32)], axis=1)

    pos_q = past + jnp.arange(tn, dtype=jnp.int32)
    n_cols = NSA_HEADS * tn
    col_t = jnp.arange(LANES) % tn
    col_h = jnp.minimum(jnp.arange(LANES) // tn, NSA_HEADS - 1)
    col_slope = slopes.reshape(-1)[col_h]
    col_pos = past + col_t
    s_q = s_q_hm.reshape(NSA_HEADS, bs, tn, HEAD_DIM).transpose(1, 0, 2, 3)
    qbd = _block_diag_q(s_q, NSA_KV_HEADS)

    def per_b(shape):
        return pl.BlockSpec((1,) + shape, lambda i: (i,) + (0,) * len(shape))

    def const1(shape):
        return pl.BlockSpec(shape, lambda i: (0,) * len(shape))

    s_ocmp, s_sel = pl.pallas_call(
        functools.partial(_s_cmp_kernel, tn=tn),
        grid=(bs,),
        in_specs=[per_b((nbp, KV_COLS)), per_b((LANES, LANES)), const1((1, LANES)), const1((1, LANES))],
        out_specs=(per_b((LANES, LANES)), per_b((nbp, LANES))),
        out_shape=(jax.ShapeDtypeStruct((bs, LANES, LANES), f32), jax.ShapeDtypeStruct((bs, nbp, LANES), f32)),
        compiler_params=_params(1),
        name="s_cmp_select",
    )(s_kcvc, qbd, col_slope.reshape(1, LANES).astype(f32), col_pos.reshape(1, LANES).astype(jnp.int32))

    def pages_t(cache, width):
        return cache[0].transpose(0, 2, 3, 4, 1).reshape(-1, width, cache.shape[2])

    def new_t(x, width):
        return jnp.pad(x.reshape(bs, tn, width).transpose(0, 2, 1), ((0, 0), (0, 0), (0, K_TILE - tn)))

    col_g0 = (jnp.arange(LANES) // (NSA_GROUP * tn)) * (NSA_GROUP * tn) + col_t
    sel_cols = s_sel[:, :, col_g0]
    n_hot = nbp // 2 + 8
    k_aug = 2 * LANES
    selb_rows = jnp.where(sel_cols[:, :n_hot] > 0.5, 0.0, MASK_BIAS)
    al_rows = jnp.stack([col_slope * CMP_BLOCK, col_slope,
                         -col_slope * CMP_BLOCK * (col_pos // CMP_BLOCK).astype(f32),
                         -col_slope * (col_pos % CMP_BLOCK).astype(f32)], axis=0)
    saug_t = jnp.concatenate([selb_rows, jnp.broadcast_to(al_rows, (bs, 4, LANES)),
                              jnp.zeros((bs, k_aug - n_hot - 4, LANES), f32)], axis=1).astype(bf16).transpose(0, 2, 1)
    pos_p = jnp.arange(past, dtype=jnp.int32)
    eaug_t = jnp.concatenate([(pos_p[:, None] // CMP_BLOCK == jnp.arange(n_hot)[None, :]).astype(bf16),
                              _alibi_key_cols(pos_p, k_aug - n_hot)], axis=1).T
    win_rows = state_nsa_win.shape[2]
    pos_w = past - win_rows + jnp.arange(win_rows, dtype=jnp.int32)
    ewin_t = _alibi_key_cols(pos_w, LANES).T
    waug_t = jnp.concatenate([al_rows, jnp.zeros((LANES - 4, LANES), f32)], axis=0).astype(bf16).T
    dist_w = col_pos[:, None] - pos_w[None, :]
    wmask_t = jnp.where((dist_w >= 0) & (dist_w <= WINDOW) & (pos_w[None, :] >= 0), 0.0, MASK_BIAS).astype(f32)
    j_new = jnp.arange(K_TILE)[None, :]
    ok_new = (j_new <= col_t[:, None]) & (j_new < tn)
    nbias_t = jnp.where(ok_new, -col_slope[:, None] * (col_t[:, None] - j_new).astype(f32), MASK_BIAS).astype(f32)
    sel_pages = pages_t(cache_nsa_sel, KV_COLS)
    win_t = state_nsa_win[0].transpose(0, 2, 3, 4, 1).reshape(bs, KV_COLS, win_rows)
    win_buf = state_nsa_win[0].reshape(bs, win_rows, KV_COLS)
    pps = S_SEL_PAGES if n_pages % S_SEL_PAGES == 0 else 1

    def b2(shape):
        return pl.BlockSpec((1,) + shape, lambda i, j, pt: (i,) + (0,) * len(shape))

    def c2(shape):
        return pl.BlockSpec(shape, lambda i, j, pt: (0,) * len(shape))

    out_o = jax.ShapeDtypeStruct((bs, LANES, LANES), f32)
    s_osel, s_owin = pl.pallas_call(
        functools.partial(_s_selwin_kernel, n_pages_step=pps),
        grid_spec=pltpu.PrefetchScalarGridSpec(
            num_scalar_prefetch=1, grid=(bs, n_pages // pps),
            in_specs=[pl.BlockSpec((1, KV_COLS, page), lambda i, j, pt, k=k: (pt[i, j * pps + k], 0, 0))
                      for k in range(pps)]
            + [b2((LANES, LANES)), b2((LANES, k_aug)),
               pl.BlockSpec((k_aug, pps * page), lambda i, j, pt: (0, j)),
               b2((KV_COLS, K_TILE)), c2((LANES, K_TILE)), c2((LANES, LANES)), c2((LANES, win_rows)),
               c2((LANES, win_rows)), b2((KV_COLS, win_rows)), b2((KV_COLS, K_TILE))],
            out_specs=(b2((LANES, LANES)), b2((LANES, LANES))),
            scratch_shapes=[pltpu.VMEM((LANES, 1), f32), pltpu.VMEM((LANES, 1), f32),
                            pltpu.VMEM((LANES, LANES), f32)]),
        out_shape=(out_o, out_o),
        compiler_params=_params(2),
        name="s_sel_win",
    )(page_table, *([sel_pages] * pps), qbd.transpose(0, 2, 1), saug_t, eaug_t, new_t(s_kv_sel, KV_COLS), nbias_t,
      waug_t, ewin_t, wmask_t, win_t, new_t(s_kv_win, KV_COLS))

    def nsa_heads(o_full):
        o = o_full[:, :n_cols, :].reshape(bs, NSA_KV_HEADS, NSA_GROUP, tn, NSA_KV_HEADS, HEAD_DIM)
        o = jnp.stack([o[:, gi, :, :, gi, :] for gi in range(NSA_KV_HEADS)], axis=1)
        return o.transpose(0, 3, 1, 2, 4).reshape(1, n_tok, NSA_WIDTH)

    so_cmp = nsa_heads(s_ocmp)
    so_sel = nsa_heads(s_osel)
    so_win = nsa_heads(s_owin)

    s_qsb = s_qsb_hm.reshape(SB_HEADS, bs, tn, HEAD_DIM).transpose(1, 0, 2, 3)
    qbd_sb = _block_diag_q(s_qsb, SB_HEADS)
    nmask_t = ((j_new < col_t[:, None]) & (j_new < tn)).astype(f32)
    sb_pages = pages_t(cache_sb, 2 * SB_WIDTH)
    n_head = min(S_SB_HEAD_PAGES, n_pages)
    n_rest = n_pages - n_head
    pp2 = S_SB_TAIL_PAGES if n_rest % S_SB_TAIL_PAGES == 0 else 1
    sb_page_shape = (1, 2 * SB_WIDTH, page)
    qt_sb = qbd_sb.transpose(0, 2, 1)
    tri_sb = _tri_weights(K_TILE)

    def b1(shape):
        return pl.BlockSpec((1,) + shape, lambda i, pt: (i,) + (0,) * len(shape))

    def c1(shape):
        return pl.BlockSpec(shape, lambda i, pt: (0,) * len(shape))

    sb_state = (jax.ShapeDtypeStruct((bs, LANES, SB_WIDTH), f32), jax.ShapeDtypeStruct((bs, LANES, K_TILE), f32),
                jax.ShapeDtypeStruct((bs, 8, LANES), jnp.int32))
    sb_scratch = [pltpu.VMEM((LANES, K_TILE), f32), pltpu.VMEM((LANES, SB_WIDTH), f32)]
    s_osb, sb_carry, sb_alive = pl.pallas_call(
        functools.partial(_s_sb_head_kernel, n_pages_step=n_head),
        grid_spec=pltpu.PrefetchScalarGridSpec(
            num_scalar_prefetch=1, grid=(bs,),
            in_specs=[pl.BlockSpec(sb_page_shape, lambda i, pt, k=k: (pt[i, n_pages - 1 - k], 0, 0))
                      for k in range(n_head)]
            + [b1((LANES, SB_WIDTH)), b1((2 * SB_WIDTH, K_TILE)), c1((LANES, K_TILE)), c1((2 * K_TILE, 2 * K_TILE))],
            out_specs=(b1((LANES, SB_WIDTH)), b1((LANES, K_TILE)), b1((8, LANES))),
            scratch_shapes=sb_scratch),
        out_shape=sb_state,
        compiler_params=_params(1),
        name="s_stick_breaking",
    )(page_table, *([sb_pages] * n_head), qt_sb, new_t(s_kv_sb, 2 * SB_WIDTH), nmask_t, tri_sb)
    if n_rest:
        def old_page(i, j, pt, al, k):
            return (jnp.where(al[i] > 0, pt[i, n_rest - 1 - (j * pp2 + k)], 0), 0, 0)

        s_osb = pl.pallas_call(
            functools.partial(_s_sb_tail_kernel, n_pages_step=pp2),
            grid_spec=pltpu.PrefetchScalarGridSpec(
                num_scalar_prefetch=2, grid=(bs, n_rest // pp2),
                in_specs=[pl.BlockSpec(sb_page_shape, functools.partial(old_page, k=k)) for k in range(pp2)]
                + [pl.BlockSpec((1, LANES, SB_WIDTH), lambda i, j, pt, al: (i, 0, 0)),
                   pl.BlockSpec((1, LANES, SB_WIDTH), lambda i, j, pt, al: (i, 0, 0)),
                   pl.BlockSpec((1, LANES, K_TILE), lambda i, j, pt, al: (i, 0, 0)),
                   pl.BlockSpec((2 * K_TILE, 2 * K_TILE), lambda i, j, pt, al: (0, 0))],
                out_specs=pl.BlockSpec((1, LANES, SB_WIDTH), lambda i, j, pt, al: (i, 0, 0)),
                scratch_shapes=sb_scratch),
            out_shape=sb_state[0],
            compiler_params=_params(2),
            name="s_stick_breaking_tail",
        )(page_table, sb_alive[:, 0, 0], *([sb_pages] * pp2), qt_sb, s_osb, sb_carry, tri_sb)
    o = s_osb[:, :SB_HEADS * tn, :].reshape(bs, SB_HEADS, tn, SB_HEADS, HEAD_DIM)
    so_sb = jnp.stack([o[:, hi, :, hi, :] for hi in range(SB_HEADS)], axis=0)
    so_sb = so_sb.reshape(1, SB_HEADS, n_tok, HEAD_DIM)

    y_sample = _out_ffn_call(x_sample.reshape(1, n_tok, d), so_cmp, so_sel, so_win, s_gates, so_sb, ms[2:], gexp,
                             norm_out_nsa[0], norm_out_sb[0], wout_b, norm_ffn[0], wg_b, wu_b, wd_b, norm_final,
                             256).reshape(bs, tn, d)
    all_win = jnp.concatenate([win_buf, s_kv_win.reshape(bs, tn, KV_COLS)], axis=1)
    keep = min(WINDOW, win_rows + tn)
    new_win_sample = all_win[:, win_rows + tn - keep:]

    g_n = NSA_KV_HEADS
    return (y_prompt, y_sample,
            kv_cmp.reshape(1, bp, t, 2, g_n, HEAD_DIM), kv_sel.reshape(1, bp, t, 2, g_n, HEAD_DIM),
            kv_sb.reshape(1, bp, t, 2, SB_HEADS, HEAD_DIM), new_win_prompt.reshape(1, bp, win_len, 2, g_n, HEAD_DIM),
            s_kv_cmp.reshape(1, bs, tn, 2, g_n, HEAD_DIM), s_kv_sel.reshape(1, bs, tn, 2, g_n, HEAD_DIM),
            s_kv_sb.reshape(1, bs, tn, 2, SB_HEADS, HEAD_DIM), new_win_sample.reshape(1, bs, keep, 2, g_n, HEAD_DIM))
```

```python
import functools

import jax
import jax.numpy as jnp
from jax import lax
from jax.experimental import pallas as pl
from jax.experimental.pallas import tpu as pltpu

HEAD_DIM = 64
NSA_HEADS = 8
NSA_KV_HEADS = 2
NSA_GROUP = NSA_HEADS // NSA_KV_HEADS
SB_HEADS = 8
NSA_WIDTH = NSA_HEADS * HEAD_DIM
SB_WIDTH = SB_HEADS * HEAD_DIM
KV_COLS = 2 * NSA_KV_HEADS * HEAD_DIM
CMP_BLOCK = 64
CMP_HIDDEN = 128
N_SELECT = 16
WINDOW = 512
EPS = 1e-6
NEG = -1e30
MASK_BIAS = -1e9
LANES = 128
Q_TILE = 128
K_TILE = 128
QK_SCALE = HEAD_DIM ** -0.5
VMEM_LIMIT = 56 * 1024 * 1024

f32 = jnp.float32
bf16 = jnp.bfloat16


def _dot(a, b):
    return jnp.dot(a, b, preferred_element_type=f32)


def _dot_nt(a, b):
    return lax.dot_general(a, b, (((1,), (1,)), ((), ())), preferred_element_type=f32)


def _dot_tn(a, b):
    return lax.dot_general(a, b, (((0,), (0,)), ((), ())), preferred_element_type=f32)


def _split(x):
    hi = x.astype(bf16)
    lo = (x - hi.astype(f32)).astype(bf16)
    return hi, lo


def _rms(x, g):
    return x * lax.rsqrt(jnp.mean(x * x, axis=-1, keepdims=True) + EPS) * g


def _params(n_axes, vmem=VMEM_LIMIT):
    return pltpu.CompilerParams(dimension_semantics=("arbitrary",) * n_axes, vmem_limit_bytes=vmem)


def _mods_kernel(c_ref, w_ref, b_ref, o_ref):
    c = c_ref[...]
    a = c * (1.0 / (1.0 + jnp.exp(-c)))
    a_hi, a_lo = _split(a)
    w_hi, w_lo = _split(w_ref[...])
    o_ref[...] = _dot(a_hi, w_hi) + _dot(a_lo, w_hi) + _dot(a_hi, w_lo) + b_ref[...]


def _mods_call(c_all, w_ada, b_ada):
    m, d = c_all.shape
    n = w_ada.shape[1]
    tn = 1536 if n % 1536 == 0 else n
    return pl.pallas_call(
        _mods_kernel,
        grid=(n // tn,),
        in_specs=[pl.BlockSpec((m, d), lambda j: (0, 0)),
                  pl.BlockSpec((d, tn), lambda j: (0, j)),
                  pl.BlockSpec((1, tn), lambda j: (0, j))],
        out_specs=pl.BlockSpec((m, tn), lambda j: (0, j)),
        out_shape=jax.ShapeDtypeStruct((m, n), f32),
        compiler_params=_params(1),
        name="mods",
    )(c_all, w_ada, b_ada.reshape(1, n))


_C_Q, _C_KC, _C_KS, _C_KW, _C_QSB, _C_KVSB, _C_G, _C_END = 0, 512, 768, 1024, 1280, 1792, 2816, 2944


def _proj_kernel(x_ref, sh_ref, sc_ref, g_ref, w_ref,
                 kvc_ref, kvs_ref, kvw_ref, kvsb_ref, gate_ref,
                 qn_ref, qsb_ref, ksel_ref, kwin_ref, ksb_ref):
    x = x_ref[0]
    h = _rms(x, g_ref[...]) * (1.0 + sc_ref[0]) + sh_ref[0]
    r = _dot(h.astype(bf16), w_ref[...])
    kvc_ref[0] = r[:, _C_KC:_C_KS]
    kvs_ref[0] = r[:, _C_KS:_C_KW]
    kvw_ref[0] = r[:, _C_KW:_C_QSB]
    kvsb_ref[0] = r[:, _C_KVSB:_C_G]
    gate_ref[0] = 1.0 / (1.0 + jnp.exp(-r[:, _C_G:_C_END]))
    for h_i in range(NSA_HEADS):
        lo = _C_Q + h_i * HEAD_DIM
        qn_ref[0, h_i] = (r[:, lo:lo + HEAD_DIM] * QK_SCALE).astype(bf16)
    for h_i in range(SB_HEADS):
        lo = _C_QSB + h_i * HEAD_DIM
        qsb_ref[0, h_i] = (r[:, lo:lo + HEAD_DIM] * QK_SCALE).astype(bf16)
    for j in range(2 * NSA_KV_HEADS):
        lo = _C_KS + j * HEAD_DIM
        ksel_ref[0, j] = r[:, lo:lo + HEAD_DIM].astype(bf16)
        lo = _C_KW + j * HEAD_DIM
        kwin_ref[0, j] = r[:, lo:lo + HEAD_DIM].astype(bf16)
    for j in range(2 * SB_HEADS):
        lo = _C_KVSB + j * HEAD_DIM
        ksb_ref[0, j] = r[:, lo:lo + HEAD_DIM].astype(bf16)


def _proj_call(x, shift, scale, g, w_cat, tm):
    b, t, d = x.shape
    rows = shift.shape[1]
    mod_spec = (pl.BlockSpec((1, 1, d), lambda i, j: (i, 0, 0)) if rows == 1
                else pl.BlockSpec((1, tm, d), lambda i, j: (i, j, 0)))

    def flat(c):
        return pl.BlockSpec((1, tm, c), lambda i, j: (i, j, 0))

    def heads(n):
        return pl.BlockSpec((1, n, tm, HEAD_DIM), lambda i, j: (i, 0, j, 0))

    out_shape = (
        jax.ShapeDtypeStruct((b, t, KV_COLS), f32),
        jax.ShapeDtypeStruct((b, t, KV_COLS), f32),
        jax.ShapeDtypeStruct((b, t, KV_COLS), f32),
        jax.ShapeDtypeStruct((b, t, 2 * SB_WIDTH), f32),
        jax.ShapeDtypeStruct((b, t, LANES), f32),
        jax.ShapeDtypeStruct((b, NSA_HEADS, t, HEAD_DIM), bf16),
        jax.ShapeDtypeStruct((b, SB_HEADS, t, HEAD_DIM), bf16),
        jax.ShapeDtypeStruct((b, 2 * NSA_KV_HEADS, t, HEAD_DIM), bf16),
        jax.ShapeDtypeStruct((b, 2 * NSA_KV_HEADS, t, HEAD_DIM), bf16),
        jax.ShapeDtypeStruct((b, 2 * SB_HEADS, t, HEAD_DIM), bf16),
    )
    out_specs = (flat(KV_COLS), flat(KV_COLS), flat(KV_COLS), flat(2 * SB_WIDTH), flat(LANES),
                 heads(NSA_HEADS), heads(SB_HEADS), heads(2 * NSA_KV_HEADS), heads(2 * NSA_KV_HEADS),
                 heads(2 * SB_HEADS))
    return pl.pallas_call(
        _proj_kernel,
        grid=(b, t // tm),
        in_specs=[flat(d), mod_spec, mod_spec,
                  pl.BlockSpec((1, d), lambda i, j: (0, 0)),
                  pl.BlockSpec(w_cat.shape, lambda i, j: (0, 0))],
        out_specs=out_specs,
        out_shape=out_shape,
        compiler_params=_params(2),
        name="proj",
    )(x, shift, scale, g.reshape(1, d), w_cat)


def _gelu_tanh(x):
    return 0.5 * x * (1.0 + jnp.tanh(0.7978845608028654 * (x + 0.044715 * (x * x * x))))


def _compress_kernel(x_ref, pe_ref, w1_ref, w2_ref, o_ref, acc_ref):
    k = pl.program_id(1)

    @pl.when(k == 0)
    def _():
        acc_ref[...] = jnp.zeros_like(acc_ref)

    acc_ref[...] += _dot((x_ref[...] + pe_ref[...]).astype(bf16), w1_ref[...])

    @pl.when(k == pl.num_programs(1) - 1)
    def _():
        o_ref[...] = _dot(_gelu_tanh(acc_ref[...]).astype(bf16), w2_ref[...])


def _compress_call(x_rows, pe_row, w1_big, w2_big):
    m, kdim = x_rows.shape
    tm = 256 if m % 256 == 0 else m
    tk = 4096
    n1 = w1_big.shape[1]
    n2 = w2_big.shape[1]
    return pl.pallas_call(
        _compress_kernel,
        grid=(m // tm, kdim // tk),
        in_specs=[pl.BlockSpec((tm, tk), lambda i, k: (i, k)),
                  pl.BlockSpec((1, tk), lambda i, k: (0, k)),
                  pl.BlockSpec((tk, n1), lambda i, k: (k, 0)),
                  pl.BlockSpec((n1, n2), lambda i, k: (0, 0))],
        out_specs=pl.BlockSpec((tm, n2), lambda i, k: (i, 0)),
        out_shape=jax.ShapeDtypeStruct((m, n2), f32),
        scratch_shapes=[pltpu.VMEM((tm, n1), f32)],
        compiler_params=_params(2),
        name="compress",
    )(x_rows, pe_row, w1_big, w2_big)


def _compress_weights(pe_k, w1_k, w2_k, pe_v, w1_v, w2_v):
    g = NSA_KV_HEADS
    w1 = jnp.zeros((CMP_BLOCK, 2, g, HEAD_DIM, 2, g, CMP_HIDDEN), f32)
    w2 = jnp.zeros((2, g, CMP_HIDDEN, 2, g, HEAD_DIM), f32)
    for kv, (a1, a2) in enumerate(((w1_k, w2_k), (w1_v, w2_v))):
        a1 = a1.reshape(CMP_BLOCK, HEAD_DIM, CMP_HIDDEN)
        for gi in range(g):
            w1 = w1.at[:, kv, gi, :, kv, gi, :].set(a1)
            w2 = w2.at[kv, gi, :, kv, gi, :].set(a2)
    w1 = w1.reshape(CMP_BLOCK * 2 * g * HEAD_DIM, 2 * g * CMP_HIDDEN).astype(bf16)
    w2 = w2.reshape(2 * g * CMP_HIDDEN, 2 * g * HEAD_DIM).astype(bf16)
    pe = jnp.stack([jnp.broadcast_to(pe_k[:, None, :], (CMP_BLOCK, g, HEAD_DIM)),
                    jnp.broadcast_to(pe_v[:, None, :], (CMP_BLOCK, g, HEAD_DIM))], axis=1)
    return pe.reshape(1, -1), w1, w2


def _compress_pages_kernel(pt_ref, *refs, n_pages):
    pages = refs[:n_pages]
    pe_ref, w1_ref, w2_ref, o_ref, x_ref = refs[n_pages:]
    for k in range(n_pages):
        x_ref[k] = pages[k][0]
    g_n = NSA_KV_HEADS
    n_blk = w1_ref.shape[3] // CMP_HIDDEN
    for kv in range(2):
        def rows_of(d, kv=kv):
            rows = jnp.concatenate([x_ref[:, (kv * g_n + gi) * HEAD_DIM + d, :] for gi in range(g_n)], axis=0)
            return (rows + pe_ref[kv, pl.ds(d, 1), :]).astype(bf16)

        def body(i, h, kv=kv):
            lhs = jnp.concatenate([rows_of(2 * i), rows_of(2 * i + 1)], axis=1)
            return h + _dot(lhs, w1_ref[kv, i])

        h = lax.fori_loop(0, HEAD_DIM // 2, body, jnp.zeros((g_n * n_pages, n_blk * CMP_HIDDEN), f32), unroll=4)
        act = _gelu_tanh(h).astype(bf16)
        for blk in range(n_blk):
            res = _dot(act[:, blk * CMP_HIDDEN:(blk + 1) * CMP_HIDDEN], w2_ref[kv])
            for gi in range(g_n):
                lo = (kv * g_n + gi) * HEAD_DIM
                o_ref[0, blk, :, lo:lo + HEAD_DIM] = res[gi * n_pages:(gi + 1) * n_pages]


def _compress_pages_call(page_table, pages_t, pe_k, w1_k, w2_k, pe_v, w1_v, w2_v):
    b, n_pages = page_table.shape
    page = pages_t.shape[2]
    n_blk = page // CMP_BLOCK
    eye = jnp.eye(n_blk, dtype=f32)

    def w1_layout(w1):
        w = w1.reshape(CMP_BLOCK, HEAD_DIM, CMP_HIDDEN).transpose(1, 0, 2)
        w = jnp.einsum('djc,ab->dajbc', w, eye).reshape(HEAD_DIM, page, n_blk * CMP_HIDDEN)
        return w.reshape(HEAD_DIM // 2, 2 * page, n_blk * CMP_HIDDEN)

    w1 = jnp.stack([w1_layout(w1_k), w1_layout(w1_v)]).astype(bf16)
    w2 = jnp.stack([w2_k, w2_v]).astype(bf16)
    pe = jnp.stack([jnp.tile(pe_k.T, (1, n_blk)), jnp.tile(pe_v.T, (1, n_blk))])
    out = pl.pallas_call(
        functools.partial(_compress_pages_kernel, n_pages=n_pages),
        grid_spec=pltpu.PrefetchScalarGridSpec(
            num_scalar_prefetch=1, grid=(b,),
            in_specs=[pl.BlockSpec((1, KV_COLS, page), lambda i, pt, k=k: (pt[i, k], 0, 0)) for k in range(n_pages)]
            + [pl.BlockSpec(pe.shape, lambda i, pt: (0, 0, 0)),
               pl.BlockSpec(w1.shape, lambda i, pt: (0, 0, 0, 0)),
               pl.BlockSpec(w2.shape, lambda i, pt: (0, 0, 0))],
            out_specs=pl.BlockSpec((1, n_blk, n_pages, KV_COLS), lambda i, pt: (i, 0, 0, 0)),
            scratch_shapes=[pltpu.VMEM((n_pages, KV_COLS, page), f32)]),
        out_shape=jax.ShapeDtypeStruct((b, n_blk, n_pages, KV_COLS), f32),
        compiler_params=_params(1),
        name="compress_pages",
    )(page_table, *([pages_t] * n_pages), pe, w1, w2)
    return out.transpose(0, 2, 1, 3).reshape(b, n_pages * n_blk, KV_COLS)


def _select_blocks(imp, n_idx, cur, n_rounds):
    forced = (n_idx == 0) | (n_idx == cur) | (n_idx == cur - 1)
    in_range = n_idx <= cur
    v = jnp.where(forced, 8.0, imp)
    v = jnp.where(in_range, v, -1.0)
    sel = jnp.zeros(imp.shape, f32)
    big = jnp.int32(1 << 20)
    for _ in range(n_rounds):
        mx = jnp.max(v, axis=0, keepdims=True)
        idx = jnp.min(jnp.where(v == mx, n_idx, big), axis=0, keepdims=True)
        hit = n_idx == idx
        sel = jnp.where(hit, 1.0, sel)
        v = jnp.where(hit, -2.0, v)
    return jnp.where(in_range, sel, 0.0)


CMP_TQ = 256


def _cmp_kernel(q_ref, kv_ref, ocmp_ref, selb_ref, any_ref):
    qi = pl.program_id(1)
    tq = q_ref.shape[2]
    nb = kv_ref.shape[1]
    kvb = kv_ref[0]
    n_idx = lax.broadcasted_iota(jnp.int32, (nb, tq), 0)
    pos = qi * tq + lax.broadcasted_iota(jnp.int32, (nb, tq), 1)
    dist_i = pos - (n_idx * CMP_BLOCK + (CMP_BLOCK - 1))
    valid = dist_i >= 0
    dist = dist_i.astype(f32)
    imps, outs = [], []
    for g in range(NSA_KV_HEADS):
        kc = kvb[:, g * HEAD_DIM:(g + 1) * HEAD_DIM].astype(bf16)
        vc = kvb[:, (NSA_KV_HEADS + g) * HEAD_DIM:(NSA_KV_HEADS + g + 1) * HEAD_DIM].astype(bf16)
        imp = jnp.zeros((nb, tq), f32)
        for r in range(NSA_GROUP):
            slope = 2.0 ** -(g * NSA_GROUP + r + 1)
            s = _dot_nt(kc, q_ref[0, g * NSA_GROUP + r]) - slope * dist
            s = jnp.where(valid, s, NEG)
            e = jnp.exp(s - jnp.max(s, axis=0, keepdims=True))
            p = jnp.where(valid, e / jnp.sum(e, axis=0, keepdims=True), 0.0)
            imp = imp + p
            outs.append(_dot_tn(p.astype(bf16), vc))
        imps.append(imp)
    ocmp_ref[0] = jnp.concatenate(outs, axis=1)
    g_n = NSA_KV_HEADS
    n_idx2 = jnp.concatenate([n_idx] * g_n, axis=1)
    cur2 = jnp.concatenate([pos // CMP_BLOCK] * g_n, axis=1)
    sel = _select_blocks(jnp.concatenate(imps, axis=1), n_idx2, cur2, min(N_SELECT, nb))
    for g in range(g_n):
        sel_t = sel[:, g * tq:(g + 1) * tq].T
        selb_ref[0, g] = jnp.where(sel_t > 0.5, 0.0, MASK_BIAS).astype(bf16)
        for h in range(tq // Q_TILE):
            part = sel_t[h * Q_TILE:(h + 1) * Q_TILE]
            any_ref[0, g, h] = jnp.max(part, axis=0, keepdims=True).astype(jnp.int32)


def _cmp_call(q_hm, kcvc):
    b, _, t, _ = q_hm.shape
    nb = kcvc.shape[1]
    tq = CMP_TQ
    nq = t // tq
    g_n = NSA_KV_HEADS
    return pl.pallas_call(
        _cmp_kernel,
        grid=(b, nq),
        in_specs=[pl.BlockSpec((1, NSA_HEADS, tq, HEAD_DIM), lambda i, j: (i, 0, j, 0)),
                  pl.BlockSpec((1, nb, KV_COLS), lambda i, j: (i, 0, 0))],
        out_specs=(pl.BlockSpec((1, tq, NSA_WIDTH), lambda i, j: (i, j, 0)),
                   pl.BlockSpec((1, g_n, tq, nb), lambda i, j: (i, 0, j, 0)),
                   pl.BlockSpec((1, g_n, tq // Q_TILE, 1, nb), lambda i, j: (i, 0, j, 0, 0))),
        out_shape=(jax.ShapeDtypeStruct((b, t, NSA_WIDTH), f32),
                   jax.ShapeDtypeStruct((b, g_n, t, nb), bf16),
                   jax.ShapeDtypeStruct((b, g_n, t // Q_TILE, 1, nb), jnp.int32)),
        compiler_params=_params(2),
        name="cmp_select",
    )(q_hm, kcvc)


SEL_TK = 256


def _softmax_tile(k_tile, qa, v_tile, m_ref, l_ref, acc_ref, mask):
    s = _dot_nt(k_tile, qa)
    if mask is not None:
        s = jnp.where(mask, s, MASK_BIAS)
    m_old = m_ref[...]
    m_new = jnp.maximum(m_old, jnp.max(s, axis=0, keepdims=True))
    alpha = jnp.exp(m_old - m_new)
    p = jnp.exp(s - m_new)
    l_ref[...] = alpha * l_ref[...] + jnp.sum(p, axis=0, keepdims=True)
    acc_ref[...] = alpha * acc_ref[...] + _dot_tn(v_tile, p.astype(bf16))
    m_ref[...] = m_new


def _selwin_kernel(flag_ref, q_ref, qal_ref, selb_ref, ks_ref, vs_ref, kw_ref, vw_ref, kconst_ref,
                   osel_ref, owin_ref, kaug_s, kaug_w, qaug, m_ref, l_ref, acc_ref):
    b = pl.program_id(0)
    g = pl.program_id(1)
    qi = pl.program_id(2)
    tq = q_ref.shape[2]
    tk = SEL_TK
    rows = NSA_GROUP * tq
    t_all = ks_ref.shape[2]

    @pl.when(qi == 0)
    def _():
        kaug_s[...] = kconst_ref[...]
        kaug_s[:, 0:HEAD_DIM] = ks_ref[0, 0]
        kaug_w[...] = kconst_ref[:, 0:LANES]
        kaug_w[:, 0:HEAD_DIM] = kw_ref[0, 0]

    qaug[:, 0:HEAD_DIM] = q_ref[0].reshape(rows, HEAD_DIM)
    qaug[:, HEAD_DIM:LANES] = qal_ref[0].reshape(rows, HEAD_DIM)
    sb = selb_ref[0, 0]
    for r in range(NSA_GROUP):
        qaug[r * tq:(r + 1) * tq, LANES:] = sb
    qa = qaug[...]
    qa_w = qaug[:, 0:LANES]

    q0 = qi * tq
    jd = q0 // tk
    q_pos = q0 + (lax.broadcasted_iota(jnp.int32, (tk, rows), 1) & (tq - 1))
    k_rel = lax.broadcasted_iota(jnp.int32, (tk, rows), 0)

    def reset():
        m_ref[...] = jnp.full(m_ref.shape, NEG, f32)
        l_ref[...] = jnp.zeros(l_ref.shape, f32)
        acc_ref[...] = jnp.zeros(acc_ref.shape, f32)

    def finish(o_ref):
        o_t = acc_ref[...] / l_ref[...]
        o_ref[0] = jnp.concatenate([o_t[:, r * tq:(r + 1) * tq].T for r in range(NSA_GROUP)], axis=1)

    def tile(kaug, v_ref, qmat, j, mask):
        off = pl.multiple_of(j * tk, tk)
        _softmax_tile(kaug[pl.ds(off, tk), :], qmat, v_ref[0, 0, pl.ds(off, tk), :], m_ref, l_ref, acc_ref, mask)

    reset()
    n_words = max(1, (t_all // tk) // 32)
    base = ((b * NSA_KV_HEADS + g) * (t_all // tq) + qi) * n_words

    def next_hit(j):
        def miss(t):
            return jnp.logical_and(t < jd, ((flag_ref[base + (t >> 5)] >> (t & 31)) & 1) == 0)

        return lax.while_loop(miss, lambda t: t + 1, j)

    def sel_body(j):
        j1 = next_hit(j)
        j2 = next_hit(jnp.minimum(j1 + 1, jd))

        @pl.when(j2 < jd)
        def _():
            off1 = pl.multiple_of(j1 * tk, tk)
            off2 = pl.multiple_of(j2 * tk, tk)
            k2 = jnp.concatenate([kaug_s[pl.ds(off1, tk), :], kaug_s[pl.ds(off2, tk), :]], axis=0)
            v2 = jnp.concatenate([vs_ref[0, 0, pl.ds(off1, tk), :], vs_ref[0, 0, pl.ds(off2, tk), :]], axis=0)
            _softmax_tile(k2, qa, v2, m_ref, l_ref, acc_ref, None)

        @pl.when(jnp.logical_and(j1 < jd, j2 >= jd))
        def _():
            tile(kaug_s, vs_ref, qa, j1, None)

        return j2 + 1

    lax.while_loop(lambda j: j < jd, sel_body, jnp.int32(0))
    tile(kaug_s, vs_ref, qa, jd, jd * tk + k_rel <= q_pos)

    n_wt = WINDOW // tk + 1
    start = jnp.maximum(jd - (n_wt - 1), 0)
    off_w = pl.multiple_of(start * tk, tk)
    s_w = _dot_nt(kaug_w[pl.ds(off_w, n_wt * tk), :], qa_w)
    kp = start * tk + lax.broadcasted_iota(jnp.int32, s_w.shape, 0)
    qp = q0 + (lax.broadcasted_iota(jnp.int32, s_w.shape, 1) & (tq - 1))
    s_w = jnp.where(kp <= qp, jnp.where(kp >= qp - WINDOW, s_w, MASK_BIAS), MASK_BIAS)
    p_w = jnp.exp(s_w - jnp.max(s_w, axis=0, keepdims=True))
    p_w = p_w * (1.0 / jnp.sum(p_w, axis=0, keepdims=True))
    ow_t = _dot_tn(vw_ref[0, 0, pl.ds(off_w, n_wt * tk), :], p_w.astype(bf16))
    owin_ref[0] = jnp.concatenate([ow_t[:, r * tq:(r + 1) * tq].T for r in range(NSA_GROUP)], axis=1)
    finish(osel_ref)


def _selwin_call(flags, q_hm, qal, selb, ksel_hm, kwin_hm, kconst):
    b, _, t, _ = q_hm.shape
    tq = Q_TILE
    nq = t // tq
    ka = kconst.shape[1]
    g_n = NSA_KV_HEADS

    def kv_spec(off):
        return pl.BlockSpec((1, 1, t, HEAD_DIM), lambda i, g, j, fl: (i, off + g, 0, 0))

    out_spec = pl.BlockSpec((1, tq, NSA_GROUP * HEAD_DIM), lambda i, g, j, fl: (i, j, g))
    return pl.pallas_call(
        _selwin_kernel,
        grid_spec=pltpu.PrefetchScalarGridSpec(
            num_scalar_prefetch=1, grid=(b, g_n, nq),
            in_specs=[pl.BlockSpec((1, NSA_GROUP, tq, HEAD_DIM), lambda i, g, j, fl: (i, g, j, 0)),
                      pl.BlockSpec((1, NSA_GROUP, tq, HEAD_DIM), lambda i, g, j, fl: (g, 0, j, 0)),
                      pl.BlockSpec((1, 1, tq, ka - LANES), lambda i, g, j, fl: (i, g, j, 0)),
                      kv_spec(0), kv_spec(g_n), kv_spec(0), kv_spec(g_n),
                      pl.BlockSpec((t, ka), lambda i, g, j, fl: (0, 0))],
            out_specs=(out_spec, out_spec),
            scratch_shapes=[pltpu.VMEM((t, ka), bf16), pltpu.VMEM((t, LANES), bf16),
                            pltpu.VMEM((NSA_GROUP * tq, ka), bf16),
                            pltpu.VMEM((1, NSA_GROUP * tq), f32), pltpu.VMEM((1, NSA_GROUP * tq), f32),
                            pltpu.VMEM((HEAD_DIM, NSA_GROUP * tq), f32)]),
        out_shape=(jax.ShapeDtypeStruct((b, t, NSA_WIDTH), f32), jax.ShapeDtypeStruct((b, t, NSA_WIDTH), f32)),
        compiler_params=_params(3),
        name="sel_win",
    )(flags, q_hm, qal, selb, ksel_hm, ksel_hm, kwin_hm, kwin_hm, kconst)


def _alibi_key_cols(pos, width):
    cols = jnp.stack([pos // CMP_BLOCK, pos % CMP_BLOCK, jnp.ones_like(pos), jnp.ones_like(pos)], axis=1)
    return jnp.pad(cols.astype(f32), ((0, 0), (0, width - 4))).astype(bf16)


def _alibi_query_cols(slopes, pos, width):
    s = slopes[..., None]
    a = (pos // CMP_BLOCK).astype(f32)
    r = (pos % CMP_BLOCK).astype(f32)
    cols = jnp.stack([jnp.broadcast_to(s * CMP_BLOCK, s.shape[:-1] + pos.shape),
                      jnp.broadcast_to(s, s.shape[:-1] + pos.shape),
                      -s * CMP_BLOCK * a, -s * r], axis=-1)
    pad = [(0, 0)] * (cols.ndim - 1) + [(0, width - 4)]
    return jnp.pad(cols, pad).astype(bf16)


def _slopes():
    h = jnp.arange(1, NSA_HEADS + 1, dtype=f32)
    return jnp.exp2(-8.0 * h / NSA_HEADS).reshape(NSA_KV_HEADS, NSA_GROUP)


def _neg_softplus(z):
    return -(jnp.maximum(z, 0.0) + jnp.log(1.0 + jnp.exp(-jnp.abs(z))))


SB_TILE = 512
SB_DEAD_LOG = -105.0


def _sb_kernel(q_ref, k_ref, v_ref, tri_ref, o_ref, carry_ref, acc_ref):
    qi = pl.program_id(2)
    tq = q_ref.shape[2]
    tk = tq
    sub = tri_ref.shape[1] // 2
    tri = tri_ref[...]

    def tile(off, r0, nr, nk, own):
        z = _dot_nt(q_ref[0, 0, r0:r0 + nr, :], k_ref[0, 0, pl.ds(off, nk), :])
        c = _neg_softplus(z)
        if own:
            mask = (lax.broadcasted_iota(jnp.int32, (nr, nk), 1)
                    < r0 + lax.broadcasted_iota(jnp.int32, (nr, nk), 0))
            c = jnp.where(mask, c, 0.0)
        c_hi, c_lo = _split(c)
        run = carry_ref[r0:r0 + nr, :]
        parts = [None] * (nk // sub)
        for i in reversed(range(nk // sub)):
            sl = slice(i * sub, (i + 1) * sub)
            res = _dot(jnp.concatenate([c_hi[:, sl], c_lo[:, sl]], axis=1), tri)
            parts[i] = z[:, sl] + res[:, :sub] + run
            run = run + res[:, sub:]
        a = jnp.exp(jnp.concatenate(parts, axis=1))
        if own:
            a = jnp.where(mask, a, 0.0)
        acc_ref[r0:r0 + nr, :] += _dot(a.astype(bf16), v_ref[0, 0, pl.ds(off, nk), :])
        carry_ref[r0:r0 + nr, :] = run

    carry_ref[...] = jnp.zeros(carry_ref.shape, f32)
    acc_ref[...] = jnp.zeros(acc_ref.shape, f32)
    half = tq // 2
    own_off = pl.multiple_of(qi * tk, tk)
    tile(own_off, 0, half, half, True)
    tile(own_off, half, tq - half, tk, True)

    n_back = qi * (tk // half)

    def body(state):
        tile(pl.multiple_of((n_back - 1 - state[0]) * half, half), 0, tq, half, False)
        return state[0] + 1, jnp.max(carry_ref[...])

    lax.while_loop(lambda st: jnp.logical_and(st[0] < n_back, st[1] > SB_DEAD_LOG), body,
                   (jnp.int32(0), jnp.max(carry_ref[...])))
    o_ref[0, 0] = acc_ref[...]


def _sb_call(q_hm, kv_hm, tri):
    b, h, t, _ = q_hm.shape
    tq = SB_TILE
    return pl.pallas_call(
        _sb_kernel,
        grid=(b, h, t // tq),
        in_specs=[pl.BlockSpec((1, 1, tq, HEAD_DIM), lambda i, hh, j: (i, hh, j, 0)),
                  pl.BlockSpec((1, 1, t, HEAD_DIM), lambda i, hh, j: (i, hh, 0, 0)),
                  pl.BlockSpec((1, 1, t, HEAD_DIM), lambda i, hh, j: (i, h + hh, 0, 0)),
                  pl.BlockSpec(tri.shape, lambda i, hh, j: (0, 0))],
        out_specs=pl.BlockSpec((1, 1, tq, HEAD_DIM), lambda i, hh, j: (i, hh, j, 0)),
        out_shape=jax.ShapeDtypeStruct((b, h, t, HEAD_DIM), f32),
        scratch_shapes=[pltpu.VMEM((tq, tri.shape[1] // 2), f32), pltpu.VMEM((tq, HEAD_DIM), f32)],
        compiler_params=_params(3),
        name="stick_breaking",
    )(q_hm, kv_hm, kv_hm, tri)


def _tri_weights(n):
    j = jnp.arange(n)[:, None]
    s = jnp.arange(n)[None, :]
    w = jnp.concatenate([(j >= s).astype(bf16), jnp.ones((n, n), bf16)], axis=1)
    return jnp.concatenate([w, w], axis=0)


def _out_ffn_kernel(x_ref, ocmp_ref, osel_ref, owin_ref, gate_ref, osb_ref,
                    gmsa_ref, shmlp_ref, scmlp_ref, gmlp_ref,
                    gexp_ref, nnsa_ref, nsb_ref, wout_ref, nffn_ref, wg_ref, wu_ref, wd_ref, nfin_ref,
                    y_ref, x1_ref, h2_ref, acc_ref):
    f = pl.program_id(2)

    @pl.when(f == 0)
    def _():
        g_hi, g_lo = _split(gate_ref[0])
        gx = _dot(g_hi, gexp_ref[...]) + _dot(g_lo, gexp_ref[...])
        w = NSA_WIDTH
        o_nsa = gx[:, 0:w] * ocmp_ref[0] + gx[:, w:2 * w] * osel_ref[0] + gx[:, 2 * w:3 * w] * owin_ref[0]
        o_sb = jnp.concatenate([osb_ref[0, h_i] for h_i in range(SB_HEADS)], axis=1)
        cat = jnp.concatenate([_rms(o_nsa, nnsa_ref[...]), _rms(o_sb, nsb_ref[...])], axis=1)
        x1 = x_ref[0] + gmsa_ref[0] * _dot(cat.astype(bf16), wout_ref[...])
        x1_ref[...] = x1
        h2_ref[...] = (_rms(x1, nffn_ref[...]) * (1.0 + scmlp_ref[0]) + shmlp_ref[0]).astype(bf16)
        acc_ref[...] = jnp.zeros_like(acc_ref)

    h2 = h2_ref[...]
    gt = _dot(h2, wg_ref[...])
    up = _dot(h2, wu_ref[...])
    act = gt * (1.0 / (1.0 + jnp.exp(-gt))) * up
    acc_ref[...] += _dot(act.astype(bf16), wd_ref[...])

    @pl.when(f == pl.num_programs(2) - 1)
    def _():
        y = x1_ref[...] + gmlp_ref[0] * acc_ref[...]
        y_ref[0] = _rms(y, nfin_ref[...])


def _out_ffn_call(x, ocmp, osel, owin, gates, osb_hm, mods4, gexp, nnsa, nsb, wout, nffn, wg, wu, wd, nfin, tm):
    b, t, d = x.shape
    tm = min(tm, t)
    dff = wg.shape[1]
    tf = dff // 2 if (dff // 2) % LANES == 0 else dff
    rows = mods4[0].shape[1]
    mod_spec = (pl.BlockSpec((1, 1, d), lambda i, j, f: (i, 0, 0)) if rows == 1
                else pl.BlockSpec((1, tm, d), lambda i, j, f: (i, j, 0)))

    def flat(c):
        return pl.BlockSpec((1, tm, c), lambda i, j, f: (i, j, 0))

    def const(shape):
        return pl.BlockSpec(shape, lambda i, j, f: (0,) * len(shape))

    return pl.pallas_call(
        _out_ffn_kernel,
        grid=(b, t // tm, dff // tf),
        in_specs=[flat(d), flat(NSA_WIDTH), flat(NSA_WIDTH), flat(NSA_WIDTH), flat(LANES),
                  pl.BlockSpec((1, SB_HEADS, tm, HEAD_DIM), lambda i, j, f: (i, 0, j, 0)),
                  mod_spec, mod_spec, mod_spec, mod_spec,
                  const(gexp.shape), const((1, NSA_WIDTH)), const((1, SB_WIDTH)), const(wout.shape),
                  const((1, d)),
                  pl.BlockSpec((d, tf), lambda i, j, f: (0, f)),
                  pl.BlockSpec((d, tf), lambda i, j, f: (0, f)),
                  pl.BlockSpec((tf, d), lambda i, j, f: (f, 0)),
                  const((1, d))],
        out_specs=flat(d),
        out_shape=jax.ShapeDtypeStruct((b, t, d), f32),
        scratch_shapes=[pltpu.VMEM((tm, d), f32), pltpu.VMEM((tm, d), bf16), pltpu.VMEM((tm, d), f32)],
        compiler_params=_params(3),
        name="out_ffn",
    )(x, ocmp, osel, owin, gates, osb_hm, *mods4, gexp, nnsa.reshape(1, -1), nsb.reshape(1, -1), wout,
      nffn.reshape(1, -1), wg, wu, wd, nfin.reshape(1, -1))


def _gate_expand():
    rows = jnp.arange(LANES)[:, None]
    cols = jnp.arange(3 * NSA_WIDTH)[None, :]
    return (rows == (cols // NSA_WIDTH) * NSA_HEADS + (cols % NSA_WIDTH) // HEAD_DIM).astype(bf16)


def _s_cmp_kernel(kv_ref, qbd_ref, qal_ref, pq_ref, ocmp_ref, sel_ref, *, tn):
    nbp = kv_ref.shape[1]
    kvb = kv_ref[0]
    kc = kvb[:, 0:LANES].astype(bf16)
    vc = kvb[:, LANES:2 * LANES].astype(bf16)
    n_idx = lax.broadcasted_iota(jnp.int32, (nbp, LANES), 0)
    pos = jnp.broadcast_to(pq_ref[...], (nbp, LANES))
    dist_i = pos - (n_idx * CMP_BLOCK + (CMP_BLOCK - 1))
    valid = dist_i >= 0
    slope = qal_ref[...]
    s = _dot(kc, qbd_ref[0]) - slope * dist_i.astype(f32)
    s = jnp.where(valid, s, NEG)
    e = jnp.exp(s - jnp.max(s, axis=0, keepdims=True))
    p = jnp.where(valid, e / jnp.sum(e, axis=0, keepdims=True), 0.0)
    ocmp_ref[0] = _dot_tn(p.astype(bf16), vc)
    imp = p
    for r in range(1, NSA_GROUP):
        imp = imp + pltpu.roll(p, LANES - r * tn, axis=1)
    cur = pos // CMP_BLOCK
    sel_ref[0] = _select_blocks(imp, n_idx, cur, min(N_SELECT, nbp))


S_SEL_PAGES = 16


def _row_softmax_pv(z_list, vt_list):
    m = z_list[0].max(axis=1, keepdims=True)
    for z in z_list[1:]:
        m = jnp.maximum(m, z.max(axis=1, keepdims=True))
    e_list = [jnp.exp(z - m) for z in z_list]
    l = e_list[0].sum(axis=1, keepdims=True)
    for e in e_list[1:]:
        l = l + e.sum(axis=1, keepdims=True)
    inv = 1.0 / l
    o = None
    for e, vt in zip(e_list, vt_list):
        pv = _dot_nt((e * inv).astype(bf16), vt)
        o = pv if o is None else o + pv
    return o


def _s_selwin_kernel(pt_ref, *refs, n_pages_step):
    pages = refs[:n_pages_step]
    (qt_ref, saug_ref, eaug_ref, news_ref, nbias_ref, waug_ref, ewin_ref, wmask_ref, kvw_ref, neww_ref,
     osel_ref, owin_ref, m_ref, l_ref, acc_ref) = refs[n_pages_step:]
    j = pl.program_id(1)
    qt = qt_ref[0]
    half = LANES

    def online(z, vt):
        m_old = m_ref[...]
        m_new = jnp.maximum(m_old, z.max(axis=1, keepdims=True))
        alpha = jnp.exp(m_old - m_new)
        p = jnp.exp(z - m_new)
        l_ref[...] = alpha * l_ref[...] + p.sum(axis=1, keepdims=True)
        acc_ref[...] = alpha * acc_ref[...] + _dot_nt(p.astype(bf16), vt)
        m_ref[...] = m_new

    @pl.when(j == 0)
    def _():
        kvw = kvw_ref[0]
        z_w = _dot(qt, kvw[0:half].astype(bf16)) + _dot(waug_ref[...], ewin_ref[...]) + wmask_ref[...]
        new_w = neww_ref[0]
        z_n = _dot(qt, new_w[0:half].astype(bf16)) + nbias_ref[...]
        owin_ref[0] = _row_softmax_pv([z_w, z_n], [kvw[half:].astype(bf16), new_w[half:].astype(bf16)])
        m_ref[...] = jnp.full(m_ref.shape, NEG, f32)
        l_ref[...] = jnp.zeros(l_ref.shape, f32)
        acc_ref[...] = jnp.zeros(acc_ref.shape, f32)
        new_s = news_ref[0]
        online(_dot(qt, new_s[0:half].astype(bf16)) + nbias_ref[...], new_s[half:].astype(bf16))

    kt = jnp.concatenate([p_ref[0, 0:half, :].astype(bf16) for p_ref in pages], axis=1)
    vt = jnp.concatenate([p_ref[0, half:, :].astype(bf16) for p_ref in pages], axis=1)
    online(_dot(qt, kt) + _dot(saug_ref[0], eaug_ref[...]), vt)

    @pl.when(j == pl.num_programs(1) - 1)
    def _():
        osel_ref[0] = acc_ref[...] / l_ref[...]


S_SB_HEAD_PAGES = 8
S_SB_TAIL_PAGES = 14


def _sb_pieces(qt, tri, tiles, mask, carry_ref, acc_ref):
    w = SB_WIDTH
    n_tok = tri.shape[1] // 2
    zs, ress = [], []
    for t_f in tiles:
        z = _dot(qt, t_f[0:w].astype(bf16))
        c = _neg_softplus(z)
        if mask is not None:
            c = c * mask
        c_hi, c_lo = _split(c)
        zs.append(z)
        ress.append(_dot(jnp.concatenate([c_hi, c_lo], axis=1), tri))
    run = carry_ref[...]
    acc = acc_ref[...]
    for t_f, z, res in zip(tiles, zs, ress):
        a = jnp.exp(z + res[:, :n_tok] + run)
        if mask is not None:
            a = a * mask
        acc = acc + _dot_nt(a.astype(bf16), t_f[w:].astype(bf16))
        run = run + res[:, n_tok:]
    carry_ref[...] = run
    acc_ref[...] = acc


def _s_sb_head_kernel(pt_ref, *refs, n_pages_step):
    pages = refs[:n_pages_step]
    qt_ref, new_ref, nmask_ref, tri_ref, acc_out, carry_out, alive_out, carry_ref, acc_ref = refs[n_pages_step:]
    carry_ref[...] = jnp.zeros_like(carry_ref)
    acc_ref[...] = jnp.zeros_like(acc_ref)
    qt = qt_ref[0]
    tri = tri_ref[...]
    _sb_pieces(qt, tri, [new_ref[0]], nmask_ref[...], carry_ref, acc_ref)
    _sb_pieces(qt, tri, [p_ref[0] for p_ref in pages], None, carry_ref, acc_ref)
    acc_out[0] = acc_ref[...]
    carry = carry_ref[...]
    carry_out[0] = carry
    top = jnp.max(jnp.max(carry, axis=1, keepdims=True), axis=0, keepdims=True)
    alive_out[0] = jnp.broadcast_to(jnp.where(top > SB_DEAD_LOG, 1, 0).astype(jnp.int32), alive_out.shape[1:])


def _s_sb_tail_kernel(pt_ref, alive_ref, *refs, n_pages_step):
    pages = refs[:n_pages_step]
    qt_ref, acc_in, carry_in, tri_ref, o_ref, carry_ref, acc_ref = refs[n_pages_step:]
    i = pl.program_id(0)
    j = pl.program_id(1)

    @pl.when(j == 0)
    def _():
        carry_ref[...] = carry_in[0]
        acc_ref[...] = acc_in[0]

    @pl.when(jnp.logical_and(alive_ref[i] > 0, jnp.max(carry_ref[...]) > SB_DEAD_LOG))
    def _():
        _sb_pieces(qt_ref[0], tri_ref[...], [p_ref[0] for p_ref in pages], None, carry_ref, acc_ref)

    @pl.when(j == pl.num_programs(1) - 1)
    def _():
        o_ref[0] = acc_ref[...]


def _block_diag_q(q, n_groups):
    bs, h, tn, hd = q.shape
    per = h // n_groups
    eye = jnp.eye(n_groups, dtype=q.dtype)
    qg = q.reshape(bs, n_groups, per, tn, hd)
    out = jnp.einsum('bgrtd,gk->bgdkrt', qg, eye).reshape(bs, n_groups * hd, h * tn)
    return jnp.pad(out, ((0, 0), (0, 0), (0, LANES - h * tn)))


def kernel(x_prompt, x_sample, cache_nsa_cmp, cache_nsa_sel, cache_sb, state_nsa_win, page_table,
           c_prompt, c_sample, w_ada, b_ada, norm_attn, norm_ffn, w_in, cmp_pe_k, cmp_k_w1, cmp_k_w2,
           cmp_pe_v, cmp_v_w1, cmp_v_w2, norm_out_nsa, norm_out_sb, w_out, w_gate, w_up, w_down, norm_final):
    assert w_ada.shape[0] == 1, "single-layer trunk"
    bp, t, d = x_prompt.shape
    bs, tn, _ = x_sample.shape
    n_pages = page_table.shape[1]
    page = cache_nsa_cmp.shape[2]
    past = n_pages * page
    nb = t // CMP_BLOCK
    assert nb == LANES and past % CMP_BLOCK == 0 and NSA_HEADS * tn <= LANES
    slopes = _slopes()

    wi = w_in[0]
    c0 = NSA_WIDTH + 3 * KV_COLS
    w_cat = jnp.concatenate([wi[:, :c0], wi[:, c0 + 3 * NSA_HEADS:], wi[:, c0:c0 + 3 * NSA_HEADS],
                             jnp.zeros((d, _C_END - _C_G - 3 * NSA_HEADS), f32)], axis=1).astype(bf16)
    pe_row, w1_big, w2_big = _compress_weights(cmp_pe_k[0], cmp_k_w1[0], cmp_k_w2[0],
                                               cmp_pe_v[0], cmp_v_w1[0], cmp_v_w2[0])
    gexp = _gate_expand()
    wout_b, wg_b, wu_b, wd_b = (w_out[0].astype(bf16), w_gate[0].astype(bf16), w_up[0].astype(bf16),
                                w_down[0].astype(bf16))

    mods = _mods_call(jnp.concatenate([c_prompt, c_sample], axis=0), w_ada[0], b_ada[0])
    mods_p = mods[:bp].reshape(bp, 6, 1, d)
    mods_s = jnp.repeat(mods[bp:].reshape(bs, 6, d), tn, axis=0).reshape(bs * tn, 6, d)
    mp = [mods_p[:, i] for i in range(6)]
    ms = [mods_s[:, i].reshape(1, bs * tn, d) for i in range(6)]

    (kv_cmp, kv_sel, kv_win, kv_sb, gates, q_hm, qsb_hm, ksel_hm, kwin_hm, ksb_hm) = _proj_call(
        x_prompt, mp[0], mp[1], norm_attn[0], w_cat, 512)
    kcvc = _compress_call(kv_cmp.reshape(bp * nb, CMP_BLOCK * KV_COLS), pe_row, w1_big, w2_big)
    kcvc = kcvc.reshape(bp, nb, KV_COLS)
    o_cmp, selb, any_sel = _cmp_call(q_hm, kcvc)
    nq = t // Q_TILE
    any_tile = any_sel.reshape(bp, NSA_KV_HEADS, nq, nb * CMP_BLOCK // SEL_TK, SEL_TK // CMP_BLOCK).max(axis=-1)
    n_words = max(1, any_tile.shape[-1] // 32)
    bits = any_tile.reshape(bp, NSA_KV_HEADS, nq, n_words, -1).astype(jnp.uint32)
    flags = (bits << jnp.arange(bits.shape[-1], dtype=jnp.uint32)).sum(axis=-1, dtype=jnp.uint32)
    flags = lax.bitcast_convert_type(flags, jnp.int32).reshape(-1)
    pos_t = jnp.arange(t, dtype=jnp.int32)
    kconst = jnp.concatenate([jnp.zeros((t, HEAD_DIM), bf16), _alibi_key_cols(pos_t, HEAD_DIM),
                              (pos_t[:, None] // CMP_BLOCK == jnp.arange(nb)[None, :]).astype(bf16)], axis=1)
    qal = _alibi_query_cols(slopes, pos_t, HEAD_DIM)
    o_sel, o_win = _selwin_call(flags, q_hm, qal, selb, ksel_hm, kwin_hm, kconst)
    o_sb = _sb_call(qsb_hm, ksb_hm, _tri_weights(K_TILE))
    y_prompt = _out_ffn_call(x_prompt, o_cmp, o_sel, o_win, gates, o_sb, mp[2:], gexp, norm_out_nsa[0],
                             norm_out_sb[0], wout_b, norm_ffn[0], wg_b, wu_b, wd_b, norm_final, 512)
    win_len = min(WINDOW, t)
    new_win_prompt = kv_win[:, t - win_len:]

    n_tok = bs * tn
    (s_kv_cmp, s_kv_sel, s_kv_win, s_kv_sb, s_gates, s_q_hm, s_qsb_hm, _, _, _) = _proj_call(
        x_sample.reshape(1, n_tok, d), ms[0], ms[1], norm_attn[0], w_cat, n_tok)
    nb_past = past // CMP_BLOCK
    nbs = nb_past + 1
    nbp = 2 * LANES
    assert nbs <= nbp - 8
    cmp_pages = cache_nsa_cmp[0].transpose(0, 2, 3, 4, 1).reshape(-1, KV_COLS, page)
    kc_past = _compress_pages_call(page_table, cmp_pages, cmp_pe_k[0], cmp_k_w1[0], cmp_k_w2[0],
                                   cmp_pe_v[0], cmp_v_w1[0], cmp_v_w2[0])
    tail = jnp.pad(s_kv_cmp.reshape(bs, tn, KV_COLS), ((0, 0), (0, CMP_BLOCK - tn), (0, 0)))
    kc_tail = _compress_call(tail.reshape(bs, CMP_BLOCK * KV_COLS), pe_row, w1_big, w2_big)
    s_kcvc = jnp.concatenate([kc_past.reshape(bs, nb_past, KV_COLS), kc_tail[:, None, :],
                              jnp.zeros((bs, nbp - nbs, KV_COLS), f32)], axis=1)

    pos_q = past + jnp.arange(tn, dtype=jnp.int32)
    n_cols = NSA_HEADS * tn
    col_t = jnp.arange(LANES) % tn
    col_h = jnp.minimum(jnp.arange(LANES) // tn, NSA_HEADS - 1)
    col_slope = slopes.reshape(-1)[col_h]
    col_pos = past + col_t
    s_q = s_q_hm.reshape(NSA_HEADS, bs, tn, HEAD_DIM).transpose(1, 0, 2, 3)
    qbd = _block_diag_q(s_q, NSA_KV_HEADS)

    def per_b(shape):
        return pl.BlockSpec((1,) + shape, lambda i: (i,) + (0,) * len(shape))

    def const1(shape):
        return pl.BlockSpec(shape, lambda i: (0,) * len(shape))

    s_ocmp, s_sel = pl.pallas_call(
        functools.partial(_s_cmp_kernel, tn=tn),
        grid=(bs,),
        in_specs=[per_b((nbp, KV_COLS)), per_b((LANES, LANES)), const1((1, LANES)), const1((1, LANES))],
        out_specs=(per_b((LANES, LANES)), per_b((nbp, LANES))),
        out_shape=(jax.ShapeDtypeStruct((bs, LANES, LANES), f32), jax.ShapeDtypeStruct((bs, nbp, LANES), f32)),
        compiler_params=_params(1),
        name="s_cmp_select",
    )(s_kcvc, qbd, col_slope.reshape(1, LANES).astype(f32), col_pos.reshape(1, LANES).astype(jnp.int32))

    def pages_t(cache, width):
        return cache[0].transpose(0, 2, 3, 4, 1).reshape(-1, width, cache.shape[2])

    def new_t(x, width):
        return jnp.pad(x.reshape(bs, tn, width).transpose(0, 2, 1), ((0, 0), (0, 0), (0, K_TILE - tn)))

    col_g0 = (jnp.arange(LANES) // (NSA_GROUP * tn)) * (NSA_GROUP * tn) + col_t
    sel_cols = s_sel[:, :, col_g0]
    n_hot = nbp // 2 + 8
    k_aug = 2 * LANES
    selb_rows = jnp.where(sel_cols[:, :n_hot] > 0.5, 0.0, MASK_BIAS)
    al_rows = jnp.stack([col_slope * CMP_BLOCK, col_slope,
                         -col_slope * CMP_BLOCK * (col_pos // CMP_BLOCK).astype(f32),
                         -col_slope * (col_pos % CMP_BLOCK).astype(f32)], axis=0)
    saug_t = jnp.concatenate([selb_rows, jnp.broadcast_to(al_rows, (bs, 4, LANES)),
                              jnp.zeros((bs, k_aug - n_hot - 4, LANES), f32)], axis=1).astype(bf16).transpose(0, 2, 1)
    pos_p = jnp.arange(past, dtype=jnp.int32)
    eaug_t = jnp.concatenate([(pos_p[:, None] // CMP_BLOCK == jnp.arange(n_hot)[None, :]).astype(bf16),
                              _alibi_key_cols(pos_p, k_aug - n_hot)], axis=1).T
    win_rows = state_nsa_win.shape[2]
    pos_w = past - win_rows + jnp.arange(win_rows, dtype=jnp.int32)
    ewin_t = _alibi_key_cols(pos_w, LANES).T
    waug_t = jnp.concatenate([al_rows, jnp.zeros((LANES - 4, LANES), f32)], axis=0).astype(bf16).T
    dist_w = col_pos[:, None] - pos_w[None, :]
    wmask_t = jnp.where((dist_w >= 0) & (dist_w <= WINDOW) & (pos_w[None, :] >= 0), 0.0, MASK_BIAS).astype(f32)
    j_new = jnp.arange(K_TILE)[None, :]
    ok_new = (j_new <= col_t[:, None]) & (j_new < tn)
    nbias_t = jnp.where(ok_new, -col_slope[:, None] * (col_t[:, None] - j_new).astype(f32), MASK_BIAS).astype(f32)
    sel_pages = pages_t(cache_nsa_sel, KV_COLS)
    win_t = state_nsa_win[0].transpose(0, 2, 3, 4, 1).reshape(bs, KV_COLS, win_rows)
    win_buf = state_nsa_win[0].reshape(bs, win_rows, KV_COLS)
    pps = S_SEL_PAGES if n_pages % S_SEL_PAGES == 0 else 1

    def b2(shape):
        return pl.BlockSpec((1,) + shape, lambda i, j, pt: (i,) + (0,) * len(shape))

    def c2(shape):
        return pl.BlockSpec(shape, lambda i, j, pt: (0,) * len(shape))

    out_o = jax.ShapeDtypeStruct((bs, LANES, LANES), f32)
    s_osel, s_owin = pl.pallas_call(
        functools.partial(_s_selwin_kernel, n_pages_step=pps),
        grid_spec=pltpu.PrefetchScalarGridSpec(
            num_scalar_prefetch=1, grid=(bs, n_pages // pps),
            in_specs=[pl.BlockSpec((1, KV_COLS, page), lambda i, j, pt, k=k: (pt[i, j * pps + k], 0, 0))
                      for k in range(pps)]
            + [b2((LANES, LANES)), b2((LANES, k_aug)),
               pl.BlockSpec((k_aug, pps * page), lambda i, j, pt: (0, j)),
               b2((KV_COLS, K_TILE)), c2((LANES, K_TILE)), c2((LANES, LANES)), c2((LANES, win_rows)),
               c2((LANES, win_rows)), b2((KV_COLS, win_rows)), b2((KV_COLS, K_TILE))],
            out_specs=(b2((LANES, LANES)), b2((LANES, LANES))),
            scratch_shapes=[pltpu.VMEM((LANES, 1), f32), pltpu.VMEM((LANES, 1), f32),
                            pltpu.VMEM((LANES, LANES), f32)]),
        out_shape=(out_o, out_o),
        compiler_params=_params(2),
        name="s_sel_win",
    )(page_table, *([sel_pages] * pps), qbd.transpose(0, 2, 1), saug_t, eaug_t, new_t(s_kv_sel, KV_COLS), nbias_t,
      waug_t, ewin_t, wmask_t, win_t, new_t(s_kv_win, KV_COLS))

    def nsa_heads(o_full):
        o = o_full[:, :n_cols, :].reshape(bs, NSA_KV_HEADS, NSA_GROUP, tn, NSA_KV_HEADS, HEAD_DIM)
        o = jnp.stack([o[:, gi, :, :, gi, :] for gi in range(NSA_KV_HEADS)], axis=1)
        return o.transpose(0, 3, 1, 2, 4).reshape(1, n_tok, NSA_WIDTH)

    so_cmp = nsa_heads(s_ocmp)
    so_sel = nsa_heads(s_osel)
    so_win = nsa_heads(s_owin)

    s_qsb = s_qsb_hm.reshape(SB_HEADS, bs, tn, HEAD_DIM).transpose(1, 0, 2, 3)
    qbd_sb = _block_diag_q(s_qsb, SB_HEADS)
    nmask_t = ((j_new < col_t[:, None]) & (j_new < tn)).astype(f32)
    sb_pages = pages_t(cache_sb, 2 * SB_WIDTH)
    n_head = min(S_SB_HEAD_PAGES, n_pages)
    n_rest = n_pages - n_head
    pp2 = S_SB_TAIL_PAGES if n_rest % S_SB_TAIL_PAGES == 0 else 1
    sb_page_shape = (1, 2 * SB_WIDTH, page)
    qt_sb = qbd_sb.transpose(0, 2, 1)
    tri_sb = _tri_weights(K_TILE)

    def b1(shape):
        return pl.BlockSpec((1,) + shape, lambda i, pt: (i,) + (0,) * len(shape))

    def c1(shape):
        return pl.BlockSpec(shape, lambda i, pt: (0,) * len(shape))

    sb_state = (jax.ShapeDtypeStruct((bs, LANES, SB_WIDTH), f32), jax.ShapeDtypeStruct((bs, LANES, K_TILE), f32),
                jax.ShapeDtypeStruct((bs, 8, LANES), jnp.int32))
    sb_scratch = [pltpu.VMEM((LANES, K_TILE), f32), pltpu.VMEM((LANES, SB_WIDTH), f32)]
    s_osb, sb_carry, sb_alive = pl.pallas_call(
        functools.partial(_s_sb_head_kernel, n_pages_step=n_head),
        grid_spec=pltpu.PrefetchScalarGridSpec(
            num_scalar_prefetch=1, grid=(bs,),
            in_specs=[pl.BlockSpec(sb_page_shape, lambda i, pt, k=k: (pt[i, n_pages - 1 - k], 0, 0))
                      for k in range(n_head)]
            + [b1((LANES, SB_WIDTH)), b1((2 * SB_WIDTH, K_TILE)), c1((LANES, K_TILE)), c1((2 * K_TILE, 2 * K_TILE))],
            out_specs=(b1((LANES, SB_WIDTH)), b1((LANES, K_TILE)), b1((8, LANES))),
            scratch_shapes=sb_scratch),
        out_shape=sb_state,
        compiler_params=_params(1),
        name="s_stick_breaking",
    )(page_table, *([sb_pages] * n_head), qt_sb, new_t(s_kv_sb, 2 * SB_WIDTH), nmask_t, tri_sb)
    if n_rest:
        def old_page(i, j, pt, al, k):
            return (jnp.where(al[i] > 0, pt[i, n_rest - 1 - (j * pp2 + k)], 0), 0, 0)

        s_osb = pl.pallas_call(
            functools.partial(_s_sb_tail_kernel, n_pages_step=pp2),
            grid_spec=pltpu.PrefetchScalarGridSpec(
                num_scalar_prefetch=2, grid=(bs, n_rest // pp2),
                in_specs=[pl.BlockSpec(sb_page_shape, functools.partial(old_page, k=k)) for k in range(pp2)]
                + [pl.BlockSpec((1, LANES, SB_WIDTH), lambda i, j, pt, al: (i, 0, 0)),
                   pl.BlockSpec((1, LANES, SB_WIDTH), lambda i, j, pt, al: (i, 0, 0)),
                   pl.BlockSpec((1, LANES, K_TILE), lambda i, j, pt, al: (i, 0, 0)),
                   pl.BlockSpec((2 * K_TILE, 2 * K_TILE), lambda i, j, pt, al: (0, 0))],
                out_specs=pl.BlockSpec((1, LANES, SB_WIDTH), lambda i, j, pt, al: (i, 0, 0)),
                scratch_shapes=sb_scratch),
            out_shape=sb_state[0],
            compiler_params=_params(2),
            name="s_stick_breaking_tail",
        )(page_table, sb_alive[:, 0, 0], *([sb_pages] * pp2), qt_sb, s_osb, sb_carry, tri_sb)
    o = s_osb[:, :SB_HEADS * tn, :].reshape(bs, SB_HEADS, tn, SB_HEADS, HEAD_DIM)
    so_sb = jnp.stack([o[:, hi, :, hi, :] for hi in range(SB_HEADS)], axis=0)
    so_sb = so_sb.reshape(1, SB_HEADS, n_tok, HEAD_DIM)

    y_sample = _out_ffn_call(x_sample.reshape(1, n_tok, d), so_cmp, so_sel, so_win, s_gates, so_sb, ms[2:], gexp,
                             norm_out_nsa[0], norm_out_sb[0], wout_b, norm_ffn[0], wg_b, wu_b, wd_b, norm_final,
                             256).reshape(bs, tn, d)
    all_win = jnp.concatenate([win_buf, s_kv_win.reshape(bs, tn, KV_COLS)], axis=1)
    keep = min(WINDOW, win_rows + tn)
    new_win_sample = all_win[:, win_rows + tn - keep:]

    g_n = NSA_KV_HEADS
    return (y_prompt, y_sample,
            kv_cmp.reshape(1, bp, t, 2, g_n, HEAD_DIM), kv_sel.reshape(1, bp, t, 2, g_n, HEAD_DIM),
            kv_sb.reshape(1, bp, t, 2, SB_HEADS, HEAD_DIM), new_win_prompt.reshape(1, bp, win_len, 2, g_n, HEAD_DIM),
            s_kv_cmp.reshape(1, bs, tn, 2, g_n, HEAD_DIM), s_kv_sel.reshape(1, bs, tn, 2, g_n, HEAD_DIM),
            s_kv_sb.reshape(1, bs, tn, 2, SB_HEADS, HEAD_DIM), new_win_sample.reshape(1, bs, keep, 2, g_n, HEAD_DIM))
```

```python
import functools

import jax
import jax.numpy as jnp
from jax import lax
from jax.experimental import pallas as pl
from jax.experimental.pallas import tpu as pltpu

HEAD_DIM = 64
NSA_HEADS = 8
NSA_KV_HEADS = 2
NSA_GROUP = NSA_HEADS // NSA_KV_HEADS
SB_HEADS = 8
NSA_WIDTH = NSA_HEADS * HEAD_DIM
SB_WIDTH = SB_HEADS * HEAD_DIM
KV_COLS = 2 * NSA_KV_HEADS * HEAD_DIM
CMP_BLOCK = 64
CMP_HIDDEN = 128
N_SELECT = 16
WINDOW = 512
EPS = 1e-6
NEG = -1e30
MASK_BIAS = -1e9
LANES = 128
Q_TILE = 128
K_TILE = 128
QK_SCALE = HEAD_DIM ** -0.5
VMEM_LIMIT = 56 * 1024 * 1024

f32 = jnp.float32
bf16 = jnp.bfloat16


def _dot(a, b):
    return jnp.dot(a, b, preferred_element_type=f32)


def _dot_nt(a, b):
    return lax.dot_general(a, b, (((1,), (1,)), ((), ())), preferred_element_type=f32)


def _dot_tn(a, b):
    return lax.dot_general(a, b, (((0,), (0,)), ((), ())), preferred_element_type=f32)


def _split(x):
    hi = x.astype(bf16)
    lo = (x - hi.astype(f32)).astype(bf16)
    return hi, lo


def _rms(x, g):
    return x * lax.rsqrt(jnp.mean(x * x, axis=-1, keepdims=True) + EPS) * g


def _params(n_axes, vmem=VMEM_LIMIT):
    return pltpu.CompilerParams(dimension_semantics=("arbitrary",) * n_axes, vmem_limit_bytes=vmem)


def _mods_kernel(c_ref, w_ref, b_ref, o_ref):
    c = c_ref[...]
    a = c * (1.0 / (1.0 + jnp.exp(-c)))
    a_hi, a_lo = _split(a)
    w_hi, w_lo = _split(w_ref[...])
    o_ref[...] = _dot(a_hi, w_hi) + _dot(a_lo, w_hi) + _dot(a_hi, w_lo) + b_ref[...]


def _mods_call(c_all, w_ada, b_ada):
    m, d = c_all.shape
    n = w_ada.shape[1]
    tn = 1536 if n % 1536 == 0 else n
    return pl.pallas_call(
        _mods_kernel,
        grid=(n // tn,),
        in_specs=[pl.BlockSpec((m, d), lambda j: (0, 0)),
                  pl.BlockSpec((d, tn), lambda j: (0, j)),
                  pl.BlockSpec((1, tn), lambda j: (0, j))],
        out_specs=pl.BlockSpec((m, tn), lambda j: (0, j)),
        out_shape=jax.ShapeDtypeStruct((m, n), f32),
        compiler_params=_params(1),
        name="mods",
    )(c_all, w_ada, b_ada.reshape(1, n))


_C_Q, _C_KC, _C_KS, _C_KW, _C_QSB, _C_KVSB, _C_G, _C_END = 0, 512, 768, 1024, 1280, 1792, 2816, 2944


def _proj_kernel(x_ref, sh_ref, sc_ref, g_ref, w_ref,
                 kvc_ref, kvs_ref, kvw_ref, kvsb_ref, gate_ref,
                 qn_ref, qsb_ref, ksel_ref, kwin_ref, ksb_ref):
    x = x_ref[0]
    h = _rms(x, g_ref[...]) * (1.0 + sc_ref[0]) + sh_ref[0]
    r = _dot(h.astype(bf16), w_ref[...])
    kvc_ref[0] = r[:, _C_KC:_C_KS]
    kvs_ref[0] = r[:, _C_KS:_C_KW]
    kvw_ref[0] = r[:, _C_KW:_C_QSB]
    kvsb_ref[0] = r[:, _C_KVSB:_C_G]
    gate_ref[0] = 1.0 / (1.0 + jnp.exp(-r[:, _C_G:_C_END]))
    for h_i in range(NSA_HEADS):
        lo = _C_Q + h_i * HEAD_DIM
        qn_ref[0, h_i] = (r[:, lo:lo + HEAD_DIM] * QK_SCALE).astype(bf16)
    for h_i in range(SB_HEADS):
        lo = _C_QSB + h_i * HEAD_DIM
        qsb_ref[0, h_i] = (r[:, lo:lo + HEAD_DIM] * QK_SCALE).astype(bf16)
    for j in range(2 * NSA_KV_HEADS):
        lo = _C_KS + j * HEAD_DIM
        ksel_ref[0, j] = r[:, lo:lo + HEAD_DIM].astype(bf16)
        lo = _C_KW + j * HEAD_DIM
        kwin_ref[0, j] = r[:, lo:lo + HEAD_DIM].astype(bf16)
    for j in range(2 * SB_HEADS):
        lo = _C_KVSB + j * HEAD_DIM
        ksb_ref[0, j] = r[:, lo:lo + HEAD_DIM].astype(bf16)


def _proj_call(x, shift, scale, g, w_cat, tm):
    b, t, d = x.shape
    rows = shift.shape[1]
    mod_spec = (pl.BlockSpec((1, 1, d), lambda i, j: (i, 0, 0)) if rows == 1
                else pl.BlockSpec((1, tm, d), lambda i, j: (i, j, 0)))

    def flat(c):
        return pl.BlockSpec((1, tm, c), lambda i, j: (i, j, 0))

    def heads(n):
        return pl.BlockSpec((1, n, tm, HEAD_DIM), lambda i, j: (i, 0, j, 0))

    out_shape = (
        jax.ShapeDtypeStruct((b, t, KV_COLS), f32),
        jax.ShapeDtypeStruct((b, t, KV_COLS), f32),
        jax.ShapeDtypeStruct((b, t, KV_COLS), f32),
        jax.ShapeDtypeStruct((b, t, 2 * SB_WIDTH), f32),
        jax.ShapeDtypeStruct((b, t, LANES), f32),
        jax.ShapeDtypeStruct((b, NSA_HEADS, t, HEAD_DIM), bf16),
        jax.ShapeDtypeStruct((b, SB_HEADS, t, HEAD_DIM), bf16),
        jax.ShapeDtypeStruct((b, 2 * NSA_KV_HEADS, t, HEAD_DIM), bf16),
        jax.ShapeDtypeStruct((b, 2 * NSA_KV_HEADS, t, HEAD_DIM), bf16),
        jax.ShapeDtypeStruct((b, 2 * SB_HEADS, t, HEAD_DIM), bf16),
    )
    out_specs = (flat(KV_COLS), flat(KV_COLS), flat(KV_COLS), flat(2 * SB_WIDTH), flat(LANES),
                 heads(NSA_HEADS), heads(SB_HEADS), heads(2 * NSA_KV_HEADS), heads(2 * NSA_KV_HEADS),
                 heads(2 * SB_HEADS))
    return pl.pallas_call(
        _proj_kernel,
        grid=(b, t // tm),
        in_specs=[flat(d), mod_spec, mod_spec,
                  pl.BlockSpec((1, d), lambda i, j: (0, 0)),
                  pl.BlockSpec(w_cat.shape, lambda i, j: (0, 0))],
        out_specs=out_specs,
        out_shape=out_shape,
        compiler_params=_params(2),
        name="proj",
    )(x, shift, scale, g.reshape(1, d), w_cat)


def _gelu_tanh(x):
    return 0.5 * x * (1.0 + jnp.tanh(0.7978845608028654 * (x + 0.044715 * (x * x * x))))


def _compress_kernel(x_ref, pe_ref, w1_ref, w2_ref, o_ref, acc_ref):
    k = pl.program_id(1)

    @pl.when(k == 0)
    def _():
        acc_ref[...] = jnp.zeros_like(acc_ref)

    acc_ref[...] += _dot((x_ref[...] + pe_ref[...]).astype(bf16), w1_ref[...])

    @pl.when(k == pl.num_programs(1) - 1)
    def _():
        o_ref[...] = _dot(_gelu_tanh(acc_ref[...]).astype(bf16), w2_ref[...])


def _compress_call(x_rows, pe_row, w1_big, w2_big):
    m, kdim = x_rows.shape
    tm = 256 if m % 256 == 0 else m
    tk = 4096
    n1 = w1_big.shape[1]
    n2 = w2_big.shape[1]
    return pl.pallas_call(
        _compress_kernel,
        grid=(m // tm, kdim // tk),
        in_specs=[pl.BlockSpec((tm, tk), lambda i, k: (i, k)),
                  pl.BlockSpec((1, tk), lambda i, k: (0, k)),
                  pl.BlockSpec((tk, n1), lambda i, k: (k, 0)),
                  pl.BlockSpec((n1, n2), lambda i, k: (0, 0))],
        out_specs=pl.BlockSpec((tm, n2), lambda i, k: (i, 0)),
        out_shape=jax.ShapeDtypeStruct((m, n2), f32),
        scratch_shapes=[pltpu.VMEM((tm, n1), f32)],
        compiler_params=_params(2),
        name="compress",
    )(x_rows, pe_row, w1_big, w2_big)


def _compress_weights(pe_k, w1_k, w2_k, pe_v, w1_v, w2_v):
    g = NSA_KV_HEADS
    w1 = jnp.zeros((CMP_BLOCK, 2, g, HEAD_DIM, 2, g, CMP_HIDDEN), f32)
    w2 = jnp.zeros((2, g, CMP_HIDDEN, 2, g, HEAD_DIM), f32)
    for kv, (a1, a2) in enumerate(((w1_k, w2_k), (w1_v, w2_v))):
        a1 = a1.reshape(CMP_BLOCK, HEAD_DIM, CMP_HIDDEN)
        for gi in range(g):
            w1 = w1.at[:, kv, gi, :, kv, gi, :].set(a1)
            w2 = w2.at[kv, gi, :, kv, gi, :].set(a2)
    w1 = w1.reshape(CMP_BLOCK * 2 * g * HEAD_DIM, 2 * g * CMP_HIDDEN).astype(bf16)
    w2 = w2.reshape(2 * g * CMP_HIDDEN, 2 * g * HEAD_DIM).astype(bf16)
    pe = jnp.stack([jnp.broadcast_to(pe_k[:, None, :], (CMP_BLOCK, g, HEAD_DIM)),
                    jnp.broadcast_to(pe_v[:, None, :], (CMP_BLOCK, g, HEAD_DIM))], axis=1)
    return pe.reshape(1, -1), w1, w2


def _compress_pages_kernel(pt_ref, *refs, n_pages):
    pages = refs[:n_pages]
    pe_ref, w1_ref, w2_ref, o_ref, x_ref = refs[n_pages:]
    for k in range(n_pages):
        x_ref[k] = pages[k][0]
    g_n = NSA_KV_HEADS
    n_blk = w1_ref.shape[3] // CMP_HIDDEN
    for kv in range(2):
        def rows_of(d, kv=kv):
            rows = jnp.concatenate([x_ref[:, (kv * g_n + gi) * HEAD_DIM + d, :] for gi in range(g_n)], axis=0)
            return (rows + pe_ref[kv, pl.ds(d, 1), :]).astype(bf16)

        def body(i, h, kv=kv):
            lhs = jnp.concatenate([rows_of(2 * i), rows_of(2 * i + 1)], axis=1)
            return h + _dot(lhs, w1_ref[kv, i])

        h = lax.fori_loop(0, HEAD_DIM // 2, body, jnp.zeros((g_n * n_pages, n_blk * CMP_HIDDEN), f32), unroll=4)
        act = _gelu_tanh(h).astype(bf16)
        for blk in range(n_blk):
            res = _dot(act[:, blk * CMP_HIDDEN:(blk + 1) * CMP_HIDDEN], w2_ref[kv])
            for gi in range(g_n):
                lo = (kv * g_n + gi) * HEAD_DIM
                o_ref[0, blk, :, lo:lo + HEAD_DIM] = res[gi * n_pages:(gi + 1) * n_pages]


def _compress_pages_call(page_table, pages_t, pe_k, w1_k, w2_k, pe_v, w1_v, w2_v):
    b, n_pages = page_table.shape
    page = pages_t.shape[2]
    n_blk = page // CMP_BLOCK
    eye = jnp.eye(n_blk, dtype=f32)

    def w1_layout(w1):
        w = w1.reshape(CMP_BLOCK, HEAD_DIM, CMP_HIDDEN).transpose(1, 0, 2)
        w = jnp.einsum('djc,ab->dajbc', w, eye).reshape(HEAD_DIM, page, n_blk * CMP_HIDDEN)
        return w.reshape(HEAD_DIM // 2, 2 * page, n_blk * CMP_HIDDEN)

    w1 = jnp.stack([w1_layout(w1_k), w1_layout(w1_v)]).astype(bf16)
    w2 = jnp.stack([w2_k, w2_v]).astype(bf16)
    pe = jnp.stack([jnp.tile(pe_k.T, (1, n_blk)), jnp.tile(pe_v.T, (1, n_blk))])
    out = pl.pallas_call(
        functools.partial(_compress_pages_kernel, n_pages=n_pages),
        grid_spec=pltpu.PrefetchScalarGridSpec(
            num_scalar_prefetch=1, grid=(b,),
            in_specs=[pl.BlockSpec((1, KV_COLS, page), lambda i, pt, k=k: (pt[i, k], 0, 0)) for k in range(n_pages)]
            + [pl.BlockSpec(pe.shape, lambda i, pt: (0, 0, 0)),
               pl.BlockSpec(w1.shape, lambda i, pt: (0, 0, 0, 0)),
               pl.BlockSpec(w2.shape, lambda i, pt: (0, 0, 0))],
            out_specs=pl.BlockSpec((1, n_blk, n_pages, KV_COLS), lambda i, pt: (i, 0, 0, 0)),
            scratch_shapes=[pltpu.VMEM((n_pages, KV_COLS, page), f32)]),
        out_shape=jax.ShapeDtypeStruct((b, n_blk, n_pages, KV_COLS), f32),
        compiler_params=_params(1),
        name="compress_pages",
    )(page_table, *([pages_t] * n_pages), pe, w1, w2)
    return out.transpose(0, 2, 1, 3).reshape(b, n_pages * n_blk, KV_COLS)


def _select_blocks(imp, n_idx, cur, n_rounds):
    forced = (n_idx == 0) | (n_idx == cur) | (n_idx == cur - 1)
    in_range = n_idx <= cur
    v = jnp.where(forced, 8.0, imp)
    v = jnp.where(in_range, v, -1.0)
    sel = jnp.zeros(imp.shape, f32)
    big = jnp.int32(1 << 20)
    for _ in range(n_rounds):
        mx = jnp.max(v, axis=0, keepdims=True)
        idx = jnp.min(jnp.where(v == mx, n_idx, big), axis=0, keepdims=True)
        hit = n_idx == idx
        sel = jnp.where(hit, 1.0, sel)
        v = jnp.where(hit, -2.0, v)
    return jnp.where(in_range, sel, 0.0)


CMP_TQ = 256


def _cmp_kernel(q_ref, kv_ref, ocmp_ref, selb_ref, any_ref):
    qi = pl.program_id(1)
    tq = q_ref.shape[2]
    nb = kv_ref.shape[1]
    kvb = kv_ref[0]
    n_idx = lax.broadcasted_iota(jnp.int32, (nb, tq), 0)
    pos = qi * tq + lax.broadcasted_iota(jnp.int32, (nb, tq), 1)
    dist_i = pos - (n_idx * CMP_BLOCK + (CMP_BLOCK - 1))
    valid = dist_i >= 0
    dist = dist_i.astype(f32)
    imps, outs = [], []
    for g in range(NSA_KV_HEADS):
        kc = kvb[:, g * HEAD_DIM:(g + 1) * HEAD_DIM].astype(bf16)
        vc = kvb[:, (NSA_KV_HEADS + g) * HEAD_DIM:(NSA_KV_HEADS + g + 1) * HEAD_DIM].astype(bf16)
        imp = jnp.zeros((nb, tq), f32)
        for r in range(NSA_GROUP):
            slope = 2.0 ** -(g * NSA_GROUP + r + 1)
            s = _dot_nt(kc, q_ref[0, g * NSA_GROUP + r]) - slope * dist
            s = jnp.where(valid, s, NEG)
            e = jnp.exp(s - jnp.max(s, axis=0, keepdims=True))
            p = jnp.where(valid, e / jnp.sum(e, axis=0, keepdims=True), 0.0)
            imp = imp + p
            outs.append(_dot_tn(p.astype(bf16), vc))
        imps.append(imp)
    ocmp_ref[0] = jnp.concatenate(outs, axis=1)
    g_n = NSA_KV_HEADS
    n_idx2 = jnp.concatenate([n_idx] * g_n, axis=1)
    cur2 = jnp.concatenate([pos // CMP_BLOCK] * g_n, axis=1)
    sel = _select_blocks(jnp.concatenate(imps, axis=1), n_idx2, cur2, min(N_SELECT, nb))
    for g in range(g_n):
        sel_t = sel[:, g * tq:(g + 1) * tq].T
        selb_ref[0, g] = jnp.where(sel_t > 0.5, 0.0, MASK_BIAS).astype(bf16)
        for h in range(tq // Q_TILE):
            part = sel_t[h * Q_TILE:(h + 1) * Q_TILE]
            any_ref[0, g, h] = jnp.max(part, axis=0, keepdims=True).astype(jnp.int32)


def _cmp_call(q_hm, kcvc):
    b, _, t, _ = q_hm.shape
    nb = kcvc.shape[1]
    tq = CMP_TQ
    nq = t // tq
    g_n = NSA_KV_HEADS
    return pl.pallas_call(
        _cmp_kernel,
        grid=(b, nq),
        in_specs=[pl.BlockSpec((1, NSA_HEADS, tq, HEAD_DIM), lambda i, j: (i, 0, j, 0)),
                  pl.BlockSpec((1, nb, KV_COLS), lambda i, j: (i, 0, 0))],
        out_specs=(pl.BlockSpec((1, tq, NSA_WIDTH), lambda i, j: (i, j, 0)),
                   pl.BlockSpec((1, g_n, tq, nb), lambda i, j: (i, 0, j, 0)),
                   pl.BlockSpec((1, g_n, tq // Q_TILE, 1, nb), lambda i, j: (i, 0, j, 0, 0))),
        out_shape=(jax.ShapeDtypeStruct((b, t, NSA_WIDTH), f32),
                   jax.ShapeDtypeStruct((b, g_n, t, nb), bf16),
                   jax.ShapeDtypeStruct((b, g_n, t // Q_TILE, 1, nb), jnp.int32)),
        compiler_params=_params(2),
        name="cmp_select",
    )(q_hm, kcvc)


SEL_TK = 256


def _softmax_tile(k_tile, qa, v_tile, m_ref, l_ref, acc_ref, mask):
    s = _dot_nt(k_tile, qa)
    if mask is not None:
        s = jnp.where(mask, s, MASK_BIAS)
    m_old = m_ref[...]
    m_new = jnp.maximum(m_old, jnp.max(s, axis=0, keepdims=True))
    alpha = jnp.exp(m_old - m_new)
    p = jnp.exp(s - m_new)
    l_ref[...] = alpha * l_ref[...] + jnp.sum(p, axis=0, keepdims=True)
    acc_ref[...] = alpha * acc_ref[...] + _dot_tn(v_tile, p.astype(bf16))
    m_ref[...] = m_new


def _selwin_kernel(flag_ref, q_ref, qal_ref, selb_ref, ks_ref, vs_ref, kw_ref, vw_ref, kconst_ref,
                   osel_ref, owin_ref, kaug_s, kaug_w, qaug, m_ref, l_ref, acc_ref):
    b = pl.program_id(0)
    g = pl.program_id(1)
    qi = pl.program_id(2)
    tq = q_ref.shape[2]
    tk = SEL_TK
    rows = NSA_GROUP * tq
    t_all = ks_ref.shape[2]

    @pl.when(qi == 0)
    def _():
        kaug_s[...] = kconst_ref[...]
        kaug_s[:, 0:HEAD_DIM] = ks_ref[0, 0]
        kaug_w[...] = kconst_ref[:, 0:LANES]
        kaug_w[:, 0:HEAD_DIM] = kw_ref[0, 0]

    qaug[:, 0:HEAD_DIM] = q_ref[0].reshape(rows, HEAD_DIM)
    qaug[:, HEAD_DIM:LANES] = qal_ref[0].reshape(rows, HEAD_DIM)
    sb = selb_ref[0, 0]
    for r in range(NSA_GROUP):
        qaug[r * tq:(r + 1) * tq, LANES:] = sb
    qa = qaug[...]
    qa_w = qaug[:, 0:LANES]

    q0 = qi * tq
    jd = q0 // tk
    q_pos = q0 + (lax.broadcasted_iota(jnp.int32, (tk, rows), 1) & (tq - 1))
    k_rel = lax.broadcasted_iota(jnp.int32, (tk, rows), 0)

    def reset():
        m_ref[...] = jnp.full(m_ref.shape, NEG, f32)
        l_ref[...] = jnp.zeros(l_ref.shape, f32)
        acc_ref[...] = jnp.zeros(acc_ref.shape, f32)

    def finish(o_ref):
        o_t = acc_ref[...] / l_ref[...]
        o_ref[0] = jnp.concatenate([o_t[:, r * tq:(r + 1) * tq].T for r in range(NSA_GROUP)], axis=1)

    def tile(kaug, v_ref, qmat, j, mask):
        off = pl.multiple_of(j * tk, tk)
        _softmax_tile(kaug[pl.ds(off, tk), :], qmat, v_ref[0, 0, pl.ds(off, tk), :], m_ref, l_ref, acc_ref, mask)

    reset()
    n_words = max(1, (t_all // tk) // 32)
    base = ((b * NSA_KV_HEADS + g) * (t_all // tq) + qi) * n_words

    def next_hit(j):
        def miss(t):
            return jnp.logical_and(t < jd, ((flag_ref[base + (t >> 5)] >> (t & 31)) & 1) == 0)

        return lax.while_loop(miss, lambda t: t + 1, j)

    def sel_body(j):
        j1 = next_hit(j)
        j2 = next_hit(jnp.minimum(j1 + 1, jd))

        @pl.when(j2 < jd)
        def _():
            off1 = pl.multiple_of(j1 * tk, tk)
            off2 = pl.multiple_of(j2 * tk, tk)
            k2 = jnp.concatenate([kaug_s[pl.ds(off1, tk), :], kaug_s[pl.ds(off2, tk), :]], axis=0)
            v2 = jnp.concatenate([vs_ref[0, 0, pl.ds(off1, tk), :], vs_ref[0, 0, pl.ds(off2, tk), :]], axis=0)
            _softmax_tile(k2, qa, v2, m_ref, l_ref, acc_ref, None)

        @pl.when(jnp.logical_and(j1 < jd, j2 >= jd))
        def _():
            tile(kaug_s, vs_ref, qa, j1, None)

        return j2 + 1

    lax.while_loop(lambda j: j < jd, sel_body, jnp.int32(0))
    tile(kaug_s, vs_ref, qa, jd, jd * tk + k_rel <= q_pos)

    n_wt = WINDOW // tk + 1
    start = jnp.maximum(jd - (n_wt - 1), 0)
    off_w = pl.multiple_of(start * tk, tk)
    s_w = _dot_nt(kaug_w[pl.ds(off_w, n_wt * tk), :], qa_w)
    kp = start * tk + lax.broadcasted_iota(jnp.int32, s_w.shape, 0)
    qp = q0 + (lax.broadcasted_iota(jnp.int32, s_w.shape, 1) & (tq - 1))
    s_w = jnp.where(kp <= qp, jnp.where(kp >= qp - WINDOW, s_w, MASK_BIAS), MASK_BIAS)
    p_w = jnp.exp(s_w - jnp.max(s_w, axis=0, keepdims=True))
    p_w = p_w * (1.0 / jnp.sum(p_w, axis=0, keepdims=True))
    ow_t = _dot_tn(vw_ref[0, 0, pl.ds(off_w, n_wt * tk), :], p_w.astype(bf16))
    owin_ref[0] = jnp.concatenate([ow_t[:, r * tq:(r + 1) * tq].T for r in range(NSA_GROUP)], axis=1)
    finish(osel_ref)


def _selwin_call(flags, q_hm, qal, selb, ksel_hm, kwin_hm, kconst):
    b, _, t, _ = q_hm.shape
    tq = Q_TILE
    nq = t // tq
    ka = kconst.shape[1]
    g_n = NSA_KV_HEADS

    def kv_spec(off):
        return pl.BlockSpec((1, 1, t, HEAD_DIM), lambda i, g, j, fl: (i, off + g, 0, 0))

    out_spec = pl.BlockSpec((1, tq, NSA_GROUP * HEAD_DIM), lambda i, g, j, fl: (i, j, g))
    return pl.pallas_call(
        _selwin_kernel,
        grid_spec=pltpu.PrefetchScalarGridSpec(
            num_scalar_prefetch=1, grid=(b, g_n, nq),
            in_specs=[pl.BlockSpec((1, NSA_GROUP, tq, HEAD_DIM), lambda i, g, j, fl: (i, g, j, 0)),
                      pl.BlockSpec((1, NSA_GROUP, tq, HEAD_DIM), lambda i, g, j, fl: (g, 0, j, 0)),
                      pl.BlockSpec((1, 1, tq, ka - LANES), lambda i, g, j, fl: (i, g, j, 0)),
                      kv_spec(0), kv_spec(g_n), kv_spec(0), kv_spec(g_n),
                      pl.BlockSpec((t, ka), lambda i, g, j, fl: (0, 0))],
            out_specs=(out_spec, out_spec),
            scratch_shapes=[pltpu.VMEM((t, ka), bf16), pltpu.VMEM((t, LANES), bf16),
                            pltpu.VMEM((NSA_GROUP * tq, ka), bf16),
                            pltpu.VMEM((1, NSA_GROUP * tq), f32), pltpu.VMEM((1, NSA_GROUP * tq), f32),
                            pltpu.VMEM((HEAD_DIM, NSA_GROUP * tq), f32)]),
        out_shape=(jax.ShapeDtypeStruct((b, t, NSA_WIDTH), f32), jax.ShapeDtypeStruct((b, t, NSA_WIDTH), f32)),
        compiler_params=_params(3),
        name="sel_win",
    )(flags, q_hm, qal, selb, ksel_hm, ksel_hm, kwin_hm, kwin_hm, kconst)


def _alibi_key_cols(pos, width):
    cols = jnp.stack([pos // CMP_BLOCK, pos % CMP_BLOCK, jnp.ones_like(pos), jnp.ones_like(pos)], axis=1)
    return jnp.pad(cols.astype(f32), ((0, 0), (0, width - 4))).astype(bf16)


def _alibi_query_cols(slopes, pos, width):
    s = slopes[..., None]
    a = (pos // CMP_BLOCK).astype(f32)
    r = (pos % CMP_BLOCK).astype(f32)
    cols = jnp.stack([jnp.broadcast_to(s * CMP_BLOCK, s.shape[:-1] + pos.shape),
                      jnp.broadcast_to(s, s.shape[:-1] + pos.shape),
                      -s * CMP_BLOCK * a, -s * r], axis=-1)
    pad = [(0, 0)] * (cols.ndim - 1) + [(0, width - 4)]
    return jnp.pad(cols, pad).astype(bf16)


def _slopes():
    h = jnp.arange(1, NSA_HEADS + 1, dtype=f32)
    return jnp.exp2(-8.0 * h / NSA_HEADS).reshape(NSA_KV_HEADS, NSA_GROUP)


def _neg_softplus(z):
    return -(jnp.maximum(z, 0.0) + jnp.log(1.0 + jnp.exp(-jnp.abs(z))))


SB_TILE = 512
SB_DEAD_LOG = -105.0


def _sb_kernel(q_ref, k_ref, v_ref, tri_ref, o_ref, carry_ref, acc_ref):
    qi = pl.program_id(2)
    tq = q_ref.shape[2]
    tk = tq
    sub = tri_ref.shape[1] // 2
    tri = tri_ref[...]

    def tile(off, r0, nr, nk, own):
        z = _dot_nt(q_ref[0, 0, r0:r0 + nr, :], k_ref[0, 0, pl.ds(off, nk), :])
        c = _neg_softplus(z)
        if own:
            mask = (lax.broadcasted_iota(jnp.int32, (nr, nk), 1)
                    < r0 + lax.broadcasted_iota(jnp.int32, (nr, nk), 0))
            c = jnp.where(mask, c, 0.0)
        c_hi, c_lo = _split(c)
        run = carry_ref[r0:r0 + nr, :]
        parts = [None] * (nk // sub)
        for i in reversed(range(nk // sub)):
            sl = slice(i * sub, (i + 1) * sub)
            res = _dot(jnp.concatenate([c_hi[:, sl], c_lo[:, sl]], axis=1), tri)
            parts[i] = z[:, sl] + res[:, :sub] + run
            run = run + res[:, sub:]
        a = jnp.exp(jnp.concatenate(parts, axis=1))
        if own:
            a = jnp.where(mask, a, 0.0)
        acc_ref[r0:r0 + nr, :] += _dot(a.astype(bf16), v_ref[0, 0, pl.ds(off, nk), :])
        carry_ref[r0:r0 + nr, :] = run

    carry_ref[...] = jnp.zeros(carry_ref.shape, f32)
    acc_ref[...] = jnp.zeros(acc_ref.shape, f32)
    half = tq // 2
    own_off = pl.multiple_of(qi * tk, tk)
    tile(own_off, 0, half, half, True)
    tile(own_off, half, tq - half, tk, True)

    n_back = qi * (tk // half)

    def body(state):
        tile(pl.multiple_of((n_back - 1 - state[0]) * half, half), 0, tq, half, False)
        return state[0] + 1, jnp.max(carry_ref[...])

    lax.while_loop(lambda st: jnp.logical_and(st[0] < n_back, st[1] > SB_DEAD_LOG), body,
                   (jnp.int32(0), jnp.max(carry_ref[...])))
    o_ref[0, 0] = acc_ref[...]


def _sb_call(q_hm, kv_hm, tri):
    b, h, t, _ = q_hm.shape
    tq = SB_TILE
    return pl.pallas_call(
        _sb_kernel,
        grid=(b, h, t // tq),
        in_specs=[pl.BlockSpec((1, 1, tq, HEAD_DIM), lambda i, hh, j: (i, hh, j, 0)),
                  pl.BlockSpec((1, 1, t, HEAD_DIM), lambda i, hh, j: (i, hh, 0, 0)),
                  pl.BlockSpec((1, 1, t, HEAD_DIM), lambda i, hh, j: (i, h + hh, 0, 0)),
                  pl.BlockSpec(tri.shape, lambda i, hh, j: (0, 0))],
        out_specs=pl.BlockSpec((1, 1, tq, HEAD_DIM), lambda i, hh, j: (i, hh, j, 0)),
        out_shape=jax.ShapeDtypeStruct((b, h, t, HEAD_DIM), f32),
        scratch_shapes=[pltpu.VMEM((tq, tri.shape[1] // 2), f32), pltpu.VMEM((tq, HEAD_DIM), f32)],
        compiler_params=_params(3),
        name="stick_breaking",
    )(q_hm, kv_hm, kv_hm, tri)


def _tri_weights(n):
    j = jnp.arange(n)[:, None]
    s = jnp.arange(n)[None, :]
    w = jnp.concatenate([(j >= s).astype(bf16), jnp.ones((n, n), bf16)], axis=1)
    return jnp.concatenate([w, w], axis=0)


def _out_ffn_kernel(x_ref, ocmp_ref, osel_ref, owin_ref, gate_ref, osb_ref,
                    gmsa_ref, shmlp_ref, scmlp_ref, gmlp_ref,
                    gexp_ref, nnsa_ref, nsb_ref, wout_ref, nffn_ref, wg_ref, wu_ref, wd_ref, nfin_ref,
                    y_ref, x1_ref, h2_ref, acc_ref):
    f = pl.program_id(2)

    @pl.when(f == 0)
    def _():
        g_hi, g_lo = _split(gate_ref[0])
        gx = _dot(g_hi, gexp_ref[...]) + _dot(g_lo, gexp_ref[...])
        w = NSA_WIDTH
        o_nsa = gx[:, 0:w] * ocmp_ref[0] + gx[:, w:2 * w] * osel_ref[0] + gx[:, 2 * w:3 * w] * owin_ref[0]
        o_sb = jnp.concatenate([osb_ref[0, h_i] for h_i in range(SB_HEADS)], axis=1)
        cat = jnp.concatenate([_rms(o_nsa, nnsa_ref[...]), _rms(o_sb, nsb_ref[...])], axis=1)
        x1 = x_ref[0] + gmsa_ref[0] * _dot(cat.astype(bf16), wout_ref[...])
        x1_ref[...] = x1
        h2_ref[...] = (_rms(x1, nffn_ref[...]) * (1.0 + scmlp_ref[0]) + shmlp_ref[0]).astype(bf16)
        acc_ref[...] = jnp.zeros_like(acc_ref)

    h2 = h2_ref[...]
    gt = _dot(h2, wg_ref[...])
    up = _dot(h2, wu_ref[...])
    act = gt * (1.0 / (1.0 + jnp.exp(-gt))) * up
    acc_ref[...] += _dot(act.astype(bf16), wd_ref[...])

    @pl.when(f == pl.num_programs(2) - 1)
    def _():
        y = x1_ref[...] + gmlp_ref[0] * acc_ref[...]
        y_ref[0] = _rms(y, nfin_ref[...])


def _out_ffn_call(x, ocmp, osel, owin, gates, osb_hm, mods4, gexp, nnsa, nsb, wout, nffn, wg, wu, wd, nfin, tm):
    b, t, d = x.shape
    tm = min(tm, t)
    dff = wg.shape[1]
    tf = dff // 2 if (dff // 2) % LANES == 0 else dff
    rows = mods4[0].shape[1]
    mod_spec = (pl.BlockSpec((1, 1, d), lambda i, j, f: (i, 0, 0)) if rows == 1
                else pl.BlockSpec((1, tm, d), lambda i, j, f: (i, j, 0)))

    def flat(c):
        return pl.BlockSpec((1, tm, c), lambda i, j, f: (i, j, 0))

    def const(shape):
        return pl.BlockSpec(shape, lambda i, j, f: (0,) * len(shape))

    return pl.pallas_call(
        _out_ffn_kernel,
        grid=(b, t // tm, dff // tf),
        in_specs=[flat(d), flat(NSA_WIDTH), flat(NSA_WIDTH), flat(NSA_WIDTH), flat(LANES),
                  pl.BlockSpec((1, SB_HEADS, tm, HEAD_DIM), lambda i, j, f: (i, 0, j, 0)),
                  mod_spec, mod_spec, mod_spec, mod_spec,
                  const(gexp.shape), const((1, NSA_WIDTH)), const((1, SB_WIDTH)), const(wout.shape),
                  const((1, d)),
                  pl.BlockSpec((d, tf), lambda i, j, f: (0, f)),
                  pl.BlockSpec((d, tf), lambda i, j, f: (0, f)),
                  pl.BlockSpec((tf, d), lambda i, j, f: (f, 0)),
                  const((1, d))],
        out_specs=flat(d),
        out_shape=jax.ShapeDtypeStruct((b, t, d), f32),
        scratch_shapes=[pltpu.VMEM((tm, d), f32), pltpu.VMEM((tm, d), bf16), pltpu.VMEM((tm, d), f32)],
        compiler_params=_params(3),
        name="out_ffn",
    )(x, ocmp, osel, owin, gates, osb_hm, *mods4, gexp, nnsa.reshape(1, -1), nsb.reshape(1, -1), wout,
      nffn.reshape(1, -1), wg, wu, wd, nfin.reshape(1, -1))


def _gate_expand():
    rows = jnp.arange(LANES)[:, None]
    cols = jnp.arange(3 * NSA_WIDTH)[None, :]
    return (rows == (cols // NSA_WIDTH) * NSA_HEADS + (cols % NSA_WIDTH) // HEAD_DIM).astype(bf16)


def _s_cmp_kernel(kv_ref, qbd_ref, qal_ref, pq_ref, ocmp_ref, sel_ref, *, tn):
    nbp = kv_ref.shape[1]
    kvb = kv_ref[0]
    kc = kvb[:, 0:LANES].astype(bf16)
    vc = kvb[:, LANES:2 * LANES].astype(bf16)
    n_idx = lax.broadcasted_iota(jnp.int32, (nbp, LANES), 0)
    pos = jnp.broadcast_to(pq_ref[...], (nbp, LANES))
    dist_i = pos - (n_idx * CMP_BLOCK + (CMP_BLOCK - 1))
    valid = dist_i >= 0
    slope = qal_ref[...]
    s = _dot(kc, qbd_ref[0]) - slope * dist_i.astype(f32)
    s = jnp.where(valid, s, NEG)
    e = jnp.exp(s - jnp.max(s, axis=0, keepdims=True))
    p = jnp.where(valid, e / jnp.sum(e, axis=0, keepdims=True), 0.0)
    ocmp_ref[0] = _dot_tn(p.astype(bf16), vc)
    imp = p
    for r in range(1, NSA_GROUP):
        imp = imp + pltpu.roll(p, LANES - r * tn, axis=1)
    cur = pos // CMP_BLOCK
    sel_ref[0] = _select_blocks(imp, n_idx, cur, min(N_SELECT, nbp))


S_SEL_PAGES = 32


def _row_softmax_pv(z_list, vt_list):
    m = z_list[0].max(axis=1, keepdims=True)
    for z in z_list[1:]:
        m = jnp.maximum(m, z.max(axis=1, keepdims=True))
    e_list = [jnp.exp(z - m) for z in z_list]
    l = e_list[0].sum(axis=1, keepdims=True)
    for e in e_list[1:]:
        l = l + e.sum(axis=1, keepdims=True)
    inv = 1.0 / l
    o = None
    for e, vt in zip(e_list, vt_list):
        pv = _dot_nt((e * inv).astype(bf16), vt)
        o = pv if o is None else o + pv
    return o


def _s_selwin_kernel(pt_ref, *refs, n_pages_step):
    pages = refs[:n_pages_step]
    (qt_ref, saug_ref, eaug_ref, news_ref, nbias_ref, waug_ref, ewin_ref, wmask_ref, kvw_ref, neww_ref,
     osel_ref, owin_ref, m_ref, l_ref, acc_ref) = refs[n_pages_step:]
    j = pl.program_id(1)
    qt = qt_ref[0]
    half = LANES

    def online(z, vt):
        m_old = m_ref[...]
        m_new = jnp.maximum(m_old, z.max(axis=1, keepdims=True))
        alpha = jnp.exp(m_old - m_new)
        p = jnp.exp(z - m_new)
        l_ref[...] = alpha * l_ref[...] + p.sum(axis=1, keepdims=True)
        acc_ref[...] = alpha * acc_ref[...] + _dot_nt(p.astype(bf16), vt)
        m_ref[...] = m_new

    @pl.when(j == 0)
    def _():
        kvw = kvw_ref[0]
        z_w = _dot(qt, kvw[0:half].astype(bf16)) + _dot(waug_ref[...], ewin_ref[...]) + wmask_ref[...]
        new_w = neww_ref[0]
        z_n = _dot(qt, new_w[0:half].astype(bf16)) + nbias_ref[...]
        owin_ref[0] = _row_softmax_pv([z_w, z_n], [kvw[half:].astype(bf16), new_w[half:].astype(bf16)])
        m_ref[...] = jnp.full(m_ref.shape, NEG, f32)
        l_ref[...] = jnp.zeros(l_ref.shape, f32)
        acc_ref[...] = jnp.zeros(acc_ref.shape, f32)
        new_s = news_ref[0]
        online(_dot(qt, new_s[0:half].astype(bf16)) + nbias_ref[...], new_s[half:].astype(bf16))

    kt = jnp.concatenate([p_ref[0, 0:half, :].astype(bf16) for p_ref in pages], axis=1)
    vt = jnp.concatenate([p_ref[0, half:, :].astype(bf16) for p_ref in pages], axis=1)
    online(_dot(qt, kt) + _dot(saug_ref[0], eaug_ref[...]), vt)

    @pl.when(j == pl.num_programs(1) - 1)
    def _():
        osel_ref[0] = acc_ref[...] / l_ref[...]


S_SB_HEAD_PAGES = 8
S_SB_TAIL_PAGES = 28


def _sb_pieces(qt, tri, tiles, mask, carry_ref, acc_ref):
    w = SB_WIDTH
    n_tok = tri.shape[1] // 2
    zs, ress = [], []
    for t_f in tiles:
        z = _dot(qt, t_f[0:w].astype(bf16))
        c = _neg_softplus(z)
        if mask is not None:
            c = c * mask
        c_hi, c_lo = _split(c)
        zs.append(z)
        ress.append(_dot(jnp.concatenate([c_hi, c_lo], axis=1), tri))
    run = carry_ref[...]
    acc = acc_ref[...]
    for t_f, z, res in zip(tiles, zs, ress):
        a = jnp.exp(z + res[:, :n_tok] + run)
        if mask is not None:
            a = a * mask
        acc = acc + _dot_nt(a.astype(bf16), t_f[w:].astype(bf16))
        run = run + res[:, n_tok:]
    carry_ref[...] = run
    acc_ref[...] = acc


def _s_sb_head_kernel(pt_ref, *refs, n_pages_step):
    pages = refs[:n_pages_step]
    qt_ref, new_ref, nmask_ref, tri_ref, acc_out, carry_out, alive_out, carry_ref, acc_ref = refs[n_pages_step:]
    carry_ref[...] = jnp.zeros_like(carry_ref)
    acc_ref[...] = jnp.zeros_like(acc_ref)
    qt = qt_ref[0]
    tri = tri_ref[...]
    _sb_pieces(qt, tri, [new_ref[0]], nmask_ref[...], carry_ref, acc_ref)
    _sb_pieces(qt, tri, [p_ref[0] for p_ref in pages], None, carry_ref, acc_ref)
    acc_out[0] = acc_ref[...]
    carry = carry_ref[...]
    carry_out[0] = carry
    top = jnp.max(jnp.max(carry, axis=1, keepdims=True), axis=0, keepdims=True)
    alive_out[0] = jnp.broadcast_to(jnp.where(top > SB_DEAD_LOG, 1, 0).astype(jnp.int32), alive_out.shape[1:])


def _s_sb_tail_kernel(pt_ref, alive_ref, *refs, n_pages_step):
    pages = refs[:n_pages_step]
    qt_ref, acc_in, carry_in, tri_ref, o_ref, carry_ref, acc_ref = refs[n_pages_step:]
    i = pl.program_id(0)
    j = pl.program_id(1)

    @pl.when(j == 0)
    def _():
        carry_ref[...] = carry_in[0]
        acc_ref[...] = acc_in[0]

    @pl.when(jnp.logical_and(alive_ref[i] > 0, jnp.max(carry_ref[...]) > SB_DEAD_LOG))
    def _():
        _sb_pieces(qt_ref[0], tri_ref[...], [p_ref[0] for p_ref in pages], None, carry_ref, acc_ref)

    @pl.when(j == pl.num_programs(1) - 1)
    def _():
        o_ref[0] = acc_ref[...]


def _block_diag_q(q, n_groups):
    bs, h, tn, hd = q.shape
    per = h // n_groups
    eye = jnp.eye(n_groups, dtype=q.dtype)
    qg = q.reshape(bs, n_groups, per, tn, hd)
    out = jnp.einsum('bgrtd,gk->bgdkrt', qg, eye).reshape(bs, n_groups * hd, h * tn)
    return jnp.pad(out, ((0, 0), (0, 0), (0, LANES - h * tn)))


def kernel(x_prompt, x_sample, cache_nsa_cmp, cache_nsa_sel, cache_sb, state_nsa_win, page_table,
           c_prompt, c_sample, w_ada, b_ada, norm_attn, norm_ffn, w_in, cmp_pe_k, cmp_k_w1, cmp_k_w2,
           cmp_pe_v, cmp_v_w1, cmp_v_w2, norm_out_nsa, norm_out_sb, w_out, w_gate, w_up, w_down, norm_final):
    assert w_ada.shape[0] == 1, "single-layer trunk"
    bp, t, d = x_prompt.shape
    bs, tn, _ = x_sample.shape
    n_pages = page_table.shape[1]
    page = cache_nsa_cmp.shape[2]
    past = n_pages * page
    nb = t // CMP_BLOCK
    assert nb == LANES and past % CMP_BLOCK == 0 and NSA_HEADS * tn <= LANES
    slopes = _slopes()

    wi = w_in[0]
    c0 = NSA_WIDTH + 3 * KV_COLS
    w_cat = jnp.concatenate([wi[:, :c0], wi[:, c0 + 3 * NSA_HEADS:], wi[:, c0:c0 + 3 * NSA_HEADS],
                             jnp.zeros((d, _C_END - _C_G - 3 * NSA_HEADS), f32)], axis=1).astype(bf16)
    pe_row, w1_big, w2_big = _compress_weights(cmp_pe_k[0], cmp_k_w1[0], cmp_k_w2[0],
                                               cmp_pe_v[0], cmp_v_w1[0], cmp_v_w2[0])
    gexp = _gate_expand()
    wout_b, wg_b, wu_b, wd_b = (w_out[0].astype(bf16), w_gate[0].astype(bf16), w_up[0].astype(bf16),
                                w_down[0].astype(bf16))

    mods = _mods_call(jnp.concatenate([c_prompt, c_sample], axis=0), w_ada[0], b_ada[0])
    mods_p = mods[:bp].reshape(bp, 6, 1, d)
    mods_s = jnp.repeat(mods[bp:].reshape(bs, 6, d), tn, axis=0).reshape(bs * tn, 6, d)
    mp = [mods_p[:, i] for i in range(6)]
    ms = [mods_s[:, i].reshape(1, bs * tn, d) for i in range(6)]

    (kv_cmp, kv_sel, kv_win, kv_sb, gates, q_hm, qsb_hm, ksel_hm, kwin_hm, ksb_hm) = _proj_call(
        x_prompt, mp[0], mp[1], norm_attn[0], w_cat, 512)
    kcvc = _compress_call(kv_cmp.reshape(bp * nb, CMP_BLOCK * KV_COLS), pe_row, w1_big, w2_big)
    kcvc = kcvc.reshape(bp, nb, KV_COLS)
    o_cmp, selb, any_sel = _cmp_call(q_hm, kcvc)
    nq = t // Q_TILE
    any_tile = any_sel.reshape(bp, NSA_KV_HEADS, nq, nb * CMP_BLOCK // SEL_TK, SEL_TK // CMP_BLOCK).max(axis=-1)
    n_words = max(1, any_tile.shape[-1] // 32)
    bits = any_tile.reshape(bp, NSA_KV_HEADS, nq, n_words, -1).astype(jnp.uint32)
    flags = (bits << jnp.arange(bits.shape[-1], dtype=jnp.uint32)).sum(axis=-1, dtype=jnp.uint32)
    flags = lax.bitcast_convert_type(flags, jnp.int32).reshape(-1)
    pos_t = jnp.arange(t, dtype=jnp.int32)
    kconst = jnp.concatenate([jnp.zeros((t, HEAD_DIM), bf16), _alibi_key_cols(pos_t, HEAD_DIM),
                              (pos_t[:, None] // CMP_BLOCK == jnp.arange(nb)[None, :]).astype(bf16)], axis=1)
    qal = _alibi_query_cols(slopes, pos_t, HEAD_DIM)
    o_sel, o_win = _selwin_call(flags, q_hm, qal, selb, ksel_hm, kwin_hm, kconst)
    o_sb = _sb_call(qsb_hm, ksb_hm, _tri_weights(K_TILE))
    y_prompt = _out_ffn_call(x_prompt, o_cmp, o_sel, o_win, gates, o_sb, mp[2:], gexp, norm_out_nsa[0],
                             norm_out_sb[0], wout_b, norm_ffn[0], wg_b, wu_b, wd_b, norm_final, 512)
    win_len = min(WINDOW, t)
    new_win_prompt = kv_win[:, t - win_len:]

    n_tok = bs * tn
    (s_kv_cmp, s_kv_sel, s_kv_win, s_kv_sb, s_gates, s_q_hm, s_qsb_hm, _, _, _) = _proj_call(
        x_sample.reshape(1, n_tok, d), ms[0], ms[1], norm_attn[0], w_cat, n_tok)
    nb_past = past // CMP_BLOCK
    nbs = nb_past + 1
    nbp = 2 * LANES
    assert nbs <= nbp - 8
    cmp_pages = cache_nsa_cmp[0].transpose(0, 2, 3, 4, 1).reshape(-1, KV_COLS, page)
    kc_past = _compress_pages_call(page_table, cmp_pages, cmp_pe_k[0], cmp_k_w1[0], cmp_k_w2[0],
                                   cmp_pe_v[0], cmp_v_w1[0], cmp_v_w2[0])
    tail = jnp.pad(s_kv_cmp.reshape(bs, tn, KV_COLS), ((0, 0), (0, CMP_BLOCK - tn), (0, 0)))
    kc_tail = _compress_call(tail.reshape(bs, CMP_BLOCK * KV_COLS), pe_row, w1_big, w2_big)
    s_kcvc = jnp.concatenate([kc_past.reshape(bs, nb_past, KV_COLS), kc_tail[:, None, :],
                              jnp.zeros((bs, nbp - nbs, KV_COLS), f32)], axis=1)

    pos_q = past + jnp.arange(tn, dtype=jnp.int32)
    n_cols = NSA_HEADS * tn
    col_t = jnp.arange(LANES) % tn
    col_h = jnp.minimum(jnp.arange(LANES) // tn, NSA_HEADS - 1)
    col_slope = slopes.reshape(-1)[col_h]
    col_pos = past + col_t
    s_q = s_q_hm.reshape(NSA_HEADS, bs, tn, HEAD_DIM).transpose(1, 0, 2, 3)
    qbd = _block_diag_q(s_q, NSA_KV_HEADS)

    def per_b(shape):
        return pl.BlockSpec((1,) + shape, lambda i: (i,) + (0,) * len(shape))

    def const1(shape):
        return pl.BlockSpec(shape, lambda i: (0,) * len(shape))

    s_ocmp, s_sel = pl.pallas_call(
        functools.partial(_s_cmp_kernel, tn=tn),
        grid=(bs,),
        in_specs=[per_b((nbp, KV_COLS)), per_b((LANES, LANES)), const1((1, LANES)), const1((1, LANES))],
        out_specs=(per_b((LANES, LANES)), per_b((nbp, LANES))),
        out_shape=(jax.ShapeDtypeStruct((bs, LANES, LANES), f32), jax.ShapeDtypeStruct((bs, nbp, LANES), f32)),
        compiler_params=_params(1),
        name="s_cmp_select",
    )(s_kcvc, qbd, col_slope.reshape(1, LANES).astype(f32), col_pos.reshape(1, LANES).astype(jnp.int32))

    def pages_t(cache, width):
        return cache[0].transpose(0, 2, 3, 4, 1).reshape(-1, width, cache.shape[2])

    def new_t(x, width):
        return jnp.pad(x.reshape(bs, tn, width).transpose(0, 2, 1), ((0, 0), (0, 0), (0, K_TILE - tn)))

    col_g0 = (jnp.arange(LANES) // (NSA_GROUP * tn)) * (NSA_GROUP * tn) + col_t
    sel_cols = s_sel[:, :, col_g0]
    n_hot = nbp // 2 + 8
    k_aug = 2 * LANES
    selb_rows = jnp.where(sel_cols[:, :n_hot] > 0.5, 0.0, MASK_BIAS)
    al_rows = jnp.stack([col_slope * CMP_BLOCK, col_slope,
                         -col_slope * CMP_BLOCK * (col_pos // CMP_BLOCK).astype(f32),
                         -col_slope * (col_pos % CMP_BLOCK).astype(f32)], axis=0)
    saug_t = jnp.concatenate([selb_rows, jnp.broadcast_to(al_rows, (bs, 4, LANES)),
                              jnp.zeros((bs, k_aug - n_hot - 4, LANES), f32)], axis=1).astype(bf16).transpose(0, 2, 1)
    pos_p = jnp.arange(past, dtype=jnp.int32)
    eaug_t = jnp.concatenate([(pos_p[:, None] // CMP_BLOCK == jnp.arange(n_hot)[None, :]).astype(bf16),
                              _alibi_key_cols(pos_p, k_aug - n_hot)], axis=1).T
    win_rows = state_nsa_win.shape[2]
    pos_w = past - win_rows + jnp.arange(win_rows, dtype=jnp.int32)
    ewin_t = _alibi_key_cols(pos_w, LANES).T
    waug_t = jnp.concatenate([al_rows, jnp.zeros((LANES - 4, LANES), f32)], axis=0).astype(bf16).T
    dist_w = col_pos[:, None] - pos_w[None, :]
    wmask_t = jnp.where((dist_w >= 0) & (dist_w <= WINDOW) & (pos_w[None, :] >= 0), 0.0, MASK_BIAS).astype(f32)
    j_new = jnp.arange(K_TILE)[None, :]
    ok_new = (j_new <= col_t[:, None]) & (j_new < tn)
    nbias_t = jnp.where(ok_new, -col_slope[:, None] * (col_t[:, None] - j_new).astype(f32), MASK_BIAS).astype(f32)
    sel_pages = pages_t(cache_nsa_sel, KV_COLS)
    win_t = state_nsa_win[0].transpose(0, 2, 3, 4, 1).reshape(bs, KV_COLS, win_rows)
    win_buf = state_nsa_win[0].reshape(bs, win_rows, KV_COLS)
    pps = S_SEL_PAGES if n_pages % S_SEL_PAGES == 0 else 1

    def b2(shape):
        return pl.BlockSpec((1,) + shape, lambda i, j, pt: (i,) + (0,) * len(shape))

    def c2(shape):
        return pl.BlockSpec(shape, lambda i, j, pt: (0,) * len(shape))

    out_o = jax.ShapeDtypeStruct((bs, LANES, LANES), f32)
    s_osel, s_owin = pl.pallas_call(
        functools.partial(_s_selwin_kernel, n_pages_step=pps),
        grid_spec=pltpu.PrefetchScalarGridSpec(
            num_scalar_prefetch=1, grid=(bs, n_pages // pps),
            in_specs=[pl.BlockSpec((1, KV_COLS, page), lambda i, j, pt, k=k: (pt[i, j * pps + k], 0, 0))
                      for k in range(pps)]
            + [b2((LANES, LANES)), b2((LANES, k_aug)),
               pl.BlockSpec((k_aug, pps * page), lambda i, j, pt: (0, j)),
               b2((KV_COLS, K_TILE)), c2((LANES, K_TILE)), c2((LANES, LANES)), c2((LANES, win_rows)),
               c2((LANES, win_rows)), b2((KV_COLS, win_rows)), b2((KV_COLS, K_TILE))],
            out_specs=(b2((LANES, LANES)), b2((LANES, LANES))),
            scratch_shapes=[pltpu.VMEM((LANES, 1), f32), pltpu.VMEM((LANES, 1), f32),
                            pltpu.VMEM((LANES, LANES), f32)]),
        out_shape=(out_o, out_o),
        compiler_params=_params(2),
        name="s_sel_win",
    )(page_table, *([sel_pages] * pps), qbd.transpose(0, 2, 1), saug_t, eaug_t, new_t(s_kv_sel, KV_COLS), nbias_t,
      waug_t, ewin_t, wmask_t, win_t, new_t(s_kv_win, KV_COLS))

    def nsa_heads(o_full):
        o = o_full[:, :n_cols, :].reshape(bs, NSA_KV_HEADS, NSA_GROUP, tn, NSA_KV_HEADS, HEAD_DIM)
        o = jnp.stack([o[:, gi, :, :, gi, :] for gi in range(NSA_KV_HEADS)], axis=1)
        return o.transpose(0, 3, 1, 2, 4).reshape(1, n_tok, NSA_WIDTH)

    so_cmp = nsa_heads(s_ocmp)
    so_sel = nsa_heads(s_osel)
    so_win = nsa_heads(s_owin)

    s_qsb = s_qsb_hm.reshape(SB_HEADS, bs, tn, HEAD_DIM).transpose(1, 0, 2, 3)
    qbd_sb = _block_diag_q(s_qsb, SB_HEADS)
    nmask_t = ((j_new < col_t[:, None]) & (j_new < tn)).astype(f32)
    sb_pages = pages_t(cache_sb, 2 * SB_WIDTH)
    n_head = min(S_SB_HEAD_PAGES, n_pages)
    n_rest = n_pages - n_head
    pp2 = S_SB_TAIL_PAGES if n_rest % S_SB_TAIL_PAGES == 0 else 1
    sb_page_shape = (1, 2 * SB_WIDTH, page)
    qt_sb = qbd_sb.transpose(0, 2, 1)
    tri_sb = _tri_weights(K_TILE)

    def b1(shape):
        return pl.BlockSpec((1,) + shape, lambda i, pt: (i,) + (0,) * len(shape))

    def c1(shape):
        return pl.BlockSpec(shape, lambda i, pt: (0,) * len(shape))

    sb_state = (jax.ShapeDtypeStruct((bs, LANES, SB_WIDTH), f32), jax.ShapeDtypeStruct((bs, LANES, K_TILE), f32),
                jax.ShapeDtypeStruct((bs, 8, LANES), jnp.int32))
    sb_scratch = [pltpu.VMEM((LANES, K_TILE), f32), pltpu.VMEM((LANES, SB_WIDTH), f32)]
    s_osb, sb_carry, sb_alive = pl.pallas_call(
        functools.partial(_s_sb_head_kernel, n_pages_step=n_head),
        grid_spec=pltpu.PrefetchScalarGridSpec(
            num_scalar_prefetch=1, grid=(bs,),
            in_specs=[pl.BlockSpec(sb_page_shape, lambda i, pt, k=k: (pt[i, n_pages - 1 - k], 0, 0))
                      for k in range(n_head)]
            + [b1((LANES, SB_WIDTH)), b1((2 * SB_WIDTH, K_TILE)), c1((LANES, K_TILE)), c1((2 * K_TILE, 2 * K_TILE))],
            out_specs=(b1((LANES, SB_WIDTH)), b1((LANES, K_TILE)), b1((8, LANES))),
            scratch_shapes=sb_scratch),
        out_shape=sb_state,
        compiler_params=_params(1),
        name="s_stick_breaking",
    )(page_table, *([sb_pages] * n_head), qt_sb, new_t(s_kv_sb, 2 * SB_WIDTH), nmask_t, tri_sb)
    if n_rest:
        def old_page(i, j, pt, al, k):
            return (jnp.where(al[i] > 0, pt[i, n_rest - 1 - (j * pp2 + k)], 0), 0, 0)

        s_osb = pl.pallas_call(
            functools.partial(_s_sb_tail_kernel, n_pages_step=pp2),
            grid_spec=pltpu.PrefetchScalarGridSpec(
                num_scalar_prefetch=2, grid=(bs, n_rest // pp2),
                in_specs=[pl.BlockSpec(sb_page_shape, functools.partial(old_page, k=k)) for k in range(pp2)]
                + [pl.BlockSpec((1, LANES, SB_WIDTH), lambda i, j, pt, al: (i, 0, 0)),
                   pl.BlockSpec((1, LANES, SB_WIDTH), lambda i, j, pt, al: (i, 0, 0)),
                   pl.BlockSpec((1, LANES, K_TILE), lambda i, j, pt, al: (i, 0, 0)),
                   pl.BlockSpec((2 * K_TILE, 2 * K_TILE), lambda i, j, pt, al: (0, 0))],
                out_specs=pl.BlockSpec((1, LANES, SB_WIDTH), lambda i, j, pt, al: (i, 0, 0)),
                scratch_shapes=sb_scratch),
            out_shape=sb_state[0],
            compiler_params=_params(2),
            name="s_stick_breaking_tail",
        )(page_table, sb_alive[:, 0, 0], *([sb_pages] * pp2), qt_sb, s_osb, sb_carry, tri_sb)
    o = s_osb[:, :SB_HEADS * tn, :].reshape(bs, SB_HEADS, tn, SB_HEADS, HEAD_DIM)
    so_sb = jnp.stack([o[:, hi, :, hi, :] for hi in range(SB_HEADS)], axis=0)
    so_sb = so_sb.reshape(1, SB_HEADS, n_tok, HEAD_DIM)

    y_sample = _out_ffn_call(x_sample.reshape(1, n_tok, d), so_cmp, so_sel, so_win, s_gates, so_sb, ms[2:], gexp,
                             norm_out_nsa[0], norm_out_sb[0], wout_b, norm_ffn[0], wg_b, wu_b, wd_b, norm_final,
                             256).reshape(bs, tn, d)
    all_win = jnp.concatenate([win_buf, s_kv_win.reshape(bs, tn, KV_COLS)], axis=1)
    keep = min(WINDOW, win_rows + tn)
    new_win_sample = all_win[:, win_rows + tn - keep:]

    g_n = NSA_KV_HEADS
    return (y_prompt, y_sample,
            kv_cmp.reshape(1, bp, t, 2, g_n, HEAD_DIM), kv_sel.reshape(1, bp, t, 2, g_n, HEAD_DIM),
            kv_sb.reshape(1, bp, t, 2, SB_HEADS, HEAD_DIM), new_win_prompt.reshape(1, bp, win_len, 2, g_n, HEAD_DIM),
            s_kv_cmp.reshape(1, bs, tn, 2, g_n, HEAD_DIM), s_kv_sel.reshape(1, bs, tn, 2, g_n, HEAD_DIM),
            s_kv_sb.reshape(1, bs, tn, 2, SB_HEADS, HEAD_DIM), new_win_sample.reshape(1, bs, keep, 2, g_n, HEAD_DIM))
```
